```python
import jax
import jax.numpy as jnp
from jax import lax
import numpy as np

D_MODEL = 1024
BATCH = 2
SEQ = 8192
DEPTH = 2
DEC_BATCH = 32
DEC_SEQ = 1
PAST_LEN = 16384
PAGE_SIZE = 128

HEAD_DIM = 64
GLA_HEADS = 4
GLA_LOWRANK = 16
GLA_TAU = 16.0
GLA_CHUNK = 64
NSA_HEADS = 8
NSA_KV_HEADS = 2
NSA_GROUP = NSA_HEADS // NSA_KV_HEADS
CMP_BLOCK = 32
CMP_STRIDE = 16
CMP_HIDDEN = 64
SEL_BLOCK = 64
SEL_RATIO = SEL_BLOCK // CMP_STRIDE
SEL_TOPK = 16
WINDOW = 512
NSA_QBLOCK = 128
FORCE_SCORE = 1.0e4
MLSTM_HEADS = 4
MLSTM_CHUNK = 64
CONV_WIDTH = 4
D_FF = 4 * D_MODEL
EPS = 1e-6

GLA_W = GLA_HEADS * HEAD_DIM
NSA_W = NSA_HEADS * HEAD_DIM
KV_W = NSA_KV_HEADS * HEAD_DIM
MLSTM_W = MLSTM_HEADS * HEAD_DIM
MIX_W = GLA_W + NSA_W + MLSTM_W
IN_SPLITS = (GLA_W, GLA_W, GLA_W, GLA_W, GLA_LOWRANK,
             NSA_W, KV_W, KV_W, KV_W, KV_W, KV_W, KV_W, 3 * NSA_HEADS,
             MLSTM_W, MLSTM_W, MLSTM_HEADS, MLSTM_HEADS, MLSTM_W)
IN_W = sum(IN_SPLITS)

kernel_name = 'hybrid_gla_nsa_mlstm_step'


def split_points():
    return tuple(int(p) for p in np.cumsum(np.array(IN_SPLITS))[:-1])


def rmsnorm(x, g):
    xf = x.astype(jnp.float32)
    y = xf * lax.rsqrt(jnp.mean(xf * xf, axis=-1, keepdims=True) + EPS)
    return (y * g.astype(jnp.float32)).astype(x.dtype)


def masked_softmax(s, mask):
    s = jnp.where(mask, s, -jnp.inf)
    m = jnp.max(s, axis=-1, keepdims=True)
    m = jnp.where(jnp.isfinite(m), m, 0.0)
    e = jnp.exp(s - m)
    return e / jnp.maximum(jnp.sum(e, axis=-1, keepdims=True), 1e-30)


def to_chunks(a, c, pad_value=0.0):
    b, t = a.shape[0], a.shape[1]
    n = -(-t // c)
    widths = [(0, 0), (0, n * c - t)] + [(0, 0)] * (a.ndim - 2)
    a = jnp.pad(a, widths, constant_values=pad_value)
    a = a.reshape((b, n, c) + a.shape[2:])
    return jnp.swapaxes(jnp.moveaxis(a, 1, 0), 2, 3)


def from_chunks(o, t):
    n, b, h, c = o.shape[:4]
    o = jnp.moveaxis(jnp.swapaxes(o, 2, 3), 0, 1)
    return o.reshape((b, n * c, h) + o.shape[4:])[:, :t]


def gla_scan(q, k, v, log_a, s0):
    f32 = jnp.float32
    t = q.shape[1]
    c = min(GLA_CHUNK, t)
    causal = jnp.tril(jnp.ones((c, c), dtype=bool))
    xs = (to_chunks(q.astype(f32), c), to_chunks(k.astype(f32), c),
          to_chunks(v.astype(f32), c), to_chunks(log_a.astype(f32), c))

    def step(s, inp):
        qc, kc, vc, gc = inp
        bcum = jnp.cumsum(gc, axis=2)
        rel = jnp.where(causal[None, None, :, :, None],
                        bcum[:, :, :, None, :] - bcum[:, :, None, :, :], -jnp.inf)
        scores = jnp.einsum('bhtd,bhsd,bhtsd->bhts', qc, kc, jnp.exp(rel))
        o = (jnp.einsum('bhts,bhse->bhte', scores, vc)
             + jnp.einsum('bhtd,bhde->bhte', qc * jnp.exp(bcum), s))
        b_end = bcum[:, :, -1:, :]
        s_new = (jnp.exp(b_end[:, :, 0, :, None]) * s
                 + jnp.einsum('bhsd,bhse->bhde', kc * jnp.exp(b_end - bcum), vc))
        return s_new, o

    s_fin, o = lax.scan(step, s0.astype(f32), xs)
    return from_chunks(o, t), s_fin


def mlstm_scan(q, k, v, i_pre, log_f, c0, n0, m0):
    f32 = jnp.float32
    t = q.shape[1]
    c = min(MLSTM_CHUNK, t)
    causal = jnp.tril(jnp.ones((c, c), dtype=bool))
    xs = (to_chunks(q, c), to_chunks(k, c), to_chunks(v, c),
          to_chunks(i_pre, c, -jnp.inf), to_chunks(log_f, c))

    def step(carry, inp):
        cm, nv, m = carry
        qc, kc, vc, ic, fc = inp
        fcum = jnp.cumsum(fc, axis=-1)
        a = fcum + m[..., None]
        dmat = jnp.where(causal, fcum[..., :, None] - fcum[..., None, :] + ic[..., None, :], -jnp.inf)
        m_t = jnp.maximum(a, jnp.max(dmat, axis=-1))
        w_state = jnp.exp(a - m_t)
        qk = jnp.einsum('bhtd,bhsd->bhts', qc, kc) * jnp.exp(dmat - m_t[..., None])
        num = (jnp.einsum('bhts,bhse->bhte', qk, vc)
               + w_state[..., None] * jnp.einsum('bhtd,bhde->bhte', qc, cm))
        den = jnp.sum(qk, axis=-1) + w_state * jnp.einsum('bhtd,bhd->bht', qc, nv)
        h = num / jnp.maximum(jnp.abs(den), jnp.exp(-m_t))[..., None]
        m_new = m_t[..., -1]
        w_last = jnp.exp(dmat[..., -1, :] - m_new[..., None])
        w_carry = jnp.exp(a[..., -1] - m_new)
        c_new = w_carry[..., None, None] * cm + jnp.einsum('bhs,bhsd,bhse->bhde', w_last, kc, vc)
        n_new = w_carry[..., None] * nv + jnp.einsum('bhs,bhsd->bhd', w_last, kc)
        return (c_new, n_new, m_new), h

    (c_f, n_f, m_f), h = lax.scan(step, (c0.astype(f32), n0.astype(f32), m0.astype(f32)), xs)
    return from_chunks(h, t), c_f, n_f, m_f


def causal_conv(u, buf, w, bias):
    t = u.shape[1]
    up = jnp.concatenate([buf.astype(u.dtype), u], axis=1)
    y = bias
    for j in range(CONV_WIDTH):
        y = y + up[:, j:j + t] * w[j]
    return y, up[:, t:]


def compress_rows(rows, pe, w1, b1, w2, b2):
    f32 = jnp.float32
    b, length, hk, d = rows.shape
    n16 = length // CMP_STRIDE
    r = rows[:, :n16 * CMP_STRIDE].reshape(b, n16, CMP_STRIDE, hk, d).astype(f32)
    pe, w1 = pe.astype(f32), w1.astype(f32)
    hid = (jnp.einsum('bnrhd,rde->bnhe', r[:, :-1] + pe[None, None, :CMP_STRIDE, None, :], w1[:CMP_STRIDE])
           + jnp.einsum('bnrhd,rde->bnhe', r[:, 1:] + pe[None, None, CMP_STRIDE:, None, :], w1[CMP_STRIDE:])
           + b1)
    return jnp.einsum('bnhe,ed->bnhd', jax.nn.silu(hid), w2.astype(f32)) + b2


def nsa_query_block(q, gates, q_pos, k_cmp, v_cmp, k_blk, v_blk, k_win, v_win, w_pos):
    f32 = jnp.float32
    b, tq, hk, g, d = q.shape
    qf = q.astype(f32) * (HEAD_DIM ** -0.5)
    n_c = k_cmp.shape[1]
    c_end = jnp.arange(n_c) * CMP_STRIDE + (CMP_BLOCK - 1)
    p_c = masked_softmax(jnp.einsum('bthgd,bnhd->bhgtn', qf, k_cmp.astype(f32)),
                         c_end[None, :] <= q_pos[:, None])
    o_c = jnp.einsum('bhgtn,bnhd->bthgd', p_c, v_cmp.astype(f32))
    n_s = k_blk.shape[2]
    p_grp = jnp.sum(p_c, axis=2)
    pp = jnp.pad(p_grp, ((0, 0), (0, 0), (0, 0), (1, SEL_RATIO * n_s - n_c)))
    p_sel = (pp[..., 1:] + pp[..., :-1]).reshape(b, hk, tq, n_s, SEL_RATIO).sum(-1)
    blk = jnp.arange(n_s)[None, :]
    cur = (q_pos // SEL_BLOCK)[:, None]
    forced = (blk == 0) | (blk == cur) | (blk == cur - 1)
    score = jnp.where(blk > cur, -1.0, jnp.where(forced, FORCE_SCORE, p_sel))
    n_top = min(SEL_TOPK, n_s)
    _, idx = lax.top_k(score, n_top)
    take = jax.vmap(jax.vmap(lambda blocks, ix: blocks[ix]))
    k_sel = take(k_blk, idx).astype(f32)
    v_sel = take(v_blk, idx).astype(f32)
    s_pos = idx[..., None] * SEL_BLOCK + jnp.arange(SEL_BLOCK)
    s_mask = (s_pos <= q_pos[None, None, :, None, None]).reshape(b, hk, 1, tq, n_top * SEL_BLOCK)
    s_sc = jnp.einsum('bthgd,bhtnsd->bhgtns', qf, k_sel).reshape(b, hk, g, tq, n_top * SEL_BLOCK)
    p_s = masked_softmax(s_sc, s_mask).reshape(b, hk, g, tq, n_top, SEL_BLOCK)
    o_s = jnp.einsum('bhgtns,bhtnsd->bthgd', p_s, v_sel)
    w_mask = ((w_pos[None, :] <= q_pos[:, None]) & (w_pos[None, :] > q_pos[:, None] - WINDOW)
              & (w_pos[None, :] >= 0))
    p_w = masked_softmax(jnp.einsum('bthgd,blhd->bhgtl', qf, k_win.astype(f32)), w_mask)
    o_w = jnp.einsum('bhgtl,blhd->bthgd', p_w, v_win.astype(f32))
    gf = gates.astype(f32)
    return gf[..., 0:1] * o_c + gf[..., 1:2] * o_s + gf[..., 2:3] * o_w


def nsa_attend(q, gates, rows_full, win_all, q0, k_cmp, v_cmp):
    b, t = q.shape[0], q.shape[1]
    length = rows_full.shape[1]
    n_sel = -(-length // SEL_BLOCK)
    sel = jnp.pad(rows_full[:, :, 2:4], ((0, 0), (0, n_sel * SEL_BLOCK - length), (0, 0), (0, 0), (0, 0)))
    sel = sel.reshape((b, n_sel, SEL_BLOCK) + sel.shape[2:]).transpose(3, 0, 4, 1, 2, 5)
    k_blk, v_blk = sel[0], sel[1]
    qb = min(NSA_QBLOCK, t)
    n_qb = -(-t // qb)
    extra = n_qb * qb - t
    pad5 = ((0, 0), (0, extra), (0, 0), (0, 0), (0, 0))
    q_pad = jnp.pad(q, pad5)
    g_pad = jnp.pad(gates, pad5)
    w_pad = jnp.pad(win_all, pad5)

    def one_block(bi):
        start = bi * qb
        q_b = lax.dynamic_slice_in_dim(q_pad, start, qb, axis=1)
        g_b = lax.dynamic_slice_in_dim(g_pad, start, qb, axis=1)
        w_b = lax.dynamic_slice_in_dim(w_pad, start, WINDOW + qb, axis=1)
        q_pos = q0 + start + jnp.arange(qb)
        w_pos = q0 + start - WINDOW + jnp.arange(WINDOW + qb)
        return nsa_query_block(q_b, g_b, q_pos, k_cmp, v_cmp, k_blk, v_blk,
                               w_b[:, :, 0], w_b[:, :, 1], w_pos)

    out = jnp.moveaxis(lax.map(one_block, jnp.arange(n_qb)), 0, 1)
    return out.reshape((b, n_qb * qb) + out.shape[3:])[:, :t]


def token_mixers(h, q0, nsa_past, win_buf, gla_s0, ml_c0, ml_n0, ml_m0, ml_conv0, lp):
    b, t, _ = h.shape
    d = HEAD_DIM
    f32 = jnp.float32
    z = h @ lp['w_in']
    (gq, gk, gv, gr, ga, nq, nkc, nvc, nks, nvs, nkw, nvw, ng,
     mu, mv, mi, mf, mo) = jnp.split(z, split_points(), axis=-1)

    gshape = (b, t, GLA_HEADS, d)
    log_a = jax.nn.log_sigmoid((ga @ lp['gla_a_w'] + lp['gla_a_b']).astype(f32)) / GLA_TAU
    o_g, s_gla = gla_scan(gq.reshape(gshape) * (d ** -0.5), gk.reshape(gshape), gv.reshape(gshape),
                          log_a.reshape(gshape), gla_s0)
    o_g = (rmsnorm(o_g, lp['gla_norm_g']) * jax.nn.silu(gr.reshape(gshape).astype(f32))).reshape(b, t, GLA_W)

    g_qk = lp['nsa_qk_g']
    kv_shape = (b, t, NSA_KV_HEADS, d)
    q = rmsnorm(nq.reshape(b, t, NSA_KV_HEADS, NSA_GROUP, d), g_qk[0])
    new_rows = jnp.stack([nkc.reshape(kv_shape), nvc.reshape(kv_shape),
                          rmsnorm(nks.reshape(kv_shape), g_qk[2]), nvs.reshape(kv_shape)], axis=2)
    new_win = jnp.stack([rmsnorm(nkw.reshape(kv_shape), g_qk[3]), nvw.reshape(kv_shape)], axis=2)
    gates = jax.nn.sigmoid((ng + lp['nsa_gate_b']).astype(f32)).reshape(b, t, NSA_KV_HEADS, NSA_GROUP, 3)
    rows_full = jnp.concatenate([nsa_past.astype(new_rows.dtype), new_rows], axis=1)
    pe, w1, b1, w2, b2 = lp['cmp_pe'], lp['cmp_w1'], lp['cmp_b1'], lp['cmp_w2'], lp['cmp_b2']
    k_cmp = rmsnorm(compress_rows(rows_full[:, :, 0], pe[0], w1[0], b1[0], w2[0], b2[0]), g_qk[1])
    v_cmp = compress_rows(rows_full[:, :, 1], pe[1], w1[1], b1[1], w2[1], b2[1])
    win_cat = jnp.concatenate([win_buf.astype(new_win.dtype), new_win], axis=1)
    win_all = jnp.pad(win_cat, ((0, 0), (WINDOW - win_buf.shape[1], 0), (0, 0), (0, 0), (0, 0)))
    o_n = nsa_attend(q, gates, rows_full, win_all, q0, k_cmp, v_cmp).reshape(b, t, NSA_W)
    win_state = win_cat[:, win_cat.shape[1] - min(WINDOW, q0 + t):]

    u_conv, conv_state = causal_conv(mu, ml_conv0, lp['ml_conv_w'], lp['ml_conv_b'])
    u_act = jax.nn.silu(u_conv.astype(f32))
    u_h = u_act.reshape(b, t, MLSTM_HEADS, d)
    mq = jnp.einsum('bthd,hde->bthe', u_h, lp['ml_wq'].astype(f32))
    mk = jnp.einsum('bthd,hde->bthe', u_h, lp['ml_wk'].astype(f32)) * (d ** -0.5)
    i_pre = (mi + lp['ml_gate_b'][0]).astype(f32)
    log_f = jax.nn.log_sigmoid((mf + lp['ml_gate_b'][1]).astype(f32))
    h_m, c_m, n_m, m_m = mlstm_scan(mq, mk, mv.reshape(b, t, MLSTM_HEADS, d).astype(f32),
                                    i_pre, log_f, ml_c0, ml_n0, ml_m0)
    o_m = jax.nn.sigmoid(mo.astype(f32)) * (rmsnorm(h_m, lp['ml_norm_g']).reshape(b, t, MLSTM_W)
                                            + lp['ml_skip'] * u_act)

    mix = jnp.concatenate([o_g, o_n, o_m], axis=-1).astype(h.dtype)
    return mix, (new_rows, win_state, s_gla, c_m, n_m, m_m, conv_state)


def trunk_layer(x, q0, nsa_past, win_buf, gla_s0, ml_c0, ml_n0, ml_m0, ml_conv0, lp):
    mix, state = token_mixers(rmsnorm(x, lp['norm1_g']), q0, nsa_past, win_buf,
                              gla_s0, ml_c0, ml_n0, ml_m0, ml_conv0, lp)
    x = x + mix @ lp['w_out']
    hid = jax.nn.relu(rmsnorm(x, lp['norm2_g']) @ lp['w_up'])
    x = x + (hid * hid) @ lp['w_down']
    return x, state


def setup_inputs(seed: int = 0) -> dict:
    key = jax.random.key(seed)
    ks = jax.random.split(key, 40)
    f32 = jnp.float32

    def nrm(k, shape, s):
        return jax.random.normal(k, shape, f32) * s

    n_pages = PAST_LEN // PAGE_SIZE
    n_used = DEC_BATCH * n_pages
    n_pool = n_used + n_used // 4
    wb = min(WINDOW, PAST_LEN)
    page_table = jax.random.permutation(ks[0], n_pool)[:n_used].reshape(DEC_BATCH, n_pages).astype(jnp.int32)
    d = HEAD_DIM
    return {
        'x_prompt': nrm(ks[1], (BATCH, SEQ, D_MODEL), 1.0),
        'x_sample': nrm(ks[2], (DEC_BATCH, DEC_SEQ, D_MODEL), 1.0),
        'cache_nsa_kv': nrm(ks[3], (DEPTH, n_pool, PAGE_SIZE, 4, NSA_KV_HEADS, d), 1.0),
        'cache_nsa_win': nrm(ks[4], (DEPTH, DEC_BATCH, wb, 2, NSA_KV_HEADS, d), 1.0),
        'state_gla': nrm(ks[5], (DEPTH, DEC_BATCH, GLA_HEADS, d, d), 0.5),
        'state_mlstm_c': nrm(ks[6], (DEPTH, DEC_BATCH, MLSTM_HEADS, d, d), 0.5),
        'state_mlstm_n': nrm(ks[7], (DEPTH, DEC_BATCH, MLSTM_HEADS, d), 1.0),
        'state_mlstm_m': nrm(ks[8], (DEPTH, DEC_BATCH, MLSTM_HEADS), 1.0),
        'state_mlstm_conv': nrm(ks[9], (DEPTH, DEC_BATCH, CONV_WIDTH - 1, MLSTM_W), 1.0),
        'page_table': page_table,
        'norm1_g': 1.0 + nrm(ks[10], (DEPTH, D_MODEL), 0.1),
        'w_in': nrm(ks[11], (DEPTH, D_MODEL, IN_W), D_MODEL ** -0.5),
        'gla_a_w': nrm(ks[12], (DEPTH, GLA_LOWRANK, GLA_W), GLA_LOWRANK ** -0.5),
        'gla_a_b': nrm(ks[13], (DEPTH, GLA_W), 0.5),
        'gla_norm_g': 1.0 + nrm(ks[14], (DEPTH, d), 0.1),
        'nsa_qk_g': 1.0 + nrm(ks[15], (DEPTH, 4, d), 0.1),
        'nsa_gate_b': nrm(ks[16], (DEPTH, 3 * NSA_HEADS), 0.1),
        'cmp_pe': nrm(ks[17], (DEPTH, 2, CMP_BLOCK, d), 0.1),
        'cmp_w1': nrm(ks[18], (DEPTH, 2, CMP_BLOCK, d, CMP_HIDDEN), (CMP_BLOCK * d) ** -0.5),
        'cmp_b1': nrm(ks[19], (DEPTH, 2, CMP_HIDDEN), 0.01),
        'cmp_w2': nrm(ks[20], (DEPTH, 2, CMP_HIDDEN, d), CMP_HIDDEN ** -0.5),
        'cmp_b2': nrm(ks[21], (DEPTH, 2, d), 0.01),
        'ml_conv_w': nrm(ks[22], (DEPTH, CONV_WIDTH, MLSTM_W), CONV_WIDTH ** -0.5),
        'ml_conv_b': nrm(ks[23], (DEPTH, MLSTM_W), 0.01),
        'ml_wq': nrm(ks[24], (DEPTH, MLSTM_HEADS, d, d), d ** -0.5),
        'ml_wk': nrm(ks[25], (DEPTH, MLSTM_HEADS, d, d), d ** -0.5),
        'ml_gate_b': jnp.stack([nrm(ks[26], (DEPTH, MLSTM_HEADS), 0.1),
                                3.0 + nrm(ks[27], (DEPTH, MLSTM_HEADS), 0.5)], axis=1),
        'ml_norm_g': 1.0 + nrm(ks[28], (DEPTH, d), 0.1),
        'ml_skip': 1.0 + nrm(ks[29], (DEPTH, MLSTM_W), 0.1),
        'w_out': nrm(ks[30], (DEPTH, MIX_W, D_MODEL), MIX_W ** -0.5),
        'norm2_g': 1.0 + nrm(ks[31], (DEPTH, D_MODEL), 0.1),
        'w_up': nrm(ks[32], (DEPTH, D_MODEL, D_FF), D_MODEL ** -0.5),
        'w_down': nrm(ks[33], (DEPTH, D_FF, D_MODEL), D_FF ** -0.5),
    }


def reference(x_prompt, x_sample, cache_nsa_kv, cache_nsa_win, state_gla, state_mlstm_c, state_mlstm_n,
              state_mlstm_m, state_mlstm_conv, page_table, norm1_g, w_in, gla_a_w, gla_a_b, gla_norm_g,
              nsa_qk_g, nsa_gate_b, cmp_pe, cmp_w1, cmp_b1, cmp_w2, cmp_b2, ml_conv_w, ml_conv_b, ml_wq, ml_wk,
              ml_gate_b, ml_norm_g, ml_skip, w_out, norm2_g, w_up, w_down):
    f32 = jnp.float32
    d = HEAD_DIM
    bp = x_prompt.shape[0]
    n_dec = x_sample.shape[0]
    past = page_table.shape[1] * PAGE_SIZE
    dt = x_prompt.dtype
    empty_rows = jnp.zeros((bp, 0, 4, NSA_KV_HEADS, d), dt)
    empty_win = jnp.zeros((bp, 0, 2, NSA_KV_HEADS, d), dt)
    gla0 = jnp.zeros((bp, GLA_HEADS, d, d), f32)
    c0 = jnp.zeros((bp, MLSTM_HEADS, d, d), f32)
    n0 = jnp.zeros((bp, MLSTM_HEADS, d), f32)
    m0 = jnp.zeros((bp, MLSTM_HEADS), f32)
    conv0 = jnp.zeros((bp, CONV_WIDTH - 1, MLSTM_W), dt)
    x_p, x_s = x_prompt, x_sample
    states_p, states_s = [], []
    for l in range(DEPTH):
        lp = {'norm1_g': norm1_g[l], 'w_in': w_in[l], 'gla_a_w': gla_a_w[l], 'gla_a_b': gla_a_b[l],
              'gla_norm_g': gla_norm_g[l], 'nsa_qk_g': nsa_qk_g[l], 'nsa_gate_b': nsa_gate_b[l],
              'cmp_pe': cmp_pe[l], 'cmp_w1': cmp_w1[l], 'cmp_b1': cmp_b1[l], 'cmp_w2': cmp_w2[l],
              'cmp_b2': cmp_b2[l], 'ml_conv_w': ml_conv_w[l], 'ml_conv_b': ml_conv_b[l], 'ml_wq': ml_wq[l],
              'ml_wk': ml_wk[l], 'ml_gate_b': ml_gate_b[l], 'ml_norm_g': ml_norm_g[l], 'ml_skip': ml_skip[l],
              'w_out': w_out[l], 'norm2_g': norm2_g[l], 'w_up': w_up[l], 'w_down': w_down[l]}
        nsa_past = cache_nsa_kv[l][page_table].reshape((n_dec, past) + cache_nsa_kv.shape[3:])
        x_p, st_p = trunk_layer(x_p, 0, empty_rows, empty_win, gla0, c0, n0, m0, conv0, lp)
        x_s, st_s = trunk_layer(x_s, past, nsa_past, cache_nsa_win[l], state_gla[l], state_mlstm_c[l],
                                state_mlstm_n[l], state_mlstm_m[l], state_mlstm_conv[l], lp)
        states_p.append(st_p)
        states_s.append(st_s)
    nsa_rows_p = jnp.stack([s[0] for s in states_p])
    nsa_rows_s = jnp.stack([s[0] for s in states_s])
    win_p = jnp.stack([s[1] for s in states_p])
    win_s = jnp.stack([s[1] for s in states_s])
    gla_p = jnp.stack([s[2] for s in states_p])
    gla_s = jnp.stack([s[2] for s in states_s])
    mc_p = jnp.stack([s[3] for s in states_p])
    mc_s = jnp.stack([s[3] for s in states_s])
    mn_p = jnp.stack([s[4] for s in states_p])
    mn_s = jnp.stack([s[4] for s in states_s])
    mm_p = jnp.stack([s[5] for s in states_p])
    mm_s = jnp.stack([s[5] for s in states_s])
    conv_p = jnp.stack([s[6] for s in states_p])
    conv_s = jnp.stack([s[6] for s in states_s])
    return (x_p, x_s, nsa_rows_p, nsa_rows_s, win_p, win_s, gla_p, gla_s, mc_p, mc_s, mn_p, mn_s,
            mm_p, mm_s, conv_p, conv_s)
```

```python
import functools

import numpy as np
import jax
import jax.numpy as jnp
from jax import lax
from jax.experimental import pallas as pl
from jax.experimental.pallas import tpu as pltpu

F32 = jnp.float32
BF16 = jnp.bfloat16

D_MODEL = 1024
HEAD_DIM = 64
PAGE_SIZE = 128
GLA_HEADS = 4
GLA_LOWRANK = 16
GLA_TAU = 16.0
NSA_HEADS = 8
NSA_KV_HEADS = 2
NSA_GROUP = NSA_HEADS // NSA_KV_HEADS
CMP_BLOCK = 32
CMP_STRIDE = 16
CMP_HIDDEN = 64
SEL_BLOCK = 64
SEL_RATIO = SEL_BLOCK // CMP_STRIDE
SEL_TOPK = 16
WINDOW = 512
FORCE_SCORE = 1.0e4
MLSTM_HEADS = 4
CONV_WIDTH = 4
D_FF = 4 * D_MODEL
EPS = 1e-6

GLA_W = GLA_HEADS * HEAD_DIM
NSA_W = NSA_HEADS * HEAD_DIM
KV_W = NSA_KV_HEADS * HEAD_DIM
MLSTM_W = MLSTM_HEADS * HEAD_DIM

LANES = 128
ZG_W = 4 * GLA_W + LANES
GATE_W = NSA_KV_HEADS * LANES
ZN_W = NSA_W + 6 * KV_W + GATE_W
ZM_W = 3 * MLSTM_W + LANES
ROWS_W = 4 * KV_W
WIN_W = 2 * KV_W
QPAD_W = NSA_HEADS * LANES

SCAN_CHUNK = 128
GLA_SUB = 32
NSA_TQ = 128
NSA_TK = 512
NEG = -1.0e30
SEL_BIAS = 29952.0

VMEM_LIMIT = 56 * 1024 * 1024


def _dot(a, b):
    return jnp.dot(a, b, preferred_element_type=F32)


def _dot_hi(a, b):
    return jnp.dot(a, b, preferred_element_type=F32, precision=lax.Precision.HIGHEST)


def _dot_nt(a, b):
    return lax.dot_general(a, b, (((1,), (1,)), ((), ())), preferred_element_type=F32)


def _dot_nt_hi(a, b):
    return lax.dot_general(a, b, (((1,), (1,)), ((), ())), preferred_element_type=F32,
                           precision=lax.Precision.HIGHEST)


def _dot_tn_hi(a, b):
    return lax.dot_general(a, b, (((0,), (0,)), ((), ())), preferred_element_type=F32,
                           precision=lax.Precision.HIGHEST)


def _sigmoid(x):
    return 1.0 / (1.0 + jnp.exp(-x))


def _log_sigmoid(x):
    return jnp.minimum(x, 0.0) - jnp.log(1.0 + jnp.exp(-jnp.abs(x)))


def _group_mean_sq(x, ones_bd):
    sq = x * x
    hi = sq.astype(BF16)
    lo = (sq - hi.astype(F32)).astype(BF16)
    return (_dot(hi, ones_bd) + _dot(lo, ones_bd)) * (1.0 / HEAD_DIM)


def _params(*sem):
    return pltpu.CompilerParams(dimension_semantics=sem, vmem_limit_bytes=VMEM_LIMIT)


def _const_spec(shape):
    nd = len(shape)
    return pl.BlockSpec(shape, lambda *_: (0,) * nd)


def _inproj_kernel(x_ref, g_ref, w_ref, zg_ref, zn_ref, zm_ref):
    x = x_ref[...]
    h = x * lax.rsqrt(jnp.mean(x * x, axis=-1, keepdims=True) + EPS) * g_ref[...]
    hb = h.astype(BF16)
    zg_ref[...] = _dot(hb, w_ref[:, 0:ZG_W])
    zn_ref[...] = _dot(hb, w_ref[:, ZG_W:ZG_W + ZN_W])
    zm_ref[...] = _dot(hb, w_ref[:, ZG_W + ZN_W:ZG_W + ZN_W + ZM_W])


def _in_proj(x, g, w, tm):
    n = x.shape[0]
    zw = ZG_W + ZN_W + ZM_W
    return pl.pallas_call(
        _inproj_kernel,
        grid=(n // tm,),
        in_specs=[pl.BlockSpec((tm, D_MODEL), lambda i: (i, 0)),
                  _const_spec((1, D_MODEL)),
                  _const_spec((D_MODEL, zw))],
        out_specs=[pl.BlockSpec((tm, ZG_W), lambda i: (i, 0)),
                   pl.BlockSpec((tm, ZN_W), lambda i: (i, 0)),
                   pl.BlockSpec((tm, ZM_W), lambda i: (i, 0))],
        out_shape=[jax.ShapeDtypeStruct((n, ZG_W), F32),
                   jax.ShapeDtypeStruct((n, ZN_W), F32),
                   jax.ShapeDtypeStruct((n, ZM_W), F32)],
        compiler_params=_params("parallel"),
        name="in_proj",
    )(x, g, w)


def _pack_w_in(w_in):
    def cols(a, b):
        return w_in[:, a:b]

    def zeros(n):
        return jnp.zeros((D_MODEL, n), w_in.dtype)
    o_nsa = 4 * GLA_W + GLA_LOWRANK
    o_ng = o_nsa + NSA_W + 6 * KV_W
    o_ml = o_ng + 3 * NSA_HEADS
    o_mi = o_ml + 2 * MLSTM_W
    o_mo = o_mi + 2 * MLSTM_HEADS
    n_gate = 3 * NSA_GROUP
    parts = [cols(0, o_nsa), zeros(LANES - GLA_LOWRANK),
             cols(o_nsa, o_ng),
             cols(o_ng, o_ng + n_gate), zeros(LANES - n_gate),
             cols(o_ng + n_gate, o_ml), zeros(LANES - n_gate),
             cols(o_ml, o_mi), cols(o_mo, o_mo + MLSTM_W), cols(o_mi, o_mo),
             zeros(LANES - 2 * MLSTM_HEADS)]
    return jnp.concatenate(parts, axis=1).astype(BF16)


def _nsa_prep_kernel(zn_ref, gq_ref, gr_ref, gw_ref, gb_ref, ones_ref, place_ref,
                     qpad_ref, rows_ref, win_ref, kv_ref, r01_ref, gate_ref):
    ones_bd = ones_ref[...]
    q = zn_ref[:, 0:NSA_W]
    qn = q * lax.rsqrt(_group_mean_sq(q, ones_bd) + EPS) * gq_ref[...]
    qpad_ref[...] = _dot(qn.astype(BF16), place_ref[...]).astype(BF16)

    r = zn_ref[:, NSA_W:NSA_W + ROWS_W]
    col = lax.broadcasted_iota(jnp.int32, r.shape, 1)
    rn = r * lax.rsqrt(_group_mean_sq(r, ones_bd) + EPS) * gr_ref[...]
    rows = jnp.where((col >= 2 * KV_W) & (col < 3 * KV_W), rn, r)
    rows_ref[...] = rows
    r01_ref[...] = rows[:, 0:2 * KV_W].astype(BF16)

    w = zn_ref[:, NSA_W + ROWS_W:NSA_W + ROWS_W + WIN_W]
    colw = lax.broadcasted_iota(jnp.int32, w.shape, 1)
    wn = w * lax.rsqrt(_group_mean_sq(w, ones_bd[0:WIN_W, 0:WIN_W]) + EPS) * gw_ref[...]
    win = jnp.where(colw < KV_W, wn, w)
    win_ref[...] = win
    kv_ref[:, 0:2 * KV_W] = rows[:, 2 * KV_W:4 * KV_W].astype(BF16)
    kv_ref[:, 2 * KV_W:4 * KV_W] = win.astype(BF16)

    gate_ref[...] = _sigmoid(zn_ref[:, NSA_W + ROWS_W + WIN_W:ZN_W] + gb_ref[...])


def _nsa_prep(zn, qk_g, gate_b, tm):
    n = zn.shape[0]
    scale = HEAD_DIM ** -0.5
    gq = (jnp.tile(qk_g[0], NSA_HEADS) * scale).reshape(1, NSA_W)
    gr = jnp.tile(qk_g[2], ROWS_W // HEAD_DIM).reshape(1, ROWS_W)
    gw = jnp.tile(qk_g[3], WIN_W // HEAD_DIM).reshape(1, WIN_W)
    n_gate = 3 * NSA_GROUP
    gpad = jnp.zeros((LANES - n_gate,), F32)
    gb = jnp.concatenate([gate_b[0:n_gate], gpad, gate_b[n_gate:], gpad]).reshape(1, GATE_W)
    grp = np.arange(NSA_W) // HEAD_DIM
    ones_bd = jnp.asarray(grp[:, None] == grp[None, :], BF16)
    src = np.arange(NSA_W)
    head, d = src // HEAD_DIM, src % HEAD_DIM
    dst = head * LANES + (head // NSA_GROUP) * HEAD_DIM + d
    place = np.zeros((NSA_W, QPAD_W), np.float32)
    place[src, dst] = 1.0
    place = jnp.asarray(place, BF16)
    row = lambda w: pl.BlockSpec((tm, w), lambda i: (i, 0))
    return pl.pallas_call(
        _nsa_prep_kernel,
        grid=(n // tm,),
        in_specs=[row(ZN_W), _const_spec((1, NSA_W)), _const_spec((1, ROWS_W)),
                  _const_spec((1, WIN_W)), _const_spec((1, GATE_W)),
                  _const_spec((NSA_W, NSA_W)), _const_spec((NSA_W, QPAD_W))],
        out_specs=[row(QPAD_W), row(ROWS_W), row(WIN_W), row(4 * KV_W), row(2 * KV_W), row(GATE_W)],
        out_shape=[jax.ShapeDtypeStruct((n, QPAD_W), BF16),
                   jax.ShapeDtypeStruct((n, ROWS_W), F32),
                   jax.ShapeDtypeStruct((n, WIN_W), F32),
                   jax.ShapeDtypeStruct((n, 4 * KV_W), BF16),
                   jax.ShapeDtypeStruct((n, 2 * KV_W), BF16),
                   jax.ShapeDtypeStruct((n, GATE_W), F32)],
        compiler_params=_params("parallel"),
        name="nsa_prep",
    )(zn, gq, gr, gw, gb, ones_bd, place)


def _gla_kernel(zg_ref, s0_ref, aw_ref, ab_ref, ng_ref, tri_ref, ones_ref, og_ref, sout_ref, s_scr,
                *, t_valid, t_pad):
    i = pl.program_id(1)
    c = SCAN_CHUNK
    d = HEAD_DIM

    @pl.when(i == 0)
    def _():
        s_scr[...] = s0_ref[0]

    q = zg_ref[:, 0:GLA_W] * (d ** -0.5)
    k = zg_ref[:, GLA_W:2 * GLA_W]
    v = zg_ref[:, 2 * GLA_W:3 * GLA_W]
    r = zg_ref[:, 3 * GLA_W:4 * GLA_W]
    ga = zg_ref[:, 4 * GLA_W:ZG_W]
    g = _log_sigmoid(_dot(ga.astype(BF16), aw_ref[...]) + ab_ref[...]) * (1.0 / GLA_TAU)
    if t_valid < t_pad:
        valid = (i * c + lax.broadcasted_iota(jnp.int32, (c, 1), 0)) < t_valid
        g = jnp.where(valid, g, 0.0)
        k = jnp.where(valid, k, 0.0)
    bcum = _dot_hi(tri_ref[...], g)
    b_end = bcum[c - 1:c, :]
    last = (lax.broadcasted_iota(jnp.int32, (c, LANES), 0) == c - 1).astype(F32)
    row_i = lax.broadcasted_iota(jnp.int32, (GLA_SUB, GLA_SUB), 0)
    col_i = lax.broadcasted_iota(jnp.int32, (GLA_SUB, GLA_SUB), 1)
    o_heads = []
    for h in range(GLA_HEADS):
        sl = slice(h * d, (h + 1) * d)
        qh, kh, vh, bh = q[:, sl], k[:, sl], v[:, sl], bcum[:, sl]
        vb = vh.astype(BF16)
        s_h = s_scr[h]
        o_inter = _dot((qh * jnp.exp(bh)).astype(BF16), s_h.astype(BF16))
        o_sub = []
        for j in range(c // GLA_SUB):
            lo, hi = j * GLA_SUB, (j + 1) * GLA_SUB
            base = bh[lo - 1:lo, :] if j > 0 else jnp.zeros((1, d), F32)
            qt = (qh[lo:hi] * jnp.exp(bh[lo:hi] - base)).astype(BF16)
            kt = (kh[0:hi] * jnp.exp(base - bh[0:hi])).astype(BF16)
            sc = _dot_nt(qt, kt)
            diag = jnp.where(col_i <= row_i, sc[:, lo:hi], 0.0)
            if j > 0:
                sc = jnp.concatenate([sc[:, 0:lo], diag], axis=1)
            else:
                sc = diag
            o_sub.append(_dot(sc.astype(BF16), vb[0:hi]))
        o_heads.append(jnp.concatenate(o_sub, axis=0) + o_inter)
        decay_col = jnp.exp(_dot_tn_hi(bh, last)[:, 0:d])
        k_hat = kh * jnp.exp(b_end[:, sl] - bh)
        s_scr[h] = decay_col * s_h + _dot_tn_hi(k_hat, vh)
    o = jnp.concatenate(o_heads, axis=1)
    on = o * lax.rsqrt(_group_mean_sq(o, ones_ref[...]) + EPS) * ng_ref[...]
    og_ref[...] = (on * (r * _sigmoid(r))).astype(BF16)

    @pl.when(i == pl.num_programs(1) - 1)
    def _():
        sout_ref[0] = s_scr[...]


def _gla(zg, s0, a_w, a_b, norm_g, batch, t_pad, t_valid):
    c = SCAN_CHUNK
    nblk = t_pad // c
    aw = jnp.concatenate([a_w, jnp.zeros((LANES - GLA_LOWRANK, GLA_W), F32)], axis=0).astype(BF16)
    tri = jnp.asarray(np.tril(np.ones((c, c), np.float32)))
    grp = np.arange(GLA_W) // HEAD_DIM
    ones_bd = jnp.asarray(grp[:, None] == grp[None, :], BF16)
    kern = functools.partial(_gla_kernel, t_valid=t_valid, t_pad=t_pad)
    return pl.pallas_call(
        kern,
        grid=(batch, nblk),
        in_specs=[pl.BlockSpec((c, ZG_W), lambda b, i: (b * nblk + i, 0)),
                  pl.BlockSpec((1, GLA_HEADS, HEAD_DIM, HEAD_DIM), lambda b, i: (b, 0, 0, 0)),
                  _const_spec((LANES, GLA_W)), _const_spec((1, GLA_W)), _const_spec((1, GLA_W)),
                  _const_spec((c, c)), _const_spec((GLA_W, GLA_W))],
        out_specs=[pl.BlockSpec((c, GLA_W), lambda b, i: (b * nblk + i, 0)),
                   pl.BlockSpec((1, GLA_HEADS, HEAD_DIM, HEAD_DIM), lambda b, i: (b, 0, 0, 0))],
        out_shape=[jax.ShapeDtypeStruct((batch * t_pad, GLA_W), BF16),
                   jax.ShapeDtypeStruct((batch, GLA_HEADS, HEAD_DIM, HEAD_DIM), F32)],
        scratch_shapes=[pltpu.VMEM((GLA_HEADS, HEAD_DIM, HEAD_DIM), F32)],
        compiler_params=_params("parallel", "arbitrary"),
        name="gla_scan",
    )(zg, s0, aw, a_b.reshape(1, GLA_W), jnp.tile(norm_g, GLA_HEADS).reshape(1, GLA_W), tri, ones_bd)


def _mlstm_kernel(zm_ref, conv0_ref, c0_ref, n0_ref, m0_ref, cw_ref, cb_ref, wq_ref, wk_ref, gb_ref,
                  ng_ref, skip_ref, tri_ref, ones_ref, om_ref, cout_ref, nout_ref, mout_ref,
                  c_scr, n_scr, m_scr, ext_scr, *, t_valid, t_pad):
    i = pl.program_id(1)
    c = SCAN_CHUNK
    d = HEAD_DIM
    nh = MLSTM_HEADS

    @pl.when(i == 0)
    def _():
        c_scr[...] = c0_ref[0]
        n_scr[...] = n0_ref[0]
        m_scr[...] = m0_ref[0]
        ext_scr[0:8, :] = conv0_ref[0]

    mu = zm_ref[:, 0:MLSTM_W]
    ext_scr[8:8 + c, :] = mu
    u_conv = cb_ref[...] + mu * cw_ref[CONV_WIDTH - 1:CONV_WIDTH, :]
    for j in range(1, CONV_WIDTH):
        u_conv = u_conv + ext_scr[8 - j:8 - j + c, :] * cw_ref[CONV_WIDTH - 1 - j:CONV_WIDTH - j, :]
    ext_scr[0:8, :] = mu[c - 8:c, :]
    u_act = u_conv * _sigmoid(u_conv)
    ub = u_act.astype(BF16)
    q = _dot(ub, wq_ref[...])
    k = _dot(ub, wk_ref[...]) * (d ** -0.5)
    v = zm_ref[:, MLSTM_W:2 * MLSTM_W]
    og = zm_ref[:, 2 * MLSTM_W:3 * MLSTM_W]
    gz = zm_ref[:, 3 * MLSTM_W:ZM_W] + gb_ref[...]
    lane = lax.broadcasted_iota(jnp.int32, (c, LANES), 1)
    x = jnp.where(lane < nh, gz, _log_sigmoid(gz))
    if t_valid < t_pad:
        valid = (i * c + lax.broadcasted_iota(jnp.int32, (c, 1), 0)) < t_valid
        x = jnp.where(valid, x, jnp.where(lane < nh, NEG, 0.0))
    fc = _dot_hi(tri_ref[...], x)
    x = jnp.where(lane < nh, x, fc)
    sel = (lax.broadcasted_iota(jnp.int32, (8, LANES), 0)
           == lax.broadcasted_iota(jnp.int32, (8, LANES), 1)).astype(F32)
    xt = _dot_nt_hi(sel, x)
    row_i = lax.broadcasted_iota(jnp.int32, (c, c), 0)
    col_i = lax.broadcasted_iota(jnp.int32, (c, c), 1)
    h_heads = []
    for h in range(nh):
        sl = slice(h * d, (h + 1) * d)
        qh, kh, vh = q[:, sl], k[:, sl], v[:, sl]
        i_col, f_col = x[:, h:h + 1], x[:, nh + h:nh + h + 1]
        i_row, f_row = xt[h:h + 1, :], xt[nh + h:nh + h + 1, :]
        m_prev = m_scr[h:h + 1, 0:1]
        c_h = c_scr[h]
        n_h = n_scr[h:h + 1, :]
        a = f_col + m_prev
        dmat = jnp.where(col_i <= row_i, f_col - f_row + i_row, NEG)
        m_t = jnp.maximum(a, jnp.max(dmat, axis=-1, keepdims=True))
        w_state = jnp.exp(a - m_t)
        qb = qh.astype(BF16)
        qk = _dot_nt(qb, kh.astype(BF16)) * jnp.exp(dmat - m_t)
        num = _dot(qk.astype(BF16), vh.astype(BF16)) + w_state * _dot(qb, c_h.astype(BF16))
        den = jnp.sum(qk, axis=-1, keepdims=True) + w_state * jnp.sum(qh * n_h, axis=-1, keepdims=True)
        h_heads.append(num / jnp.maximum(jnp.abs(den), jnp.exp(-m_t)))
        m_new = m_t[c - 1:c, :]
        w_carry = jnp.exp(a[c - 1:c, :] - m_new)
        w_last = jnp.exp(f_col[c - 1:c, :] - f_col + i_col - m_new)
        kw = kh * w_last
        c_scr[h] = w_carry * c_h + _dot_tn_hi(kw, vh)
        n_scr[h:h + 1, :] = w_carry * n_h + jnp.sum(kw, axis=0, keepdims=True)
        m_scr[h:h + 1, :] = jnp.broadcast_to(m_new, (1, LANES))
    hm = jnp.concatenate(h_heads, axis=1)
    hn = hm * lax.rsqrt(_group_mean_sq(hm, ones_ref[...]) + EPS) * ng_ref[...]
    om_ref[...] = (_sigmoid(og) * (hn + skip_ref[...] * u_act)).astype(BF16)

    @pl.when(i == pl.num_programs(1) - 1)
    def _():
        cout_ref[0] = c_scr[...]
        nout_ref[0] = n_scr[...]
        mout_ref[0] = m_scr[...]


def _block_diag_heads(w):
    nh, d, _ = w.shape
    eye = jnp.eye(nh, dtype=w.dtype)
    return jnp.einsum('hde,hg->hdge', w, eye).reshape(nh * d, nh * d)


def _mlstm(zm, conv0, c0, n0, m0, lp, batch, t_pad, t_valid):
    c = SCAN_CHUNK
    nblk = t_pad // c
    nh, d = MLSTM_HEADS, HEAD_DIM
    conv0p = jnp.concatenate([jnp.zeros((batch, 8 - (CONV_WIDTH - 1), MLSTM_W), F32), conv0.astype(F32)], axis=1)
    n0p = jnp.concatenate([n0, jnp.zeros((batch, 8 - nh, d), F32)], axis=1)
    m0p = jnp.concatenate([jnp.broadcast_to(m0[:, :, None], (batch, nh, LANES)),
                           jnp.zeros((batch, 8 - nh, LANES), F32)], axis=1)
    gb = jnp.concatenate([lp['ml_gate_b'][0], lp['ml_gate_b'][1],
                          jnp.zeros((LANES - 2 * nh,), F32)]).reshape(1, LANES)
    tri = jnp.asarray(np.tril(np.ones((c, c), np.float32)))
    grp = np.arange(MLSTM_W) // d
    ones_bd = jnp.asarray(grp[:, None] == grp[None, :], BF16)
    kern = functools.partial(_mlstm_kernel, t_valid=t_valid, t_pad=t_pad)
    per_b = lambda shape: pl.BlockSpec((1,) + shape, lambda b, i: (b,) + (0,) * len(shape))
    om, c_f, n_f, m_f = pl.pallas_call(
        kern,
        grid=(batch, nblk),
        in_specs=[pl.BlockSpec((c, ZM_W), lambda b, i: (b * nblk + i, 0)),
                  per_b((8, MLSTM_W)), per_b((nh, d, d)), per_b((8, d)), per_b((8, LANES)),
                  _const_spec((CONV_WIDTH, MLSTM_W)), _const_spec((1, MLSTM_W)),
                  _const_spec((MLSTM_W, MLSTM_W)), _const_spec((MLSTM_W, MLSTM_W)),
                  _const_spec((1, LANES)), _const_spec((1, MLSTM_W)), _const_spec((1, MLSTM_W)),
                  _const_spec((c, c)), _const_spec((MLSTM_W, MLSTM_W))],
        out_specs=[pl.BlockSpec((c, MLSTM_W), lambda b, i: (b * nblk + i, 0)),
                   per_b((nh, d, d)), per_b((8, d)), per_b((8, LANES))],
        out_shape=[jax.ShapeDtypeStruct((batch * t_pad, MLSTM_W), BF16),
                   jax.ShapeDtypeStruct((batch, nh, d, d), F32),
                   jax.ShapeDtypeStruct((batch, 8, d), F32),
                   jax.ShapeDtypeStruct((batch, 8, LANES), F32)],
        scratch_shapes=[pltpu.VMEM((nh, d, d), F32), pltpu.VMEM((8, d), F32),
                        pltpu.VMEM((8, LANES), F32), pltpu.VMEM((8 + c, MLSTM_W), F32)],
        compiler_params=_params("parallel", "arbitrary"),
        name="mlstm_scan",
    )(zm, conv0p, c0, n0p, m0p, lp['ml_conv_w'], lp['ml_conv_b'].reshape(1, MLSTM_W),
      _block_diag_heads(lp['ml_wq']).astype(BF16), _block_diag_heads(lp['ml_wk']).astype(BF16),
      gb, jnp.tile(lp['ml_norm_g'], nh).reshape(1, MLSTM_W), lp['ml_skip'].reshape(1, MLSTM_W),
      tri, ones_bd)
    return om, c_f, n_f[:, :nh], m_f[:, :nh, 0]


CMP_IN = CMP_STRIDE * 2 * KV_W
CMP_G = 2 * 2 * KV_W


def _cmp_weights(lp):
    w1 = lp['cmp_w1'].reshape(2, 2, CMP_STRIDE, HEAD_DIM, CMP_HIDDEN)
    eye = jnp.eye(2, dtype=F32)
    w_ab = jnp.einsum('sarde,st,hg->rshdatge', w1, eye, eye).reshape(CMP_IN, CMP_G)
    pe = lp['cmp_pe'].reshape(2, 2, CMP_STRIDE, 1, HEAD_DIM)
    pe = jnp.broadcast_to(jnp.transpose(pe, (1, 2, 0, 3, 4)),
                          (2, CMP_STRIDE, 2, NSA_KV_HEADS, HEAD_DIM)).reshape(2, CMP_IN)
    pe8 = jnp.concatenate([pe, jnp.zeros((6, CMP_IN), F32)], axis=0)
    b1 = jnp.broadcast_to(lp['cmp_b1'][:, None, :], (2, NSA_KV_HEADS, CMP_HIDDEN)).reshape(1, 2 * KV_W)
    w2 = jnp.einsum('sed,st,hg->shetgd', lp['cmp_w2'], eye, eye).reshape(2 * KV_W, 2 * KV_W)
    b2 = jnp.broadcast_to(lp['cmp_b2'][:, None, :], (2, NSA_KV_HEADS, HEAD_DIM)).reshape(1, 2 * KV_W)
    g1 = jnp.tile(lp['nsa_qk_g'][1], NSA_KV_HEADS).reshape(1, KV_W)
    grp = np.arange(KV_W) // HEAD_DIM
    ones_bd = jnp.asarray(grp[:, None] == grp[None, :], BF16)
    return (w_ab.astype(BF16), pe8.astype(BF16), b1, w2.astype(BF16), b2, g1, ones_bd)


def _cmp_finish(gsum, n_rows, wab_ref, pe_ref, b1_ref, w2_ref, b2_ref, g1_ref, ones_ref):
    half = 2 * KV_W
    g_pe = _dot(pe_ref[...], wab_ref[...])
    bias = g_pe[0:1, 0:half] + g_pe[1:2, half:CMP_G] + b1_ref[...]
    hid = gsum[:, 0:half] + pltpu.roll(gsum[:, half:CMP_G], n_rows - 1, 0) + bias
    act = hid * _sigmoid(hid)
    cmp = _dot(act.astype(BF16), w2_ref[...]) + b2_ref[...]
    kc = cmp[:, 0:KV_W]
    kc = kc * lax.rsqrt(_group_mean_sq(kc, ones_ref[...]) + EPS) * g1_ref[...]
    return kc, cmp[:, KV_W:half]


def _cmp_kernel(r_ref, wab_ref, pe_ref, b1_ref, w2_ref, b2_ref, g1_ref, ones_ref, kc_ref, vc_ref, *, n_rows):
    gsum = _dot(r_ref[0], wab_ref[...])
    kc, vc = _cmp_finish(gsum, n_rows, wab_ref, pe_ref, b1_ref, w2_ref, b2_ref, g1_ref, ones_ref)
    kc_ref[0] = kc.astype(BF16)
    vc_ref[0] = vc.astype(BF16)


def _cmp_specs():
    half = 2 * KV_W
    return [_const_spec((CMP_IN, CMP_G)), _const_spec((8, CMP_IN)), _const_spec((1, half)),
            _const_spec((half, half)), _const_spec((1, half)), _const_spec((1, KV_W)),
            _const_spec((KV_W, KV_W))]


def _compress(r01, cw, batch, t_len):
    n16 = t_len // CMP_STRIDE
    x = r01.reshape(batch, n16, CMP_IN)
    blk = lambda w: pl.BlockSpec((1, n16, w), lambda b: (b, 0, 0))
    return pl.pallas_call(
        functools.partial(_cmp_kernel, n_rows=n16),
        grid=(batch,),
        in_specs=[blk(CMP_IN)] + _cmp_specs(),
        out_specs=[blk(KV_W), blk(KV_W)],
        out_shape=[jax.ShapeDtypeStruct((batch, n16, KV_W), BF16)] * 2,
        compiler_params=_params("parallel"),
        name="nsa_compress",
    )(x, *cw)


def _nsa_kernel(qpad_ref, ksel_ref, vsel_ref, kwin_ref, vwin_ref, kc_ref, vc_ref, gate_ref, mt_ref,
                et_ref, pl_ref, out_ref, sc_scr, m_scr, l_scr, acc_scr, *, nb):
    qi = pl.program_id(2)
    tq, tk, grp = NSA_TQ, NSA_TK, NSA_GROUP
    rows = grp * tq
    nbp, ncp = mt_ref.shape
    start = qi * tq
    q4 = jnp.concatenate([qpad_ref[:, g * LANES:(g + 1) * LANES] for g in range(grp)], axis=0)
    tpos = start + (lax.broadcasted_iota(jnp.int32, (rows, 1), 0) & (tq - 1))

    sc = _dot_nt(q4, kc_ref[0])
    ccol = lax.broadcasted_iota(jnp.int32, (rows, ncp), 1)
    vis = (ccol * CMP_STRIDE + (CMP_BLOCK - 1)) <= tpos
    s_m = jnp.where(vis, sc, NEG)
    e = jnp.where(vis, jnp.exp(s_m - jnp.max(s_m, axis=-1, keepdims=True)), 0.0)
    p_c = e / jnp.maximum(jnp.sum(e, axis=-1, keepdims=True), 1e-30)
    o_c = _dot(p_c.astype(BF16), vc_ref[0])

    pg = p_c[0:tq]
    for g in range(1, grp):
        pg = pg + p_c[g * tq:(g + 1) * tq]
    h1 = pg.astype(BF16)
    r1 = pg - h1.astype(F32)
    h2 = r1.astype(BF16)
    h3 = (r1 - h2.astype(F32)).astype(BF16)
    mt = mt_ref[...]
    p_sel = _dot_nt(mt, h1) + _dot_nt(mt, h2) + _dot_nt(mt, h3)
    blk = lax.broadcasted_iota(jnp.int32, (nbp, tq), 0)
    cur = (start + lax.broadcasted_iota(jnp.int32, (nbp, tq), 1)) // SEL_BLOCK
    forced = (blk == 0) | (blk == cur) | (blk == cur - 1)
    score = jnp.where(blk > cur, -1.0, jnp.where(forced, FORCE_SCORE, p_sel))
    if nb < nbp:
        score = jnp.where(blk >= nb, -2.0, score)
    sc_scr[...] = score

    def rank_body(i, cnt):
        row = sc_scr[pl.ds(i, 1), :]
        ge = jnp.where(row >= score, 1.0, 0.0)
        gt = jnp.where(row > score, 1.0, 0.0)
        return cnt + jnp.where(blk > i, ge, gt)
    rank = lax.fori_loop(0, nb, rank_body, jnp.zeros((nbp, tq), F32))
    unsel = jnp.where(rank < SEL_TOPK, 0.0, -1.0)
    unsel_t = unsel.T.astype(BF16)
    lhs = jnp.concatenate([q4, jnp.concatenate([unsel_t] * grp, axis=0)], axis=1)

    m_scr[...] = jnp.full((rows, 1), NEG, F32)
    l_scr[...] = jnp.zeros((rows, 1), F32)
    acc_scr[...] = jnp.zeros((rows, LANES), F32)

    def key_body(kt, carry):
        koff = pl.multiple_of(kt * tk, tk)
        kk = jnp.concatenate([ksel_ref[pl.ds(koff, tk), :], et_ref[pl.ds(koff, tk), :]], axis=1)
        s = _dot_nt(lhs, kk)
        kpos = koff + lax.broadcasted_iota(jnp.int32, (rows, tk), 1)
        s = jnp.where(kpos <= tpos, s, NEG)
        m_old = m_scr[...]
        m_new = jnp.maximum(m_old, jnp.max(s, axis=-1, keepdims=True))
        p = jnp.exp(s - m_new)
        alpha = jnp.exp(m_old - m_new)
        l_scr[...] = alpha * l_scr[...] + jnp.sum(p, axis=-1, keepdims=True)
        acc_scr[...] = alpha * acc_scr[...] + _dot(p.astype(BF16), vsel_ref[pl.ds(koff, tk), :])
        m_scr[...] = m_new
        return carry
    lax.fori_loop(0, (start + tq + tk - 1) // tk, key_body, 0)
    o_s = acc_scr[...] / l_scr[...]

    wk = WINDOW + tq
    wstart = pl.multiple_of(jnp.maximum(start - WINDOW, 0), tq)
    s = _dot_nt(q4, kwin_ref[pl.ds(wstart, wk), :])
    wpos = wstart + lax.broadcasted_iota(jnp.int32, (rows, wk), 1)
    s = jnp.where((wpos <= tpos) & (wpos > tpos - WINDOW), s, NEG)
    e = jnp.exp(s - jnp.max(s, axis=-1, keepdims=True))
    o_w = _dot(e.astype(BF16), vwin_ref[pl.ds(wstart, wk), :]) / jnp.sum(e, axis=-1, keepdims=True)

    gt = gate_ref[...]

    def gate(branch):
        return jnp.concatenate([gt[0:tq, g * 3 + branch:g * 3 + branch + 1] for g in range(grp)], axis=0)
    o = (gate(0) * o_c + gate(1) * o_s + gate(2) * o_w).astype(BF16)
    out = _dot(o[0:tq], pl_ref[0, 0])
    for g in range(1, grp):
        out = out + _dot(o[g * tq:(g + 1) * tq], pl_ref[0, g])
    out_ref[...] = out.astype(BF16)


def _nsa_constants(t_len):
    nb = t_len // SEL_BLOCK
    nbp = max(LANES, -(-nb // LANES) * LANES)
    ncp = t_len // CMP_STRIDE
    j = np.arange(nbp)[:, None]
    n = np.arange(ncp)[None, :]
    first = (n >= SEL_RATIO * j) & (n <= SEL_RATIO * j + SEL_RATIO - 1)
    second = (n >= SEL_RATIO * j - 1) & (n <= SEL_RATIO * j + SEL_RATIO - 2)
    mt = (first.astype(np.float32) + second.astype(np.float32)) * (n < ncp - 1) * (j < nb)
    et = (np.arange(t_len)[:, None] // SEL_BLOCK == np.arange(nbp)[None, :]).astype(np.float32) * SEL_BIAS
    place = np.zeros((NSA_KV_HEADS, NSA_GROUP, LANES, NSA_GROUP * HEAD_DIM), np.float32)
    for h in range(NSA_KV_HEADS):
        for g in range(NSA_GROUP):
            place[h, g, h * HEAD_DIM + np.arange(HEAD_DIM), g * HEAD_DIM + np.arange(HEAD_DIM)] = 1.0
    return nb, jnp.asarray(mt, BF16), jnp.asarray(et, BF16), jnp.asarray(place, BF16)


def _nsa_attend(qpad, kv, kc, vc, gates, batch, t_len):
    tq = NSA_TQ
    nq = t_len // tq
    rows = NSA_GROUP * tq
    nb, mt, et, place = _nsa_constants(t_len)
    nbp, ncp = mt.shape
    seq = lambda c: pl.BlockSpec((t_len, KV_W), lambda b, h, i: (b, c))
    return pl.pallas_call(
        functools.partial(_nsa_kernel, nb=nb),
        grid=(batch, NSA_KV_HEADS, nq),
        in_specs=[pl.BlockSpec((tq, NSA_GROUP * LANES), lambda b, h, i: (b * nq + i, h)),
                  seq(0), seq(1), seq(2), seq(3),
                  pl.BlockSpec((1, ncp, KV_W), lambda b, h, i: (b, 0, 0)),
                  pl.BlockSpec((1, ncp, KV_W), lambda b, h, i: (b, 0, 0)),
                  pl.BlockSpec((tq, LANES), lambda b, h, i: (b * nq + i, h)),
                  _const_spec((nbp, ncp)), _const_spec((t_len, nbp)),
                  pl.BlockSpec((1, NSA_GROUP, LANES, NSA_GROUP * HEAD_DIM), lambda b, h, i: (h, 0, 0, 0))],
        out_specs=pl.BlockSpec((tq, NSA_GROUP * HEAD_DIM), lambda b, h, i: (b * nq + i, h)),
        out_shape=jax.ShapeDtypeStruct((batch * t_len, NSA_W), BF16),
        scratch_shapes=[pltpu.VMEM((nbp, tq), F32), pltpu.VMEM((rows, 1), F32),
                        pltpu.VMEM((rows, 1), F32), pltpu.VMEM((rows, LANES), F32)],
        compiler_params=_params("parallel", "parallel", "arbitrary"),
        name="nsa_attend",
    )(qpad, kv, kv, kv, kv, kc, vc, gates, mt, et, place)


def _outffn_kernel(x_ref, og_ref, on_ref, om_ref, wo_ref, g2_ref, wu_ref, wd_ref, y_ref):
    x1 = (x_ref[...] + _dot(og_ref[...], wo_ref[0:GLA_W, :])
          + _dot(on_ref[...], wo_ref[GLA_W:GLA_W + NSA_W, :])
          + _dot(om_ref[...], wo_ref[GLA_W + NSA_W:GLA_W + NSA_W + MLSTM_W, :]))
    h = x1 * lax.rsqrt(jnp.mean(x1 * x1, axis=-1, keepdims=True) + EPS) * g2_ref[...]
    hid = jnp.maximum(_dot(h.astype(BF16), wu_ref[...]), 0.0)
    y_ref[...] = x1 + _dot((hid * hid).astype(BF16), wd_ref[...])


def _out_ffn(x, og, on, om, wo, g2, wu, wd, tm):
    n = x.shape[0]
    row = lambda w: pl.BlockSpec((tm, w), lambda i: (i, 0))
    return pl.pallas_call(
        _outffn_kernel,
        grid=(n // tm,),
        in_specs=[row(D_MODEL), row(GLA_W), row(NSA_W), row(MLSTM_W),
                  _const_spec((D_MODEL, D_MODEL)), _const_spec((1, D_MODEL)),
                  _const_spec((D_MODEL, D_FF)), _const_spec((D_FF, D_MODEL))],
        out_specs=row(D_MODEL),
        out_shape=jax.ShapeDtypeStruct((n, D_MODEL), F32),
        compiler_params=_params("parallel"),
        name="out_ffn",
    )(x, og, on, om, wo, g2, wu, wd)


def _layer_weights(lp):
    return {'w_in': _pack_w_in(lp['w_in']), 'w_out': lp['w_out'].astype(BF16),
            'w_up': lp['w_up'].astype(BF16), 'w_down': lp['w_down'].astype(BF16),
            'cmp': _cmp_weights(lp)}


def _prompt_layer(x, lp, lw, batch, t_len):
    d = HEAD_DIM
    zg, zn, zm = _in_proj(x, lp['norm1_g'].reshape(1, D_MODEL), lw['w_in'], 256)
    qpad, rows, win, kv, r01, gates = _nsa_prep(zn, lp['nsa_qk_g'], lp['nsa_gate_b'], 256)
    zero = lambda *s: jnp.zeros(s, F32)
    og, s_gla = _gla(zg, zero(batch, GLA_HEADS, d, d), lp['gla_a_w'], lp['gla_a_b'], lp['gla_norm_g'],
                     batch, t_len, t_len)
    om, c_m, n_m, m_m = _mlstm(zm, zero(batch, CONV_WIDTH - 1, MLSTM_W), zero(batch, MLSTM_HEADS, d, d),
                               zero(batch, MLSTM_HEADS, d), zero(batch, MLSTM_HEADS), lp, batch, t_len, t_len)
    kc, vc = _compress(r01, lw['cmp'], batch, t_len)
    on = _nsa_attend(qpad, kv, kc, vc, gates, batch, t_len)
    y = _out_ffn(x, og, on, om, lw['w_out'], lp['norm2_g'].reshape(1, D_MODEL), lw['w_up'], lw['w_down'], 256)
    wlen = min(WINDOW, t_len)
    new_rows = rows.reshape(batch, t_len, 4, NSA_KV_HEADS, d)
    win_state = win.reshape(batch, t_len, 2, NSA_KV_HEADS, d)[:, t_len - wlen:]
    conv_state = zm.reshape(batch, t_len, ZM_W)[:, t_len - (CONV_WIDTH - 1):, 0:MLSTM_W]
    return y, (new_rows, win_state, s_gla, c_m, n_m, m_m, conv_state)


GATHER_PAGES = 8
GROUPS_PER_PAGE = PAGE_SIZE // CMP_STRIDE


def _cmp_gather_kernel(pt_ref, *refs):
    w_ref, g_ref = refs[2 * GATHER_PAGES], refs[2 * GATHER_PAGES + 1]
    parts = []
    for s in range(2):
        page_refs = refs[s * GATHER_PAGES:(s + 1) * GATHER_PAGES]
        acc = jnp.zeros((GATHER_PAGES * GROUPS_PER_PAGE, 2 * KV_W), F32)
        for r in range(CMP_STRIDE):
            xr = jnp.concatenate([p[0, pl.ds(r, GROUPS_PER_PAGE, stride=CMP_STRIDE), :] for p in page_refs],
                                 axis=0)
            acc = acc + _dot(xr.astype(BF16), w_ref[r, s])
        parts.append(acc)
    g_ref[0] = jnp.concatenate([parts[0][:, 0:KV_W], parts[1][:, 0:KV_W],
                                parts[0][:, KV_W:2 * KV_W], parts[1][:, KV_W:2 * KV_W]], axis=1)


def _cmp_gather(cache, pt_flat, w_ab, batch, n_pages):
    n_pool = cache.shape[0]
    cache2 = cache.reshape(n_pool, PAGE_SIZE, ROWS_W)
    steps = n_pages // GATHER_PAGES
    rows = GATHER_PAGES * GROUPS_PER_PAGE
    w4 = w_ab.reshape(CMP_STRIDE, 2, KV_W, 2, 2, KV_W)
    w_rs = jnp.stack([w4[:, s, :, :, s, :] for s in range(2)], axis=1).reshape(CMP_STRIDE, 2, KV_W, 2 * KV_W)

    def page_spec(s, k):
        return pl.BlockSpec((1, PAGE_SIZE, KV_W),
                            lambda b, i, pt: (pt[b * n_pages + i * GATHER_PAGES + k], 0, s))
    grid_spec = pltpu.PrefetchScalarGridSpec(
        num_scalar_prefetch=1,
        grid=(batch, steps),
        in_specs=[page_spec(s, k) for s in range(2) for k in range(GATHER_PAGES)]
        + [pl.BlockSpec((CMP_STRIDE, 2, KV_W, 2 * KV_W), lambda b, i, pt: (0, 0, 0, 0))],
        out_specs=pl.BlockSpec((1, rows, CMP_G), lambda b, i, pt: (b, i, 0)),
    )
    return pl.pallas_call(
        _cmp_gather_kernel,
        grid_spec=grid_spec,
        out_shape=jax.ShapeDtypeStruct((batch, n_pages * GROUPS_PER_PAGE, CMP_G), F32),
        compiler_params=_params("parallel", "arbitrary"),
        name="nsa_cmp_gather",
    )(pt_flat, *([cache2] * (2 * GATHER_PAGES)), w_rs)


def _to_col(row, n):
    eye = (lax.broadcasted_iota(jnp.int32, (n, n), 0) == lax.broadcasted_iota(jnp.int32, (n, n), 1))
    return jnp.sum(jnp.where(eye, jnp.broadcast_to(row, (n, n)), 0.0), axis=1, keepdims=True)


def _dec_cmp_kernel(g_ref, q_ref, wc_ref, wnew_ref, wab_ref, pe_ref, b1_ref, w2_ref, b2_ref, g1_ref,
                    ones_ref, m_ref, oc_ref, ow_ref, idx_ref, *, n_groups, nb, cur):
    nh, grp = NSA_KV_HEADS, NSA_GROUP
    q8 = q_ref[0]
    kc, vc = _cmp_finish(g_ref[0], n_groups, wab_ref, pe_ref, b1_ref, w2_ref, b2_ref, g1_ref, ones_ref)
    sc = _dot_nt(q8, kc.astype(BF16))
    vis = lax.broadcasted_iota(jnp.int32, sc.shape, 1) < n_groups - 1
    s_m = jnp.where(vis, sc, NEG)
    e = jnp.where(vis, jnp.exp(s_m - jnp.max(s_m, axis=-1, keepdims=True)), 0.0)
    p_c = e / jnp.maximum(jnp.sum(e, axis=-1, keepdims=True), 1e-30)
    oc_ref[0] = _dot(p_c.astype(BF16), vc.astype(BF16))

    nsp = m_ref.shape[1]
    pg = jnp.concatenate([jnp.sum(p_c[h * grp:(h + 1) * grp], axis=0, keepdims=True) for h in range(nh)]
                         + [jnp.zeros((8 - nh, n_groups), F32)], axis=0)
    h1 = pg.astype(BF16)
    r1 = pg - h1.astype(F32)
    h2 = r1.astype(BF16)
    h3 = (r1 - h2.astype(F32)).astype(BF16)
    m = m_ref[...]
    p_sel = _dot(h1, m) + _dot(h2, m) + _dot(h3, m)
    blk = lax.broadcasted_iota(jnp.int32, (1, nsp), 1)
    forced = (blk == 0) | (blk == cur) | (blk == cur - 1)
    ii = lax.broadcasted_iota(jnp.int32, (nsp, nsp), 0)
    jj = lax.broadcasted_iota(jnp.int32, (nsp, nsp), 1)
    slot = lax.broadcasted_iota(jnp.int32, (SEL_TOPK, nsp), 0).astype(F32)
    blk_f = lax.broadcasted_iota(jnp.int32, (SEL_TOPK, nsp), 1).astype(F32)
    for h in range(nh):
        score = jnp.where(blk > cur, -1.0, jnp.where(forced, FORCE_SCORE, p_sel[h:h + 1, :]))
        score = jnp.where(blk >= nb, -2.0, score)
        s_col = _to_col(score, nsp)
        ge = jnp.where(s_col >= score, 1.0, 0.0)
        gt = jnp.where(s_col > score, 1.0, 0.0)
        rank = jnp.sum(jnp.where(jj > ii, ge, gt), axis=0, keepdims=True)
        sel = (rank < SEL_TOPK).astype(F32)
        before = jnp.sum(jnp.where(ii < jj, _to_col(sel, nsp), 0.0), axis=0, keepdims=True)
        onehot = jnp.where((before == slot) & (sel > 0.5), 1.0, 0.0)
        idx = jnp.sum(onehot * blk_f, axis=1, keepdims=True)
        idx_ref[0, h] = jnp.broadcast_to(idx, (SEL_TOPK, LANES)).astype(jnp.int32)

    wb = wc_ref.shape[1]
    wc = wc_ref[0]
    wnew = wnew_ref[0]
    s = _dot_nt(q8, wc[:, 0:KV_W].astype(BF16))
    s = jnp.where(lax.broadcasted_iota(jnp.int32, s.shape, 1) > wb - WINDOW, s, NEG)
    qf = q8.astype(F32)
    s_new = jnp.sum(qf * wnew[:, 0:KV_W].astype(BF16).astype(F32), axis=-1, keepdims=True)
    mx = jnp.maximum(jnp.max(s, axis=-1, keepdims=True), s_new)
    e = jnp.exp(s - mx)
    e_new = jnp.exp(s_new - mx)
    num = _dot(e.astype(BF16), wc[:, KV_W:2 * KV_W].astype(BF16)) + e_new * wnew[:, KV_W:2 * KV_W]
    ow_ref[0] = num / (jnp.sum(e, axis=-1, keepdims=True) + e_new)


def _dec_constants(past):
    n_groups = past // CMP_STRIDE
    nb = past // SEL_BLOCK + 1
    nsp = -(-nb // LANES) * LANES
    j = np.arange(nsp)[None, :]
    n = np.arange(n_groups)[:, None]
    first = (n >= SEL_RATIO * j) & (n <= SEL_RATIO * j + SEL_RATIO - 1)
    second = (n >= SEL_RATIO * j - 1) & (n <= SEL_RATIO * j + SEL_RATIO - 2)
    m = (first.astype(np.float32) + second.astype(np.float32)) * (n < n_groups - 1) * (j < nb)
    return n_groups, nb, jnp.asarray(m, BF16)


def _dec_cmp_attn(gsum, q8, win_cache, win_new, cw, batch, past):
    n_groups, nb, m = _dec_constants(past)
    nsp = m.shape[1]
    wb = win_cache.shape[1]
    per_b = lambda shape: pl.BlockSpec((1,) + shape, lambda b: (b,) + (0,) * len(shape))
    kern = functools.partial(_dec_cmp_kernel, n_groups=n_groups, nb=nb, cur=past // SEL_BLOCK)
    return pl.pallas_call(
        kern,
        grid=(batch,),
        in_specs=[per_b((n_groups, CMP_G)), per_b((NSA_HEADS, LANES)), per_b((wb, 2 * KV_W)),
                  per_b((1, 2 * KV_W))] + _cmp_specs() + [_const_spec((n_groups, nsp))],
        out_specs=[per_b((NSA_HEADS, LANES)), per_b((NSA_HEADS, LANES)),
                   per_b((NSA_KV_HEADS, SEL_TOPK, LANES))],
        out_shape=[jax.ShapeDtypeStruct((batch, NSA_HEADS, LANES), F32),
                   jax.ShapeDtypeStruct((batch, NSA_HEADS, LANES), F32),
                   jax.ShapeDtypeStruct((batch, NSA_KV_HEADS, SEL_TOPK, LANES), jnp.int32)],
        compiler_params=_params("parallel"),
        name="nsa_decode_cmp",
    )(gsum, q8, win_cache, win_new, *cw, m)


def _dec_sel_kernel(pt_ref, idx_ref, q_ref, new_ref, oc_ref, ow_ref, gate_ref, cache_ref, out_ref,
                    kv_buf, sems, *, n_pages, n_cache_blocks):
    b = pl.program_id(0)
    nh, nq, topk = NSA_KV_HEADS, NSA_HEADS, SEL_TOPK
    half = PAGE_SIZE // SEL_BLOCK

    def block_copy(h, k):
        blk = jnp.minimum(idx_ref[(b * nh + h) * topk + k], n_cache_blocks - 1)
        row = pt_ref[b * n_pages + blk // half] * half + blk % half
        slot = h * topk + k
        return pltpu.make_async_copy(cache_ref.at[row, :, pl.ds(2 * KV_W, 2 * KV_W)], kv_buf.at[slot],
                                     sems.at[slot])
    for h in range(nh):
        for k in range(topk):
            block_copy(h, k).start()
    for h in range(nh):
        for k in range(topk):
            block_copy(h, k).wait()

    q8 = q_ref[0]
    new = new_ref[0]
    s_new = jnp.sum(q8 * new[:, 2 * KV_W:3 * KV_W].astype(BF16).astype(F32), axis=-1, keepdims=True)
    v_new = new[:, 3 * KV_W:4 * KV_W]
    n_keys = topk * SEL_BLOCK
    kblk = lax.broadcasted_iota(jnp.int32, (1, n_keys), 1) // SEL_BLOCK
    o_heads = []
    for h in range(nh):
        kv = kv_buf[h * topk:(h + 1) * topk].reshape(n_keys, 2 * KV_W)
        s = _dot_nt(q8.astype(BF16), kv[:, 0:KV_W].astype(BF16))
        pen = jnp.zeros((1, n_keys), F32)
        for k in range(topk):
            blk = idx_ref[(b * nh + h) * topk + k]
            pen = jnp.where(kblk == k, jnp.where(blk < n_cache_blocks, 0.0, NEG), pen)
        s = s + pen
        m = jnp.maximum(jnp.max(s, axis=-1, keepdims=True), s_new)
        e = jnp.exp(s - m)
        e_new = jnp.exp(s_new - m)
        num = _dot(e.astype(BF16), kv[:, KV_W:2 * KV_W].astype(BF16)) + e_new * v_new
        o_heads.append(num / (jnp.sum(e, axis=-1, keepdims=True) + e_new))
    row = lax.broadcasted_iota(jnp.int32, (nq, KV_W), 0)
    o_s = jnp.where(row < NSA_GROUP, o_heads[0], o_heads[1])
    gt = gate_ref[0]
    out_ref[0] = gt[:, 0:1] * oc_ref[0] + gt[:, 1:2] * o_s + gt[:, 2:3] * ow_ref[0]


def _dec_sel_attn(cache, pt_flat, idx_flat, q8, rows_new, o_c, o_w, gates, batch, n_pages):
    n_pool = cache.shape[0]
    half = PAGE_SIZE // SEL_BLOCK
    cache3 = cache.reshape(n_pool * half, SEL_BLOCK, ROWS_W)
    nh, grp, nq = NSA_KV_HEADS, NSA_GROUP, NSA_HEADS
    g3 = gates.reshape(batch, nh, LANES)[:, :, 0:3 * grp].reshape(batch, nq, 3)
    g8 = jnp.concatenate([g3, jnp.zeros((batch, nq, LANES - 3), F32)], axis=-1)
    new8 = jnp.broadcast_to(rows_new.reshape(batch, 1, ROWS_W), (batch, nq, ROWS_W))

    per_b = lambda w: pl.BlockSpec((1, nq, w), lambda b, pt, idx: (b, 0, 0))
    grid_spec = pltpu.PrefetchScalarGridSpec(
        num_scalar_prefetch=2,
        grid=(batch,),
        in_specs=[per_b(LANES), per_b(ROWS_W), per_b(LANES), per_b(LANES), per_b(LANES),
                  pl.BlockSpec(memory_space=pl.ANY)],
        out_specs=per_b(LANES),
        scratch_shapes=[pltpu.VMEM((nh * SEL_TOPK, SEL_BLOCK, 2 * KV_W), F32),
                        pltpu.SemaphoreType.DMA((nh * SEL_TOPK,))],
    )
    kern = functools.partial(_dec_sel_kernel, n_pages=n_pages, n_cache_blocks=n_pages * half)
    return pl.pallas_call(
        kern,
        grid_spec=grid_spec,
        out_shape=jax.ShapeDtypeStruct((batch, nq, LANES), F32),
        compiler_params=_params("arbitrary"),
        name="nsa_decode_sel",
    )(pt_flat, idx_flat, q8.astype(F32), new8, o_c, o_w, g8, cache3)


def _sample_layer(x, lp, lw, cache, win_cache, gla_s0, c0, n0, m0, conv0, page_table):
    batch, n_pages = page_table.shape
    past = n_pages * PAGE_SIZE
    d = HEAD_DIM
    c = SCAN_CHUNK
    zg, zn, zm = _in_proj(x, lp['norm1_g'].reshape(1, D_MODEL), lw['w_in'], batch)
    qpad, rows, win, _, _, gates = _nsa_prep(zn, lp['nsa_qk_g'], lp['nsa_gate_b'], batch)
    pad = lambda z: jnp.pad(z[:, None, :], ((0, 0), (0, c - 1), (0, 0))).reshape(batch * c, z.shape[-1])
    first = lambda o: o.reshape(batch, c, o.shape[-1])[:, 0]
    og, s_gla = _gla(pad(zg), gla_s0, lp['gla_a_w'], lp['gla_a_b'], lp['gla_norm_g'], batch, c, 1)
    om, c_m, n_m, m_m = _mlstm(pad(zm), conv0, c0, n0, m0, lp, batch, c, 1)
    pt_flat = page_table.reshape(-1)
    gsum = _cmp_gather(cache, pt_flat, lw['cmp'][0], batch, n_pages)
    q8 = qpad.reshape(batch, NSA_HEADS, LANES)
    wb = win_cache.shape[1]
    o_c, o_w, idx = _dec_cmp_attn(gsum, q8, win_cache.reshape(batch, wb, 2 * KV_W),
                                  win.reshape(batch, 1, 2 * KV_W), lw['cmp'], batch, past)
    o_n = _dec_sel_attn(cache, pt_flat, idx[:, :, :, 0].reshape(-1), q8, rows, o_c, o_w, gates,
                        batch, n_pages)
    on = jnp.stack([o_n[:, h * NSA_GROUP:(h + 1) * NSA_GROUP, h * d:(h + 1) * d]
                    for h in range(NSA_KV_HEADS)], axis=1)
    on = on.reshape(batch, NSA_W).astype(BF16)
    y = _out_ffn(x, first(og), on, first(om), lw['w_out'], lp['norm2_g'].reshape(1, D_MODEL),
                 lw['w_up'], lw['w_down'], batch)
    new_rows = rows.reshape(batch, 1, 4, NSA_KV_HEADS, d)
    new_win = win.reshape(batch, 1, 2, NSA_KV_HEADS, d)
    win_cat = jnp.concatenate([win_cache.astype(F32), new_win], axis=1)
    win_state = win_cat[:, win_cat.shape[1] - min(WINDOW, past + 1):]
    conv_state = jnp.concatenate([conv0.astype(F32), zm[:, None, 0:MLSTM_W]], axis=1)[:, 1:]
    return y, (new_rows, win_state, s_gla, c_m, n_m, m_m, conv_state)


def kernel(x_prompt, x_sample, cache_nsa_kv, cache_nsa_win, state_gla, state_mlstm_c, state_mlstm_n,
           state_mlstm_m, state_mlstm_conv, page_table, norm1_g, w_in, gla_a_w, gla_a_b, gla_norm_g,
           nsa_qk_g, nsa_gate_b, cmp_pe, cmp_w1, cmp_b1, cmp_w2, cmp_b2, ml_conv_w, ml_conv_b, ml_wq, ml_wk,
           ml_gate_b, ml_norm_g, ml_skip, w_out, norm2_g, w_up, w_down):
    bp, t_len, _ = x_prompt.shape
    n_dec = x_sample.shape[0]
    depth = w_in.shape[0]
    x_p = x_prompt.reshape(bp * t_len, D_MODEL)
    x_s = x_sample.reshape(n_dec, D_MODEL)
    states_p, states_s = [], []
    for l in range(depth):
        lp = {'norm1_g': norm1_g[l], 'w_in': w_in[l], 'gla_a_w': gla_a_w[l], 'gla_a_b': gla_a_b[l],
              'gla_norm_g': gla_norm_g[l], 'nsa_qk_g': nsa_qk_g[l], 'nsa_gate_b': nsa_gate_b[l],
              'cmp_pe': cmp_pe[l], 'cmp_w1': cmp_w1[l], 'cmp_b1': cmp_b1[l], 'cmp_w2': cmp_w2[l],
              'cmp_b2': cmp_b2[l], 'ml_conv_w': ml_conv_w[l], 'ml_conv_b': ml_conv_b[l], 'ml_wq': ml_wq[l],
              'ml_wk': ml_wk[l], 'ml_gate_b': ml_gate_b[l], 'ml_norm_g': ml_norm_g[l], 'ml_skip': ml_skip[l],
              'w_out': w_out[l], 'norm2_g': norm2_g[l], 'w_up': w_up[l], 'w_down': w_down[l]}
        lw = _layer_weights(lp)
        x_p, st_p = _prompt_layer(x_p, lp, lw, bp, t_len)
        x_s, st_s = _sample_layer(x_s, lp, lw, cache_nsa_kv[l], cache_nsa_win[l], state_gla[l],
                                  state_mlstm_c[l], state_mlstm_n[l], state_mlstm_m[l], state_mlstm_conv[l],
                                  page_table)
        states_p.append(st_p)
        states_s.append(st_s)
    outs = [x_p.reshape(bp, t_len, D_MODEL), x_s.reshape(n_dec, 1, D_MODEL)]
    for i in range(7):
        outs.append(jnp.stack([s[i] for s in states_p]))
        outs.append(jnp.stack([s[i] for s in states_s]))
    return tuple(outs)
```

```python
import functools

import numpy as np
import jax
import jax.numpy as jnp
from jax import lax
from jax.experimental import pallas as pl
from jax.experimental.pallas import tpu as pltpu

F32 = jnp.float32
BF16 = jnp.bfloat16

D_MODEL = 1024
HEAD_DIM = 64
PAGE_SIZE = 128
GLA_HEADS = 4
GLA_LOWRANK = 16
GLA_TAU = 16.0
NSA_HEADS = 8
NSA_KV_HEADS = 2
NSA_GROUP = NSA_HEADS // NSA_KV_HEADS
CMP_BLOCK = 32
CMP_STRIDE = 16
CMP_HIDDEN = 64
SEL_BLOCK = 64
SEL_RATIO = SEL_BLOCK // CMP_STRIDE
SEL_TOPK = 16
WINDOW = 512
FORCE_SCORE = 1.0e4
MLSTM_HEADS = 4
CONV_WIDTH = 4
D_FF = 4 * D_MODEL
EPS = 1e-6

GLA_W = GLA_HEADS * HEAD_DIM
NSA_W = NSA_HEADS * HEAD_DIM
KV_W = NSA_KV_HEADS * HEAD_DIM
MLSTM_W = MLSTM_HEADS * HEAD_DIM

LANES = 128
ZG_W = 4 * GLA_W + LANES
GATE_W = NSA_KV_HEADS * LANES
ZN_W = NSA_W + 6 * KV_W + GATE_W
ZM_W = 3 * MLSTM_W + LANES
ROWS_W = 4 * KV_W
WIN_W = 2 * KV_W
QPAD_W = NSA_HEADS * LANES

SCAN_CHUNK = 128
GLA_SUB = 32
NSA_TQ = 128
NSA_TK = 256
NSA_TK_FIXED = 512
NSA_MAX_FIXED_SHIFT = 40.0
NEG = -1.0e30
SEL_BIAS = 29952.0

VMEM_LIMIT = 56 * 1024 * 1024


def _dot(a, b):
    return jnp.dot(a, b, preferred_element_type=F32)


def _dot_hi(a, b):
    return jnp.dot(a, b, preferred_element_type=F32, precision=lax.Precision.HIGHEST)


def _dot_nt(a, b):
    return lax.dot_general(a, b, (((1,), (1,)), ((), ())), preferred_element_type=F32)


def _dot_nt_hi(a, b):
    return lax.dot_general(a, b, (((1,), (1,)), ((), ())), preferred_element_type=F32,
                           precision=lax.Precision.HIGHEST)


def _dot_tn_hi(a, b):
    return lax.dot_general(a, b, (((0,), (0,)), ((), ())), preferred_element_type=F32,
                           precision=lax.Precision.HIGHEST)


def _sigmoid(x):
    return 1.0 / (1.0 + jnp.exp(-x))


def _log_sigmoid(x):
    return jnp.minimum(x, 0.0) - jnp.log(1.0 + jnp.exp(-jnp.abs(x)))


def _group_mean_sq(x, ones_bd):
    sq = x * x
    hi = sq.astype(BF16)
    lo = (sq - hi.astype(F32)).astype(BF16)
    return (_dot(hi, ones_bd) + _dot(lo, ones_bd)) * (1.0 / HEAD_DIM)


def _params(*sem):
    return pltpu.CompilerParams(dimension_semantics=sem, vmem_limit_bytes=VMEM_LIMIT)


def _const_spec(shape):
    nd = len(shape)
    return pl.BlockSpec(shape, lambda *_: (0,) * nd)


def _inproj_kernel(x_ref, g_ref, w_ref, zg_ref, zn_ref, zm_ref):
    x = x_ref[...]
    h = x * lax.rsqrt(jnp.mean(x * x, axis=-1, keepdims=True) + EPS) * g_ref[...]
    hb = h.astype(BF16)
    zg_ref[...] = _dot(hb, w_ref[:, 0:ZG_W])
    zn_ref[...] = _dot(hb, w_ref[:, ZG_W:ZG_W + ZN_W])
    zm_ref[...] = _dot(hb, w_ref[:, ZG_W + ZN_W:ZG_W + ZN_W + ZM_W])


def _in_proj(x, g, w, tm):
    n = x.shape[0]
    zw = ZG_W + ZN_W + ZM_W
    return pl.pallas_call(
        _inproj_kernel,
        grid=(n // tm,),
        in_specs=[pl.BlockSpec((tm, D_MODEL), lambda i: (i, 0)),
                  _const_spec((1, D_MODEL)),
                  _const_spec((D_MODEL, zw))],
        out_specs=[pl.BlockSpec((tm, ZG_W), lambda i: (i, 0)),
                   pl.BlockSpec((tm, ZN_W), lambda i: (i, 0)),
                   pl.BlockSpec((tm, ZM_W), lambda i: (i, 0))],
        out_shape=[jax.ShapeDtypeStruct((n, ZG_W), F32),
                   jax.ShapeDtypeStruct((n, ZN_W), F32),
                   jax.ShapeDtypeStruct((n, ZM_W), F32)],
        compiler_params=_params("parallel"),
        name="in_proj",
    )(x, g, w)


def _pack_w_in(w_in):
    def cols(a, b):
        return w_in[:, a:b]

    def zeros(n):
        return jnp.zeros((D_MODEL, n), w_in.dtype)
    o_nsa = 4 * GLA_W + GLA_LOWRANK
    o_ng = o_nsa + NSA_W + 6 * KV_W
    o_ml = o_ng + 3 * NSA_HEADS
    o_mi = o_ml + 2 * MLSTM_W
    o_mo = o_mi + 2 * MLSTM_HEADS
    n_gate = 3 * NSA_GROUP
    parts = [cols(0, o_nsa), zeros(LANES - GLA_LOWRANK),
             cols(o_nsa, o_ng),
             cols(o_ng, o_ng + n_gate), zeros(LANES - n_gate),
             cols(o_ng + n_gate, o_ml), zeros(LANES - n_gate),
             cols(o_ml, o_mi), cols(o_mo, o_mo + MLSTM_W), cols(o_mi, o_mo),
             zeros(LANES - 2 * MLSTM_HEADS)]
    return jnp.concatenate(parts, axis=1).astype(BF16)


def _nsa_prep_kernel(zn_ref, gq_ref, gr_ref, gw_ref, gb_ref, ones_ref, place_ref,
                     qpad_ref, rows_ref, win_ref, kv_ref, r01_ref, gate_ref, *, transposed):
    ones_bd = ones_ref[...]
    q = zn_ref[:, 0:NSA_W]
    qn = q * lax.rsqrt(_group_mean_sq(q, ones_bd) + EPS) * gq_ref[...]
    qpad_ref[...] = _dot(qn.astype(BF16), place_ref[...]).astype(BF16)

    r = zn_ref[:, NSA_W:NSA_W + ROWS_W]
    col = lax.broadcasted_iota(jnp.int32, r.shape, 1)
    rn = r * lax.rsqrt(_group_mean_sq(r, ones_bd) + EPS) * gr_ref[...]
    rows = jnp.where((col >= 2 * KV_W) & (col < 3 * KV_W), rn, r)
    if transposed:
        rows_ref[0] = rows.T
    else:
        rows_ref[...] = rows
    r01_ref[...] = rows[:, 0:2 * KV_W].astype(BF16)

    w = zn_ref[:, NSA_W + ROWS_W:NSA_W + ROWS_W + WIN_W]
    colw = lax.broadcasted_iota(jnp.int32, w.shape, 1)
    wn = w * lax.rsqrt(_group_mean_sq(w, ones_bd[0:WIN_W, 0:WIN_W]) + EPS) * gw_ref[...]
    win = jnp.where(colw < KV_W, wn, w)
    if transposed:
        win_ref[0] = win.T
    else:
        win_ref[...] = win
    kv_ref[:, 0:2 * KV_W] = rows[:, 2 * KV_W:4 * KV_W].astype(BF16)
    kv_ref[:, 2 * KV_W:4 * KV_W] = win.astype(BF16)

    gate_ref[...] = _sigmoid(zn_ref[:, NSA_W + ROWS_W + WIN_W:ZN_W] + gb_ref[...])


def _nsa_prep(zn, qk_g, gate_b, tm, seq=None):
    n = zn.shape[0]
    scale = HEAD_DIM ** -0.5
    gq = (jnp.tile(qk_g[0], NSA_HEADS) * scale).reshape(1, NSA_W)
    gr = jnp.tile(qk_g[2], ROWS_W // HEAD_DIM).reshape(1, ROWS_W)
    gw = jnp.tile(qk_g[3], WIN_W // HEAD_DIM).reshape(1, WIN_W)
    n_gate = 3 * NSA_GROUP
    gpad = jnp.zeros((LANES - n_gate,), F32)
    gb = jnp.concatenate([gate_b[0:n_gate], gpad, gate_b[n_gate:], gpad]).reshape(1, GATE_W)
    grp = np.arange(NSA_W) // HEAD_DIM
    ones_bd = jnp.asarray(grp[:, None] == grp[None, :], BF16)
    src = np.arange(NSA_W)
    head, d = src // HEAD_DIM, src % HEAD_DIM
    dst = head * LANES + (head // NSA_GROUP) * HEAD_DIM + d
    place = np.zeros((NSA_W, QPAD_W), np.float32)
    place[src, dst] = 1.0
    place = jnp.asarray(place, BF16)
    row = lambda w: pl.BlockSpec((tm, w), lambda i: (i, 0))
    if seq is None:
        state_spec = row
        state_shape = lambda w: jax.ShapeDtypeStruct((n, w), F32)
    else:
        batch, t_len = seq
        nblk = t_len // tm
        state_spec = lambda w: pl.BlockSpec((1, w, tm), lambda i: (i // nblk, 0, i % nblk))
        state_shape = lambda w: jax.ShapeDtypeStruct((batch, w, t_len), F32)
    return pl.pallas_call(
        functools.partial(_nsa_prep_kernel, transposed=seq is not None),
        grid=(n // tm,),
        in_specs=[row(ZN_W), _const_spec((1, NSA_W)), _const_spec((1, ROWS_W)),
                  _const_spec((1, WIN_W)), _const_spec((1, GATE_W)),
                  _const_spec((NSA_W, NSA_W)), _const_spec((NSA_W, QPAD_W))],
        out_specs=[row(QPAD_W), state_spec(ROWS_W), state_spec(WIN_W), row(4 * KV_W), row(2 * KV_W),
                   row(GATE_W)],
        out_shape=[jax.ShapeDtypeStruct((n, QPAD_W), BF16),
                   state_shape(ROWS_W),
                   state_shape(WIN_W),
                   jax.ShapeDtypeStruct((n, 4 * KV_W), BF16),
                   jax.ShapeDtypeStruct((n, 2 * KV_W), BF16),
                   jax.ShapeDtypeStruct((n, GATE_W), F32)],
        compiler_params=_params("parallel"),
        name="nsa_prep",
    )(zn, gq, gr, gw, gb, ones_bd, place)


def _gla_kernel(zg_ref, s0_ref, aw_ref, ab_ref, ng_ref, tri_ref, ones_ref, og_ref, sout_ref, s_scr,
                *, t_valid, t_pad):
    i = pl.program_id(1)
    c = SCAN_CHUNK
    d = HEAD_DIM

    @pl.when(i == 0)
    def _():
        s_scr[...] = s0_ref[0]

    q = zg_ref[:, 0:GLA_W] * (d ** -0.5)
    k = zg_ref[:, GLA_W:2 * GLA_W]
    v = zg_ref[:, 2 * GLA_W:3 * GLA_W]
    r = zg_ref[:, 3 * GLA_W:4 * GLA_W]
    ga = zg_ref[:, 4 * GLA_W:ZG_W]
    g = _log_sigmoid(_dot(ga.astype(BF16), aw_ref[...]) + ab_ref[...]) * (1.0 / GLA_TAU)
    if t_valid < t_pad:
        valid = (i * c + lax.broadcasted_iota(jnp.int32, (c, 1), 0)) < t_valid
        g = jnp.where(valid, g, 0.0)
        k = jnp.where(valid, k, 0.0)
    bcum = _dot_hi(tri_ref[...], g)
    b_end = bcum[c - 1:c, :]
    last = (lax.broadcasted_iota(jnp.int32, (c, LANES), 0) == c - 1).astype(F32)
    row_i = lax.broadcasted_iota(jnp.int32, (GLA_SUB, GLA_SUB), 0)
    col_i = lax.broadcasted_iota(jnp.int32, (GLA_SUB, GLA_SUB), 1)
    o_heads = []
    for h in range(GLA_HEADS):
        sl = slice(h * d, (h + 1) * d)
        qh, kh, vh, bh = q[:, sl], k[:, sl], v[:, sl], bcum[:, sl]
        vb = vh.astype(BF16)
        s_h = s_scr[h]
        o_inter = _dot((qh * jnp.exp(bh)).astype(BF16), s_h.astype(BF16))
        o_sub = []
        for j in range(c // GLA_SUB):
            lo, hi = j * GLA_SUB, (j + 1) * GLA_SUB
            base = bh[lo - 1:lo, :] if j > 0 else jnp.zeros((1, d), F32)
            qt = (qh[lo:hi] * jnp.exp(bh[lo:hi] - base)).astype(BF16)
            kt = (kh[0:hi] * jnp.exp(base - bh[0:hi])).astype(BF16)
            sc = _dot_nt(qt, kt)
            diag = jnp.where(col_i <= row_i, sc[:, lo:hi], 0.0)
            if j > 0:
                sc = jnp.concatenate([sc[:, 0:lo], diag], axis=1)
            else:
                sc = diag
            o_sub.append(_dot(sc.astype(BF16), vb[0:hi]))
        o_heads.append(jnp.concatenate(o_sub, axis=0) + o_inter)
        decay_col = jnp.exp(_dot_tn_hi(bh, last)[:, 0:d])
        k_hat = kh * jnp.exp(b_end[:, sl] - bh)
        s_scr[h] = decay_col * s_h + _dot_tn_hi(k_hat, vh)
    o = jnp.concatenate(o_heads, axis=1)
    on = o * lax.rsqrt(_group_mean_sq(o, ones_ref[...]) + EPS) * ng_ref[...]
    og_ref[...] = (on * (r * _sigmoid(r))).astype(BF16)

    @pl.when(i == pl.num_programs(1) - 1)
    def _():
        sout_ref[0] = s_scr[...]


def _gla(zg, s0, a_w, a_b, norm_g, batch, t_pad, t_valid):
    c = SCAN_CHUNK
    nblk = t_pad // c
    aw = jnp.concatenate([a_w, jnp.zeros((LANES - GLA_LOWRANK, GLA_W), F32)], axis=0).astype(BF16)
    tri = jnp.asarray(np.tril(np.ones((c, c), np.float32)))
    grp = np.arange(GLA_W) // HEAD_DIM
    ones_bd = jnp.asarray(grp[:, None] == grp[None, :], BF16)
    kern = functools.partial(_gla_kernel, t_valid=t_valid, t_pad=t_pad)
    return pl.pallas_call(
        kern,
        grid=(batch, nblk),
        in_specs=[pl.BlockSpec((c, ZG_W), lambda b, i: (b * nblk + i, 0)),
                  pl.BlockSpec((1, GLA_HEADS, HEAD_DIM, HEAD_DIM), lambda b, i: (b, 0, 0, 0)),
                  _const_spec((LANES, GLA_W)), _const_spec((1, GLA_W)), _const_spec((1, GLA_W)),
                  _const_spec((c, c)), _const_spec((GLA_W, GLA_W))],
        out_specs=[pl.BlockSpec((c, GLA_W), lambda b, i: (b * nblk + i, 0)),
                   pl.BlockSpec((1, GLA_HEADS, HEAD_DIM, HEAD_DIM), lambda b, i: (b, 0, 0, 0))],
        out_shape=[jax.ShapeDtypeStruct((batch * t_pad, GLA_W), BF16),
                   jax.ShapeDtypeStruct((batch, GLA_HEADS, HEAD_DIM, HEAD_DIM), F32)],
        scratch_shapes=[pltpu.VMEM((GLA_HEADS, HEAD_DIM, HEAD_DIM), F32)],
        compiler_params=_params("parallel", "arbitrary"),
        name="gla_scan",
    )(zg, s0, aw, a_b.reshape(1, GLA_W), jnp.tile(norm_g, GLA_HEADS).reshape(1, GLA_W), tri, ones_bd)


def _mlstm_kernel(zm_ref, conv0_ref, c0_ref, n0_ref, m0_ref, cw_ref, cb_ref, wq_ref, wk_ref, gb_ref,
                  ng_ref, skip_ref, tri_ref, ones_ref, om_ref, cout_ref, nout_ref, mout_ref,
                  c_scr, n_scr, m_scr, ext_scr, *, t_valid, t_pad):
    i = pl.program_id(1)
    c = SCAN_CHUNK
    d = HEAD_DIM
    nh = MLSTM_HEADS

    @pl.when(i == 0)
    def _():
        c_scr[...] = c0_ref[0]
        n_scr[...] = n0_ref[0]
        m_scr[...] = m0_ref[0]
        ext_scr[0:8, :] = conv0_ref[0]

    mu = zm_ref[:, 0:MLSTM_W]
    ext_scr[8:8 + c, :] = mu
    u_conv = cb_ref[...] + mu * cw_ref[CONV_WIDTH - 1:CONV_WIDTH, :]
    for j in range(1, CONV_WIDTH):
        u_conv = u_conv + ext_scr[8 - j:8 - j + c, :] * cw_ref[CONV_WIDTH - 1 - j:CONV_WIDTH - j, :]
    ext_scr[0:8, :] = mu[c - 8:c, :]
    u_act = u_conv * _sigmoid(u_conv)
    ub = u_act.astype(BF16)
    q = _dot(ub, wq_ref[...])
    k = _dot(ub, wk_ref[...]) * (d ** -0.5)
    v = zm_ref[:, MLSTM_W:2 * MLSTM_W]
    og = zm_ref[:, 2 * MLSTM_W:3 * MLSTM_W]
    gz = zm_ref[:, 3 * MLSTM_W:ZM_W] + gb_ref[...]
    lane = lax.broadcasted_iota(jnp.int32, (c, LANES), 1)
    x = jnp.where(lane < nh, gz, _log_sigmoid(gz))
    if t_valid < t_pad:
        valid = (i * c + lax.broadcasted_iota(jnp.int32, (c, 1), 0)) < t_valid
        x = jnp.where(valid, x, jnp.where(lane < nh, NEG, 0.0))
    fc = _dot_hi(tri_ref[...], x)
    x = jnp.where(lane < nh, x, fc)
    sel = (lax.broadcasted_iota(jnp.int32, (8, LANES), 0)
           == lax.broadcasted_iota(jnp.int32, (8, LANES), 1)).astype(F32)
    xt = _dot_nt_hi(sel, x)
    row_i = lax.broadcasted_iota(jnp.int32, (c, c), 0)
    col_i = lax.broadcasted_iota(jnp.int32, (c, c), 1)
    h_heads = []
    for h in range(nh):
        sl = slice(h * d, (h + 1) * d)
        qh, kh, vh = q[:, sl], k[:, sl], v[:, sl]
        i_col, f_col = x[:, h:h + 1], x[:, nh + h:nh + h + 1]
        i_row, f_row = xt[h:h + 1, :], xt[nh + h:nh + h + 1, :]
        m_prev = m_scr[h:h + 1, 0:1]
        c_h = c_scr[h]
        n_h = n_scr[h:h + 1, :]
        a = f_col + m_prev
        dmat = jnp.where(col_i <= row_i, f_col - f_row + i_row, NEG)
        m_t = jnp.maximum(a, jnp.max(dmat, axis=-1, keepdims=True))
        w_state = jnp.exp(a - m_t)
        qb = qh.astype(BF16)
        qk = _dot_nt(qb, kh.astype(BF16)) * jnp.exp(dmat - m_t)
        num = _dot(qk.astype(BF16), vh.astype(BF16)) + w_state * _dot(qb, c_h.astype(BF16))
        den = jnp.sum(qk, axis=-1, keepdims=True) + w_state * jnp.sum(qh * n_h, axis=-1, keepdims=True)
        h_heads.append(num / jnp.maximum(jnp.abs(den), jnp.exp(-m_t)))
        m_new = m_t[c - 1:c, :]
        w_carry = jnp.exp(a[c - 1:c, :] - m_new)
        w_last = jnp.exp(f_col[c - 1:c, :] - f_col + i_col - m_new)
        kw = kh * w_last
        c_scr[h] = w_carry * c_h + _dot_tn_hi(kw, vh)
        n_scr[h:h + 1, :] = w_carry * n_h + jnp.sum(kw, axis=0, keepdims=True)
        m_scr[h:h + 1, :] = jnp.broadcast_to(m_new, (1, LANES))
    hm = jnp.concatenate(h_heads, axis=1)
    hn = hm * lax.rsqrt(_group_mean_sq(hm, ones_ref[...]) + EPS) * ng_ref[...]
    om_ref[...] = (_sigmoid(og) * (hn + skip_ref[...] * u_act)).astype(BF16)

    @pl.when(i == pl.num_programs(1) - 1)
    def _():
        cout_ref[0] = c_scr[...]
        nout_ref[0] = n_scr[...]
        mout_ref[0] = m_scr[...]


def _block_diag_heads(w):
    nh, d, _ = w.shape
    eye = jnp.eye(nh, dtype=w.dtype)
    return jnp.einsum('hde,hg->hdge', w, eye).reshape(nh * d, nh * d)


def _mlstm(zm, conv0, c0, n0, m0, lp, batch, t_pad, t_valid):
    c = SCAN_CHUNK
    nblk = t_pad // c
    nh, d = MLSTM_HEADS, HEAD_DIM
    conv0p = jnp.concatenate([jnp.zeros((batch, 8 - (CONV_WIDTH - 1), MLSTM_W), F32), conv0.astype(F32)], axis=1)
    n0p = jnp.concatenate([n0, jnp.zeros((batch, 8 - nh, d), F32)], axis=1)
    m0p = jnp.concatenate([jnp.broadcast_to(m0[:, :, None], (batch, nh, LANES)),
                           jnp.zeros((batch, 8 - nh, LANES), F32)], axis=1)
    gb = jnp.concatenate([lp['ml_gate_b'][0], lp['ml_gate_b'][1],
                          jnp.zeros((LANES - 2 * nh,), F32)]).reshape(1, LANES)
    tri = jnp.asarray(np.tril(np.ones((c, c), np.float32)))
    grp = np.arange(MLSTM_W) // d
    ones_bd = jnp.asarray(grp[:, None] == grp[None, :], BF16)
    kern = functools.partial(_mlstm_kernel, t_valid=t_valid, t_pad=t_pad)
    per_b = lambda shape: pl.BlockSpec((1,) + shape, lambda b, i: (b,) + (0,) * len(shape))
    om, c_f, n_f, m_f = pl.pallas_call(
        kern,
        grid=(batch, nblk),
        in_specs=[pl.BlockSpec((c, ZM_W), lambda b, i: (b * nblk + i, 0)),
                  per_b((8, MLSTM_W)), per_b((nh, d, d)), per_b((8, d)), per_b((8, LANES)),
                  _const_spec((CONV_WIDTH, MLSTM_W)), _const_spec((1, MLSTM_W)),
                  _const_spec((MLSTM_W, MLSTM_W)), _const_spec((MLSTM_W, MLSTM_W)),
                  _const_spec((1, LANES)), _const_spec((1, MLSTM_W)), _const_spec((1, MLSTM_W)),
                  _const_spec((c, c)), _const_spec((MLSTM_W, MLSTM_W))],
        out_specs=[pl.BlockSpec((c, MLSTM_W), lambda b, i: (b * nblk + i, 0)),
                   per_b((nh, d, d)), per_b((8, d)), per_b((8, LANES))],
        out_shape=[jax.ShapeDtypeStruct((batch * t_pad, MLSTM_W), BF16),
                   jax.ShapeDtypeStruct((batch, nh, d, d), F32),
                   jax.ShapeDtypeStruct((batch, 8, d), F32),
                   jax.ShapeDtypeStruct((batch, 8, LANES), F32)],
        scratch_shapes=[pltpu.VMEM((nh, d, d), F32), pltpu.VMEM((8, d), F32),
                        pltpu.VMEM((8, LANES), F32), pltpu.VMEM((8 + c, MLSTM_W), F32)],
        compiler_params=_params("parallel", "arbitrary"),
        name="mlstm_scan",
    )(zm, conv0p, c0, n0p, m0p, lp['ml_conv_w'], lp['ml_conv_b'].reshape(1, MLSTM_W),
      _block_diag_heads(lp['ml_wq']).astype(BF16), _block_diag_heads(lp['ml_wk']).astype(BF16),
      gb, jnp.tile(lp['ml_norm_g'], nh).reshape(1, MLSTM_W), lp['ml_skip'].reshape(1, MLSTM_W),
      tri, ones_bd)
    return om, c_f, n_f[:, :nh], m_f[:, :nh, 0]


CMP_IN = CMP_STRIDE * 2 * KV_W
CMP_G = 2 * 2 * KV_W


def _cmp_weights(lp):
    w1 = lp['cmp_w1'].reshape(2, 2, CMP_STRIDE, HEAD_DIM, CMP_HIDDEN)
    eye = jnp.eye(2, dtype=F32)
    w_ab = jnp.einsum('sarde,st,hg->rshdatge', w1, eye, eye).reshape(CMP_IN, CMP_G)
    pe = lp['cmp_pe'].reshape(2, 2, CMP_STRIDE, 1, HEAD_DIM)
    pe = jnp.broadcast_to(jnp.transpose(pe, (1, 2, 0, 3, 4)),
                          (2, CMP_STRIDE, 2, NSA_KV_HEADS, HEAD_DIM)).reshape(2, CMP_IN)
    pe8 = jnp.concatenate([pe, jnp.zeros((6, CMP_IN), F32)], axis=0)
    b1 = jnp.broadcast_to(lp['cmp_b1'][:, None, :], (2, NSA_KV_HEADS, CMP_HIDDEN)).reshape(1, 2 * KV_W)
    w2 = jnp.einsum('sed,st,hg->shetgd', lp['cmp_w2'], eye, eye).reshape(2 * KV_W, 2 * KV_W)
    b2 = jnp.broadcast_to(lp['cmp_b2'][:, None, :], (2, NSA_KV_HEADS, HEAD_DIM)).reshape(1, 2 * KV_W)
    g1 = jnp.tile(lp['nsa_qk_g'][1], NSA_KV_HEADS).reshape(1, KV_W)
    grp = np.arange(KV_W) // HEAD_DIM
    ones_bd = jnp.asarray(grp[:, None] == grp[None, :], BF16)
    return (w_ab.astype(BF16), pe8.astype(BF16), b1, w2.astype(BF16), b2, g1, ones_bd)


def _cmp_finish(gsum, n_rows, wab_ref, pe_ref, b1_ref, w2_ref, b2_ref, g1_ref, ones_ref):
    half = 2 * KV_W
    g_pe = _dot(pe_ref[...], wab_ref[...])
    bias = g_pe[0:1, 0:half] + g_pe[1:2, half:CMP_G] + b1_ref[...]
    hid = gsum[:, 0:half] + pltpu.roll(gsum[:, half:CMP_G], n_rows - 1, 0) + bias
    act = hid * _sigmoid(hid)
    cmp = _dot(act.astype(BF16), w2_ref[...]) + b2_ref[...]
    kc = cmp[:, 0:KV_W]
    kc = kc * lax.rsqrt(_group_mean_sq(kc, ones_ref[...]) + EPS) * g1_ref[...]
    return kc, cmp[:, KV_W:half]


def _cmp_kernel(r_ref, wab_ref, pe_ref, b1_ref, w2_ref, b2_ref, g1_ref, ones_ref, kc_ref, vc_ref, *, n_rows):
    gsum = _dot(r_ref[0], wab_ref[...])
    kc, vc = _cmp_finish(gsum, n_rows, wab_ref, pe_ref, b1_ref, w2_ref, b2_ref, g1_ref, ones_ref)
    kc_ref[0] = kc.astype(BF16)
    vc_ref[0] = vc.astype(BF16)


def _cmp_specs():
    half = 2 * KV_W
    return [_const_spec((CMP_IN, CMP_G)), _const_spec((8, CMP_IN)), _const_spec((1, half)),
            _const_spec((half, half)), _const_spec((1, half)), _const_spec((1, KV_W)),
            _const_spec((KV_W, KV_W))]


def _compress(r01, cw, batch, t_len):
    n16 = t_len // CMP_STRIDE
    x = r01.reshape(batch, n16, CMP_IN)
    blk = lambda w: pl.BlockSpec((1, n16, w), lambda b: (b, 0, 0))
    return pl.pallas_call(
        functools.partial(_cmp_kernel, n_rows=n16),
        grid=(batch,),
        in_specs=[blk(CMP_IN)] + _cmp_specs(),
        out_specs=[blk(KV_W), blk(KV_W)],
        out_shape=[jax.ShapeDtypeStruct((batch, n16, KV_W), BF16)] * 2,
        compiler_params=_params("parallel"),
        name="nsa_compress",
    )(x, *cw)


def _nsa_kernel(qpad_ref, ksel_ref, vsel_ref, kwin_ref, vwin_ref, kc_ref, vc_ref, gate_ref, mt_ref,
                et_ref, pl_ref, bound_ref, out_ref, sc_scr, lhs_scr, m_scr, l_scr, acc_scr, *, nb, fixed_max):
    qi = pl.program_id(2)
    tq, grp = NSA_TQ, NSA_GROUP
    tk = NSA_TK_FIXED if fixed_max else NSA_TK
    rows = grp * tq
    nbp, ncp = mt_ref.shape
    start = qi * tq
    q4 = jnp.concatenate([qpad_ref[:, g * LANES:(g + 1) * LANES] for g in range(grp)], axis=0)
    tpos = start + (lax.broadcasted_iota(jnp.int32, (rows, 1), 0) & (tq - 1))

    sc = _dot_nt(q4, kc_ref[0])
    ccol = lax.broadcasted_iota(jnp.int32, (rows, ncp), 1)
    vis = (ccol * CMP_STRIDE + (CMP_BLOCK - 1)) <= tpos
    s_m = jnp.where(vis, sc, NEG)
    e = jnp.where(vis, jnp.exp(s_m - jnp.max(s_m, axis=-1, keepdims=True)), 0.0)
    p_c = e / jnp.maximum(jnp.sum(e, axis=-1, keepdims=True), 1e-30)
    o_c = _dot(p_c.astype(BF16), vc_ref[0])

    pg = p_c[0:tq]
    for g in range(1, grp):
        pg = pg + p_c[g * tq:(g + 1) * tq]
    h1 = pg.astype(BF16)
    r1 = pg - h1.astype(F32)
    h2 = r1.astype(BF16)
    h3 = (r1 - h2.astype(F32)).astype(BF16)
    mt = mt_ref[...]
    p_sel = _dot_nt(mt, h1) + _dot_nt(mt, h2) + _dot_nt(mt, h3)
    blk = lax.broadcasted_iota(jnp.int32, (nbp, tq), 0)
    cur = (start + lax.broadcasted_iota(jnp.int32, (nbp, tq), 1)) // SEL_BLOCK
    forced = (blk == 0) | (blk == cur) | (blk == cur - 1)
    score = jnp.where(blk > cur, -1.0, jnp.where(forced, FORCE_SCORE, p_sel))
    if nb < nbp:
        score = jnp.where(blk >= nb, -2.0, score)
    sc_scr[...] = score

    n_grp = nbp // 8
    s_grp = [score[8 * v:8 * v + 8, :] for v in range(n_grp)]
    cnt = [jnp.zeros((8, tq), F32) for _ in range(n_grp)]
    sub = lax.broadcasted_iota(jnp.int32, (8, tq), 0)
    for i in range(nb):
        row = sc_scr[i:i + 1, :]
        vi, ri = divmod(i, 8)
        for v in range(n_grp):
            if v > vi:
                hit = jnp.where(row >= s_grp[v], 1.0, 0.0)
            elif v < vi:
                hit = jnp.where(row > s_grp[v], 1.0, 0.0)
            else:
                hit = jnp.where(sub > ri, jnp.where(row >= s_grp[v], 1.0, 0.0),
                                jnp.where(row > s_grp[v], 1.0, 0.0))
            cnt[v] = cnt[v] + hit
    rank = jnp.concatenate(cnt, axis=0)
    unsel = jnp.where(rank < SEL_TOPK, 0.0, -1.0)
    unsel_t = unsel.T.astype(BF16)
    lhs_scr[...] = jnp.concatenate([q4, jnp.concatenate([unsel_t] * grp, axis=0)], axis=1)

    l_scr[...] = jnp.zeros((rows, LANES), F32)
    acc_scr[...] = jnp.zeros((rows, LANES), F32)
    n_full = start // tk

    def key_operands(kt):
        koff = pl.multiple_of(kt * tk, tk)
        kk = jnp.concatenate([ksel_ref[pl.ds(koff, tk), :], et_ref[pl.ds(koff, tk), :]], axis=1)
        return koff, kk, vsel_ref[pl.ds(koff, tk), :]

    if fixed_max:
        bound = bound_ref[0, 0]

        def key_tile(kt, masked):
            koff, kk, vv = key_operands(kt)
            s = _dot_nt(lhs_scr[...], kk)
            if masked:
                s = jnp.where(koff + lax.broadcasted_iota(jnp.int32, (rows, tk), 1) <= tpos, s, NEG)
            p = jnp.exp(s - bound)
            part = p[:, 0:LANES]
            for c in range(1, tk // LANES):
                part = part + p[:, c * LANES:(c + 1) * LANES]
            l_scr[...] = l_scr[...] + part
            acc_scr[...] = acc_scr[...] + _dot(p.astype(BF16), vv)
    else:
        m_scr[...] = jnp.full((rows, LANES), NEG, F32)
        qpos = start + lax.broadcasted_iota(jnp.int32, (tq, tk), 0)

        def key_tile(kt, masked):
            koff, kk, vv = key_operands(kt)
            for g in range(grp):
                rs = slice(g * tq, (g + 1) * tq)
                s = _dot_nt(lhs_scr[rs, :], kk)
                if masked:
                    s = jnp.where(koff + lax.broadcasted_iota(jnp.int32, (tq, tk), 1) <= qpos, s, NEG)
                m_old = m_scr[rs, :]
                m_new = jnp.maximum(m_old, jnp.max(s, axis=-1, keepdims=True))
                p = jnp.exp(s - jnp.concatenate([m_new] * (tk // LANES), axis=1))
                alpha = jnp.exp(m_old - m_new)
                l_scr[rs, :] = alpha * l_scr[rs, :] + jnp.sum(p, axis=-1, keepdims=True)
                acc_scr[rs, :] = alpha * acc_scr[rs, :] + _dot(p.astype(BF16), vv)
                m_scr[rs, :] = m_new

    def full_tile(kt, carry):
        key_tile(kt, False)
        return carry
    lax.fori_loop(0, n_full, full_tile, 0)
    key_tile(n_full, True)
    if fixed_max:
        o_s = acc_scr[...] / jnp.sum(l_scr[...], axis=-1, keepdims=True)
    else:
        o_s = acc_scr[...] / l_scr[...]

    wk = WINDOW + tq
    wstart = pl.multiple_of(jnp.maximum(start - WINDOW, 0), tq)
    s = _dot_nt(q4, kwin_ref[pl.ds(wstart, wk), :])
    wpos = wstart + lax.broadcasted_iota(jnp.int32, (rows, wk), 1)
    s = jnp.where((wpos <= tpos) & (wpos > tpos - WINDOW), s, NEG)
    e = jnp.exp(s - jnp.max(s, axis=-1, keepdims=True))
    o_w = _dot(e.astype(BF16), vwin_ref[pl.ds(wstart, wk), :]) / jnp.sum(e, axis=-1, keepdims=True)

    gt = gate_ref[...]

    def gate(branch):
        return jnp.concatenate([gt[0:tq, g * 3 + branch:g * 3 + branch + 1] for g in range(grp)], axis=0)
    o = (gate(0) * o_c + gate(1) * o_s + gate(2) * o_w).astype(BF16)
    out = _dot(o[0:tq], pl_ref[0, 0])
    for g in range(1, grp):
        out = out + _dot(o[g * tq:(g + 1) * tq], pl_ref[0, g])
    out_ref[...] = out.astype(BF16)


def _nsa_constants(t_len):
    nb = t_len // SEL_BLOCK
    nbp = max(LANES, -(-nb // LANES) * LANES)
    ncp = t_len // CMP_STRIDE
    j = np.arange(nbp)[:, None]
    n = np.arange(ncp)[None, :]
    first = (n >= SEL_RATIO * j) & (n <= SEL_RATIO * j + SEL_RATIO - 1)
    second = (n >= SEL_RATIO * j - 1) & (n <= SEL_RATIO * j + SEL_RATIO - 2)
    mt = (first.astype(np.float32) + second.astype(np.float32)) * (n < ncp - 1) * (j < nb)
    et = (np.arange(t_len)[:, None] // SEL_BLOCK == np.arange(nbp)[None, :]).astype(np.float32) * SEL_BIAS
    place = np.zeros((NSA_KV_HEADS, NSA_GROUP, LANES, NSA_GROUP * HEAD_DIM), np.float32)
    for h in range(NSA_KV_HEADS):
        for g in range(NSA_GROUP):
            place[h, g, h * HEAD_DIM + np.arange(HEAD_DIM), g * HEAD_DIM + np.arange(HEAD_DIM)] = 1.0
    return nb, jnp.asarray(mt, BF16), jnp.asarray(et, BF16), jnp.asarray(place, BF16)


def _nsa_attend(qpad, kv, kc, vc, gates, qk_g, batch, t_len):
    tq = NSA_TQ
    nq = t_len // tq
    rows = NSA_GROUP * tq
    nb, mt, et, place = _nsa_constants(t_len)
    nbp, ncp = mt.shape
    seq = lambda c: pl.BlockSpec((t_len, KV_W), lambda b, h, i: (b, c))
    bound = (HEAD_DIM ** 0.5) * jnp.max(jnp.abs(qk_g[0])) * jnp.max(jnp.abs(qk_g[2])) * 1.02 + 0.1

    def attend(fixed_max):
        return pl.pallas_call(
            functools.partial(_nsa_kernel, nb=nb, fixed_max=fixed_max),
            grid=(batch, NSA_KV_HEADS, nq),
            in_specs=[pl.BlockSpec((tq, NSA_GROUP * LANES), lambda b, h, i: (b * nq + i, h)),
                      seq(0), seq(1), seq(2), seq(3),
                      pl.BlockSpec((1, ncp, KV_W), lambda b, h, i: (b, 0, 0)),
                      pl.BlockSpec((1, ncp, KV_W), lambda b, h, i: (b, 0, 0)),
                      pl.BlockSpec((tq, LANES), lambda b, h, i: (b * nq + i, h)),
                      _const_spec((nbp, ncp)), _const_spec((t_len, nbp)),
                      pl.BlockSpec((1, NSA_GROUP, LANES, NSA_GROUP * HEAD_DIM), lambda b, h, i: (h, 0, 0, 0)),
                      pl.BlockSpec(memory_space=pltpu.SMEM)],
            out_specs=pl.BlockSpec((tq, NSA_GROUP * HEAD_DIM), lambda b, h, i: (b * nq + i, h)),
            out_shape=jax.ShapeDtypeStruct((batch * t_len, NSA_W), BF16),
            scratch_shapes=[pltpu.VMEM((nbp, tq), F32), pltpu.VMEM((rows, LANES + nbp), BF16),
                            pltpu.VMEM((rows, LANES), F32), pltpu.VMEM((rows, LANES), F32),
                            pltpu.VMEM((rows, LANES), F32)],
            compiler_params=_params("parallel", "parallel", "arbitrary"),
            name="nsa_attend",
        )(qpad, kv, kv, kv, kv, kc, vc, gates, mt, et, place, bound.reshape(1, 1))
    return lax.cond(bound <= NSA_MAX_FIXED_SHIFT, lambda: attend(True), lambda: attend(False))


def _outffn_kernel(x_ref, og_ref, on_ref, om_ref, wo_ref, g2_ref, wu_ref, wd_ref, y_ref):
    x1 = (x_ref[...] + _dot(og_ref[...], wo_ref[0:GLA_W, :])
          + _dot(on_ref[...], wo_ref[GLA_W:GLA_W + NSA_W, :])
          + _dot(om_ref[...], wo_ref[GLA_W + NSA_W:GLA_W + NSA_W + MLSTM_W, :]))
    h = x1 * lax.rsqrt(jnp.mean(x1 * x1, axis=-1, keepdims=True) + EPS) * g2_ref[...]
    hid = jnp.maximum(_dot(h.astype(BF16), wu_ref[...]), 0.0)
    y_ref[...] = x1 + _dot((hid * hid).astype(BF16), wd_ref[...])


def _out_ffn(x, og, on, om, wo, g2, wu, wd, tm):
    n = x.shape[0]
    row = lambda w: pl.BlockSpec((tm, w), lambda i: (i, 0))
    return pl.pallas_call(
        _outffn_kernel,
        grid=(n // tm,),
        in_specs=[row(D_MODEL), row(GLA_W), row(NSA_W), row(MLSTM_W),
                  _const_spec((D_MODEL, D_MODEL)), _const_spec((1, D_MODEL)),
                  _const_spec((D_MODEL, D_FF)), _const_spec((D_FF, D_MODEL))],
        out_specs=row(D_MODEL),
        out_shape=jax.ShapeDtypeStruct((n, D_MODEL), F32),
        compiler_params=_params("parallel"),
        name="out_ffn",
    )(x, og, on, om, wo, g2, wu, wd)


def _layer_weights(lp):
    return {'w_in': _pack_w_in(lp['w_in']), 'w_out': lp['w_out'].astype(BF16),
            'w_up': lp['w_up'].astype(BF16), 'w_down': lp['w_down'].astype(BF16),
            'cmp': _cmp_weights(lp)}


def _rows_on_lanes_to_state(a, n_slots):
    batch, _, n_rows = a.shape
    return jnp.transpose(a.reshape(batch, n_slots, NSA_KV_HEADS, HEAD_DIM, n_rows), (0, 4, 1, 2, 3))


def _prompt_layer(x, lp, lw, batch, t_len):
    d = HEAD_DIM
    zg, zn, zm = _in_proj(x, lp['norm1_g'].reshape(1, D_MODEL), lw['w_in'], 256)
    qpad, rows_t, win_t, kv, r01, gates = _nsa_prep(zn, lp['nsa_qk_g'], lp['nsa_gate_b'], 256,
                                                    seq=(batch, t_len))
    zero = lambda *s: jnp.zeros(s, F32)
    og, s_gla = _gla(zg, zero(batch, GLA_HEADS, d, d), lp['gla_a_w'], lp['gla_a_b'], lp['gla_norm_g'],
                     batch, t_len, t_len)
    om, c_m, n_m, m_m = _mlstm(zm, zero(batch, CONV_WIDTH - 1, MLSTM_W), zero(batch, MLSTM_HEADS, d, d),
                               zero(batch, MLSTM_HEADS, d), zero(batch, MLSTM_HEADS), lp, batch, t_len, t_len)
    kc, vc = _compress(r01, lw['cmp'], batch, t_len)
    on = _nsa_attend(qpad, kv, kc, vc, gates, lp['nsa_qk_g'], batch, t_len)
    y = _out_ffn(x, og, on, om, lw['w_out'], lp['norm2_g'].reshape(1, D_MODEL), lw['w_up'], lw['w_down'], 256)
    wlen = min(WINDOW, t_len)
    new_rows = _rows_on_lanes_to_state(rows_t, 4)
    win_state = _rows_on_lanes_to_state(win_t[:, :, t_len - wlen:], 2)
    conv_state = zm.reshape(batch, t_len, ZM_W)[:, t_len - (CONV_WIDTH - 1):, 0:MLSTM_W]
    return y, (new_rows, win_state, s_gla, c_m, n_m, m_m, conv_state)


GATHER_PAGES = 16
GROUPS_PER_PAGE = PAGE_SIZE // CMP_STRIDE


def _paged_cache_view(cache_nsa_kv):
    depth, n_pool = cache_nsa_kv.shape[0], cache_nsa_kv.shape[1]
    return jnp.transpose(cache_nsa_kv, (0, 1, 3, 4, 5, 2)).reshape(depth, n_pool, ROWS_W, PAGE_SIZE)


def _window_cache_view(cache_nsa_win):
    depth, batch, wb = cache_nsa_win.shape[0:3]
    return jnp.transpose(cache_nsa_win, (0, 1, 3, 4, 5, 2)).reshape(depth, batch, WIN_W, wb)


def _cmp_gather_kernel(pt_ref, ct_ref, w_ref, g_ref, buf, xs, sems, *, layer):
    s = pl.program_id(0)
    slot = s % 2

    def page_copies(step, dst):
        return [pltpu.make_async_copy(ct_ref.at[layer, pt_ref[step * GATHER_PAGES + p], pl.ds(0, 2 * KV_W), :],
                                      buf.at[dst, p], sems.at[dst, p]) for p in range(GATHER_PAGES)]

    @pl.when(s == 0)
    def _():
        for c in page_copies(0, 0):
            c.start()

    @pl.when(s + 1 < pl.num_programs(0))
    def _():
        for c in page_copies(s + 1, 1 - slot):
            c.start()
    for c in page_copies(s, slot):
        c.wait()

    for p in range(GATHER_PAGES):
        for sl in range(2):
            xs[sl, p * PAGE_SIZE:(p + 1) * PAGE_SIZE, :] = buf[slot, p, sl * KV_W:(sl + 1) * KV_W, :].T
    parts = []
    for sl in range(2):
        acc = jnp.zeros((GATHER_PAGES * GROUPS_PER_PAGE, 2 * KV_W), F32)
        for r in range(CMP_STRIDE):
            xr = xs[sl, pl.ds(r, GATHER_PAGES * GROUPS_PER_PAGE, stride=CMP_STRIDE), :]
            acc = acc + _dot(xr.astype(BF16), w_ref[r, sl])
        parts.append(acc)
    g_ref[0] = jnp.concatenate([parts[0][:, 0:KV_W], parts[1][:, 0:KV_W],
                                parts[0][:, KV_W:2 * KV_W], parts[1][:, KV_W:2 * KV_W]], axis=1)


def _cmp_gather(ct, layer, pt_flat, w_ab, batch, n_pages):
    steps = n_pages // GATHER_PAGES
    rows = GATHER_PAGES * GROUPS_PER_PAGE
    w4 = w_ab.reshape(CMP_STRIDE, 2, KV_W, 2, 2, KV_W)
    w_rs = jnp.stack([w4[:, s, :, :, s, :] for s in range(2)], axis=1).reshape(CMP_STRIDE, 2, KV_W, 2 * KV_W)
    grid_spec = pltpu.PrefetchScalarGridSpec(
        num_scalar_prefetch=1,
        grid=(batch * steps,),
        in_specs=[pl.BlockSpec(memory_space=pl.ANY),
                  pl.BlockSpec((CMP_STRIDE, 2, KV_W, 2 * KV_W), lambda s, pt: (0, 0, 0, 0))],
        out_specs=pl.BlockSpec((1, rows, CMP_G), lambda s, pt: (s // steps, s % steps, 0)),
        scratch_shapes=[pltpu.VMEM((2, GATHER_PAGES, 2 * KV_W, PAGE_SIZE), F32),
                        pltpu.VMEM((2, GATHER_PAGES * PAGE_SIZE, KV_W), F32),
                        pltpu.SemaphoreType.DMA((2, GATHER_PAGES))],
    )
    return pl.pallas_call(
        functools.partial(_cmp_gather_kernel, layer=layer),
        grid_spec=grid_spec,
        out_shape=jax.ShapeDtypeStruct((batch, n_pages * GROUPS_PER_PAGE, CMP_G), F32),
        compiler_params=_params("arbitrary"),
        name="nsa_cmp_gather",
    )(pt_flat, ct, w_rs)


def _to_col(row, n):
    eye = (lax.broadcasted_iota(jnp.int32, (n, n), 0) == lax.broadcasted_iota(jnp.int32, (n, n), 1))
    return jnp.sum(jnp.where(eye, jnp.broadcast_to(row, (n, n)), 0.0), axis=1, keepdims=True)


def _dec_cmp_kernel(g_ref, q_ref, wc_ref, wnew_ref, wab_ref, pe_ref, b1_ref, w2_ref, b2_ref, g1_ref,
                    ones_ref, m_ref, oc_ref, ow_ref, idx_ref, wout_ref, *, n_groups, nb, cur):
    nh, grp = NSA_KV_HEADS, NSA_GROUP
    q8 = q_ref[0]
    kc, vc = _cmp_finish(g_ref[0], n_groups, wab_ref, pe_ref, b1_ref, w2_ref, b2_ref, g1_ref, ones_ref)
    sc = _dot_nt(q8, kc.astype(BF16))
    vis = lax.broadcasted_iota(jnp.int32, sc.shape, 1) < n_groups - 1
    s_m = jnp.where(vis, sc, NEG)
    e = jnp.where(vis, jnp.exp(s_m - jnp.max(s_m, axis=-1, keepdims=True)), 0.0)
    p_c = e / jnp.maximum(jnp.sum(e, axis=-1, keepdims=True), 1e-30)
    oc_ref[0] = _dot(p_c.astype(BF16), vc.astype(BF16))

    nsp = m_ref.shape[1]
    pg = jnp.concatenate([jnp.sum(p_c[h * grp:(h + 1) * grp], axis=0, keepdims=True) for h in range(nh)]
                         + [jnp.zeros((8 - nh, n_groups), F32)], axis=0)
    h1 = pg.astype(BF16)
    r1 = pg - h1.astype(F32)
    h2 = r1.astype(BF16)
    h3 = (r1 - h2.astype(F32)).astype(BF16)
    m = m_ref[...]
    p_sel = _dot(h1, m) + _dot(h2, m) + _dot(h3, m)
    blk = lax.broadcasted_iota(jnp.int32, (1, nsp), 1)
    forced = (blk == 0) | (blk == cur) | (blk == cur - 1)
    ii = lax.broadcasted_iota(jnp.int32, (nsp, nsp), 0)
    jj = lax.broadcasted_iota(jnp.int32, (nsp, nsp), 1)
    slot = lax.broadcasted_iota(jnp.int32, (SEL_TOPK, nsp), 0).astype(F32)
    blk_f = lax.broadcasted_iota(jnp.int32, (SEL_TOPK, nsp), 1).astype(F32)
    for h in range(nh):
        score = jnp.where(blk > cur, -1.0, jnp.where(forced, FORCE_SCORE, p_sel[h:h + 1, :]))
        score = jnp.where(blk >= nb, -2.0, score)
        s_col = _to_col(score, nsp)
        ge = jnp.where(s_col >= score, 1.0, 0.0)
        gt = jnp.where(s_col > score, 1.0, 0.0)
        rank = jnp.sum(jnp.where(jj > ii, ge, gt), axis=0, keepdims=True)
        sel = (rank < SEL_TOPK).astype(F32)
        before = jnp.sum(jnp.where(ii < jj, _to_col(sel, nsp), 0.0), axis=0, keepdims=True)
        onehot = jnp.where((before == slot) & (sel > 0.5), 1.0, 0.0)
        idx = jnp.sum(onehot * blk_f, axis=1, keepdims=True)
        idx_ref[0, h] = jnp.broadcast_to(idx, (SEL_TOPK, LANES)).astype(jnp.int32)

    wt = wc_ref[0, 0]
    wb = wt.shape[1]
    wnew = wnew_ref[0]
    s = _dot(q8, wt[0:KV_W].astype(BF16))
    s = jnp.where(lax.broadcasted_iota(jnp.int32, s.shape, 1) > wb - WINDOW, s, NEG)
    qf = q8.astype(F32)
    s_new = jnp.sum(qf * wnew[:, 0:KV_W].astype(BF16).astype(F32), axis=-1, keepdims=True)
    mx = jnp.maximum(jnp.max(s, axis=-1, keepdims=True), s_new)
    e = jnp.exp(s - mx)
    e_new = jnp.exp(s_new - mx)
    num = _dot_nt(e.astype(BF16), wt[KV_W:2 * KV_W].astype(BF16)) + e_new * wnew[:, KV_W:2 * KV_W]
    ow_ref[0] = num / (jnp.sum(e, axis=-1, keepdims=True) + e_new)
    lane = lax.broadcasted_iota(jnp.int32, wt.shape, 1)
    wout_ref[0] = jnp.where(lane == wb - 1, _to_col(wnew, 2 * KV_W), pltpu.roll(wt, wb - 1, 1))


def _dec_constants(past):
    n_groups = past // CMP_STRIDE
    nb = past // SEL_BLOCK + 1
    nsp = -(-nb // LANES) * LANES
    j = np.arange(nsp)[None, :]
    n = np.arange(n_groups)[:, None]
    first = (n >= SEL_RATIO * j) & (n <= SEL_RATIO * j + SEL_RATIO - 1)
    second = (n >= SEL_RATIO * j - 1) & (n <= SEL_RATIO * j + SEL_RATIO - 2)
    m = (first.astype(np.float32) + second.astype(np.float32)) * (n < n_groups - 1) * (j < nb)
    return n_groups, nb, jnp.asarray(m, BF16)


def _dec_cmp_attn(gsum, q8, wt, layer, win_new, cw, batch, past):
    n_groups, nb, m = _dec_constants(past)
    nsp = m.shape[1]
    wb = wt.shape[3]
    assert wb == WINDOW
    per_b = lambda shape: pl.BlockSpec((1,) + shape, lambda b: (b,) + (0,) * len(shape))
    kern = functools.partial(_dec_cmp_kernel, n_groups=n_groups, nb=nb, cur=past // SEL_BLOCK)
    return pl.pallas_call(
        kern,
        grid=(batch,),
        in_specs=[per_b((n_groups, CMP_G)), per_b((NSA_HEADS, LANES)),
                  pl.BlockSpec((1, 1, 2 * KV_W, wb), lambda b: (layer, b, 0, 0)),
                  per_b((1, 2 * KV_W))] + _cmp_specs() + [_const_spec((n_groups, nsp))],
        out_specs=[per_b((NSA_HEADS, LANES)), per_b((NSA_HEADS, LANES)),
                   per_b((NSA_KV_HEADS, SEL_TOPK, LANES)), per_b((2 * KV_W, wb))],
        out_shape=[jax.ShapeDtypeStruct((batch, NSA_HEADS, LANES), F32),
                   jax.ShapeDtypeStruct((batch, NSA_HEADS, LANES), F32),
                   jax.ShapeDtypeStruct((batch, NSA_KV_HEADS, SEL_TOPK, LANES), jnp.int32),
                   jax.ShapeDtypeStruct((batch, 2 * KV_W, wb), F32)],
        compiler_params=_params("parallel"),
        name="nsa_decode_cmp",
    )(gsum, q8, wt, win_new, *cw, m)


def _dec_sel_kernel(pt_ref, idx_ref, q_ref, new_ref, oc_ref, ow_ref, gate_ref, ct_ref, out_ref,
                    kv_buf, sems, *, layer, n_pages, n_cache_blocks):
    b = pl.program_id(0)
    buf_slot = b % 2
    nh, nq, topk = NSA_KV_HEADS, NSA_HEADS, SEL_TOPK
    half = PAGE_SIZE // SEL_BLOCK
    nblk = nh * topk

    def page_copies(seq, dst):
        copies = []
        for j in range(nblk):
            blk = jnp.minimum(idx_ref[seq * nblk + j], n_cache_blocks - 1)
            page = pt_ref[seq * n_pages + blk // half]
            copies.append(pltpu.make_async_copy(ct_ref.at[layer, page, pl.ds(2 * KV_W, 2 * KV_W), :],
                                                kv_buf.at[dst, j], sems.at[dst, j]))
        return copies

    @pl.when(b == 0)
    def _():
        for c in page_copies(0, 0):
            c.start()

    @pl.when(b + 1 < pl.num_programs(0))
    def _():
        for c in page_copies(b + 1, 1 - buf_slot):
            c.start()
    for c in page_copies(b, buf_slot):
        c.wait()

    q8 = q_ref[0]
    new = new_ref[0]
    s_new = jnp.sum(q8 * new[:, 2 * KV_W:3 * KV_W].astype(BF16).astype(F32), axis=-1, keepdims=True)
    v_new = new[:, 3 * KV_W:4 * KV_W]
    qb = q8.astype(BF16)
    page_half = lax.broadcasted_iota(jnp.int32, (1, PAGE_SIZE), 1) // SEL_BLOCK
    o_heads = []
    for h in range(nh):
        s_parts = []
        m = s_new
        for k in range(topk):
            j = h * topk + k
            blk = idx_ref[b * nblk + j]
            s_k = _dot(qb, kv_buf[buf_slot, j, 0:KV_W, :].astype(BF16))
            keep = jnp.where(blk < n_cache_blocks, 0.0, NEG)
            s_k = s_k + jnp.where(page_half == blk % half, keep, NEG)
            s_parts.append(s_k)
            m = jnp.maximum(m, jnp.max(s_k, axis=-1, keepdims=True))
        e_new = jnp.exp(s_new - m)
        den = e_new
        num = e_new * v_new
        for k in range(topk):
            e = jnp.exp(s_parts[k] - m)
            den = den + jnp.sum(e, axis=-1, keepdims=True)
            num = num + _dot_nt(e.astype(BF16), kv_buf[buf_slot, h * topk + k, KV_W:2 * KV_W, :].astype(BF16))
        o_heads.append(num / den)
    row = lax.broadcasted_iota(jnp.int32, (nq, KV_W), 0)
    o_s = jnp.where(row < NSA_GROUP, o_heads[0], o_heads[1])
    gt = gate_ref[0]
    out_ref[0] = gt[:, 0:1] * oc_ref[0] + gt[:, 1:2] * o_s + gt[:, 2:3] * ow_ref[0]


def _dec_sel_attn(ct, layer, pt_flat, idx_flat, q8, rows_new, o_c, o_w, gates, batch, n_pages):
    half = PAGE_SIZE // SEL_BLOCK
    nh, grp, nq = NSA_KV_HEADS, NSA_GROUP, NSA_HEADS
    g3 = gates.reshape(batch, nh, LANES)[:, :, 0:3 * grp].reshape(batch, nq, 3)
    g8 = jnp.concatenate([g3, jnp.zeros((batch, nq, LANES - 3), F32)], axis=-1)
    new8 = jnp.broadcast_to(rows_new.reshape(batch, 1, ROWS_W), (batch, nq, ROWS_W))

    per_b = lambda w: pl.BlockSpec((1, nq, w), lambda b, pt, idx: (b, 0, 0))
    grid_spec = pltpu.PrefetchScalarGridSpec(
        num_scalar_prefetch=2,
        grid=(batch,),
        in_specs=[per_b(LANES), per_b(ROWS_W), per_b(LANES), per_b(LANES), per_b(LANES),
                  pl.BlockSpec(memory_space=pl.ANY)],
        out_specs=per_b(LANES),
        scratch_shapes=[pltpu.VMEM((2, nh * SEL_TOPK, 2 * KV_W, PAGE_SIZE), F32),
                        pltpu.SemaphoreType.DMA((2, nh * SEL_TOPK))],
    )
    kern = functools.partial(_dec_sel_kernel, layer=layer, n_pages=n_pages, n_cache_blocks=n_pages * half)
    return pl.pallas_call(
        kern,
        grid_spec=grid_spec,
        out_shape=jax.ShapeDtypeStruct((batch, nq, LANES), F32),
        compiler_params=_params("arbitrary"),
        name="nsa_decode_sel",
    )(pt_flat, idx_flat, q8.astype(F32), new8, o_c, o_w, g8, ct)


def _sample_layer(x, lp, lw, layer, ct, wt, gla_s0, c0, n0, m0, conv0, page_table):
    batch, n_pages = page_table.shape
    past = n_pages * PAGE_SIZE
    d = HEAD_DIM
    c = SCAN_CHUNK
    zg, zn, zm = _in_proj(x, lp['norm1_g'].reshape(1, D_MODEL), lw['w_in'], batch)
    qpad, rows, win, _, _, gates = _nsa_prep(zn, lp['nsa_qk_g'], lp['nsa_gate_b'], batch)
    pad = lambda z: jnp.pad(z[:, None, :], ((0, 0), (0, c - 1), (0, 0))).reshape(batch * c, z.shape[-1])
    first = lambda o: o.reshape(batch, c, o.shape[-1])[:, 0]
    og, s_gla = _gla(pad(zg), gla_s0, lp['gla_a_w'], lp['gla_a_b'], lp['gla_norm_g'], batch, c, 1)
    om, c_m, n_m, m_m = _mlstm(pad(zm), conv0, c0, n0, m0, lp, batch, c, 1)
    pt_flat = page_table.reshape(-1)
    gsum = _cmp_gather(ct, layer, pt_flat, lw['cmp'][0], batch, n_pages)
    q8 = qpad.reshape(batch, NSA_HEADS, LANES)
    o_c, o_w, idx, win_t = _dec_cmp_attn(gsum, q8, wt, layer, win.reshape(batch, 1, 2 * KV_W), lw['cmp'],
                                         batch, past)
    o_n = _dec_sel_attn(ct, layer, pt_flat, idx[:, :, :, 0].reshape(-1), q8, rows, o_c, o_w, gates,
                        batch, n_pages)
    on = jnp.stack([o_n[:, h * NSA_GROUP:(h + 1) * NSA_GROUP, h * d:(h + 1) * d]
                    for h in range(NSA_KV_HEADS)], axis=1)
    on = on.reshape(batch, NSA_W).astype(BF16)
    y = _out_ffn(x, first(og), on, first(om), lw['w_out'], lp['norm2_g'].reshape(1, D_MODEL),
                 lw['w_up'], lw['w_down'], batch)
    new_rows = rows.reshape(batch, 1, 4, NSA_KV_HEADS, d)
    win_state = _rows_on_lanes_to_state(win_t, 2)
    conv_state = jnp.concatenate([conv0.astype(F32), zm[:, None, 0:MLSTM_W]], axis=1)[:, 1:]
    return y, (new_rows, win_state, s_gla, c_m, n_m, m_m, conv_state)


def kernel(x_prompt, x_sample, cache_nsa_kv, cache_nsa_win, state_gla, state_mlstm_c, state_mlstm_n,
           state_mlstm_m, state_mlstm_conv, page_table, norm1_g, w_in, gla_a_w, gla_a_b, gla_norm_g,
           nsa_qk_g, nsa_gate_b, cmp_pe, cmp_w1, cmp_b1, cmp_w2, cmp_b2, ml_conv_w, ml_conv_b, ml_wq, ml_wk,
           ml_gate_b, ml_norm_g, ml_skip, w_out, norm2_g, w_up, w_down):
    bp, t_len, _ = x_prompt.shape
    n_dec = x_sample.shape[0]
    depth = w_in.shape[0]
    x_p = x_prompt.reshape(bp * t_len, D_MODEL)
    x_s = x_sample.reshape(n_dec, D_MODEL)
    states_p, states_s = [], []
    ct = _paged_cache_view(cache_nsa_kv)
    wt = _window_cache_view(cache_nsa_win)
    for l in range(depth):
        lp = {'norm1_g': norm1_g[l], 'w_in': w_in[l], 'gla_a_w': gla_a_w[l], 'gla_a_b': gla_a_b[l],
              'gla_norm_g': gla_norm_g[l], 'nsa_qk_g': nsa_qk_g[l], 'nsa_gate_b': nsa_gate_b[l],
              'cmp_pe': cmp_pe[l], 'cmp_w1': cmp_w1[l], 'cmp_b1': cmp_b1[l], 'cmp_w2': cmp_w2[l],
              'cmp_b2': cmp_b2[l], 'ml_conv_w': ml_conv_w[l], 'ml_conv_b': ml_conv_b[l], 'ml_wq': ml_wq[l],
              'ml_wk': ml_wk[l], 'ml_gate_b': ml_gate_b[l], 'ml_norm_g': ml_norm_g[l], 'ml_skip': ml_skip[l],
              'w_out': w_out[l], 'norm2_g': norm2_g[l], 'w_up': w_up[l], 'w_down': w_down[l]}
        lw = _layer_weights(lp)
        x_p, st_p = _prompt_layer(x_p, lp, lw, bp, t_len)
        x_s, st_s = _sample_layer(x_s, lp, lw, l, ct, wt, state_gla[l],
                                  state_mlstm_c[l], state_mlstm_n[l], state_mlstm_m[l], state_mlstm_conv[l],
                                  page_table)
        states_p.append(st_p)
        states_s.append(st_s)
    outs = [x_p.reshape(bp, t_len, D_MODEL), x_s.reshape(n_dec, 1, D_MODEL)]
    for i in range(7):
        outs.append(jnp.stack([s[i] for s in states_p]))
        outs.append(jnp.stack([s[i] for s in states_s]))
    return tuple(outs)
```

```python
import functools

import numpy as np
import jax
import jax.numpy as jnp
from jax import lax
from jax.experimental import pallas as pl
from jax.experimental.pallas import tpu as pltpu

F32 = jnp.float32
BF16 = jnp.bfloat16

D_MODEL = 1024
HEAD_DIM = 64
PAGE_SIZE = 128
GLA_HEADS = 4
GLA_LOWRANK = 16
GLA_TAU = 16.0
NSA_HEADS = 8
NSA_KV_HEADS = 2
NSA_GROUP = NSA_HEADS // NSA_KV_HEADS
CMP_BLOCK = 32
CMP_STRIDE = 16
CMP_HIDDEN = 64
SEL_BLOCK = 64
SEL_RATIO = SEL_BLOCK // CMP_STRIDE
SEL_TOPK = 16
WINDOW = 512
FORCE_SCORE = 1.0e4
MLSTM_HEADS = 4
CONV_WIDTH = 4
D_FF = 4 * D_MODEL
EPS = 1e-6

GLA_W = GLA_HEADS * HEAD_DIM
NSA_W = NSA_HEADS * HEAD_DIM
KV_W = NSA_KV_HEADS * HEAD_DIM
MLSTM_W = MLSTM_HEADS * HEAD_DIM

LANES = 128
ZG_W = 4 * GLA_W + LANES
GATE_W = NSA_KV_HEADS * LANES
ZN_W = NSA_W + 6 * KV_W + GATE_W
ZM_W = 3 * MLSTM_W + LANES
ROWS_W = 4 * KV_W
WIN_W = 2 * KV_W
QPAD_W = NSA_HEADS * LANES

SCAN_CHUNK = 128
GLA_SUB = 32
NSA_TQ = 128
NSA_TK = 256
NSA_TK_FIXED = 512
NSA_MAX_FIXED_SHIFT = 40.0
NEG = -1.0e30
SEL_BIAS = 29952.0

VMEM_LIMIT = 56 * 1024 * 1024


def _dot(a, b):
    return jnp.dot(a, b, preferred_element_type=F32)


def _dot_hi(a, b):
    return jnp.dot(a, b, preferred_element_type=F32, precision=lax.Precision.HIGHEST)


def _dot_nt(a, b):
    return lax.dot_general(a, b, (((1,), (1,)), ((), ())), preferred_element_type=F32)


def _dot_nt_hi(a, b):
    return lax.dot_general(a, b, (((1,), (1,)), ((), ())), preferred_element_type=F32,
                           precision=lax.Precision.HIGHEST)


def _dot_tn_hi(a, b):
    return lax.dot_general(a, b, (((0,), (0,)), ((), ())), preferred_element_type=F32,
                           precision=lax.Precision.HIGHEST)


def _sigmoid(x):
    return 1.0 / (1.0 + jnp.exp(-x))


def _log_sigmoid(x):
    return jnp.minimum(x, 0.0) - jnp.log(1.0 + jnp.exp(-jnp.abs(x)))


def _group_sum(x, ones_bd):
    hi = x.astype(BF16)
    lo = (x - hi.astype(F32)).astype(BF16)
    return _dot(hi, ones_bd) + _dot(lo, ones_bd)


def _group_mean_sq(x, ones_bd):
    return _group_sum(x * x, ones_bd) * (1.0 / HEAD_DIM)


def _params(*sem):
    return pltpu.CompilerParams(dimension_semantics=sem, vmem_limit_bytes=VMEM_LIMIT)


def _const_spec(shape):
    nd = len(shape)
    return pl.BlockSpec(shape, lambda *_: (0,) * nd)


def _inproj_kernel(x_ref, g_ref, w_ref, zg_ref, zn_ref, zm_ref):
    x = x_ref[...]
    h = x * lax.rsqrt(jnp.mean(x * x, axis=-1, keepdims=True) + EPS) * g_ref[...]
    hb = h.astype(BF16)
    zg_ref[...] = _dot(hb, w_ref[:, 0:ZG_W])
    zn_ref[...] = _dot(hb, w_ref[:, ZG_W:ZG_W + ZN_W])
    zm_ref[...] = _dot(hb, w_ref[:, ZG_W + ZN_W:ZG_W + ZN_W + ZM_W])


def _in_proj(x, g, w, tm):
    n = x.shape[0]
    zw = ZG_W + ZN_W + ZM_W
    return pl.pallas_call(
        _inproj_kernel,
        grid=(n // tm,),
        in_specs=[pl.BlockSpec((tm, D_MODEL), lambda i: (i, 0)),
                  _const_spec((1, D_MODEL)),
                  _const_spec((D_MODEL, zw))],
        out_specs=[pl.BlockSpec((tm, ZG_W), lambda i: (i, 0)),
                   pl.BlockSpec((tm, ZN_W), lambda i: (i, 0)),
                   pl.BlockSpec((tm, ZM_W), lambda i: (i, 0))],
        out_shape=[jax.ShapeDtypeStruct((n, ZG_W), F32),
                   jax.ShapeDtypeStruct((n, ZN_W), F32),
                   jax.ShapeDtypeStruct((n, ZM_W), F32)],
        compiler_params=_params("parallel"),
        name="in_proj",
    )(x, g, w)


def _pack_w_in(w_in):
    def cols(a, b):
        return w_in[:, a:b]

    def zeros(n):
        return jnp.zeros((D_MODEL, n), w_in.dtype)
    o_nsa = 4 * GLA_W + GLA_LOWRANK
    o_ng = o_nsa + NSA_W + 6 * KV_W
    o_ml = o_ng + 3 * NSA_HEADS
    o_mi = o_ml + 2 * MLSTM_W
    o_mo = o_mi + 2 * MLSTM_HEADS
    n_gate = 3 * NSA_GROUP
    parts = [cols(0, o_nsa), zeros(LANES - GLA_LOWRANK),
             cols(o_nsa, o_ng),
             cols(o_ng, o_ng + n_gate), zeros(LANES - n_gate),
             cols(o_ng + n_gate, o_ml), zeros(LANES - n_gate),
             cols(o_ml, o_mi), cols(o_mo, o_mo + MLSTM_W), cols(o_mi, o_mo),
             zeros(LANES - 2 * MLSTM_HEADS)]
    return jnp.concatenate(parts, axis=1).astype(BF16)


def _nsa_prep_kernel(zn_ref, gq_ref, gr_ref, gw_ref, gb_ref, ones_ref, place_ref,
                     qpad_ref, rows_ref, win_ref, kv_ref, r01_ref, gate_ref, *, transposed):
    ones_bd = ones_ref[...]
    q = zn_ref[:, 0:NSA_W]
    qn = q * lax.rsqrt(_group_mean_sq(q, ones_bd) + EPS) * gq_ref[...]
    qpad_ref[...] = _dot(qn.astype(BF16), place_ref[...]).astype(BF16)

    r = zn_ref[:, NSA_W:NSA_W + ROWS_W]
    col = lax.broadcasted_iota(jnp.int32, r.shape, 1)
    rn = r * lax.rsqrt(_group_mean_sq(r, ones_bd) + EPS) * gr_ref[...]
    rows = jnp.where((col >= 2 * KV_W) & (col < 3 * KV_W), rn, r)
    if transposed:
        rows_ref[0] = rows.T
    else:
        rows_ref[...] = rows
    r01_ref[...] = rows[:, 0:2 * KV_W].astype(BF16)

    w = zn_ref[:, NSA_W + ROWS_W:NSA_W + ROWS_W + WIN_W]
    colw = lax.broadcasted_iota(jnp.int32, w.shape, 1)
    wn = w * lax.rsqrt(_group_mean_sq(w, ones_bd[0:WIN_W, 0:WIN_W]) + EPS) * gw_ref[...]
    win = jnp.where(colw < KV_W, wn, w)
    if transposed:
        win_ref[0] = win.T
    else:
        win_ref[...] = win
    kv_ref[:, 0:2 * KV_W] = rows[:, 2 * KV_W:4 * KV_W].astype(BF16)
    kv_ref[:, 2 * KV_W:4 * KV_W] = win.astype(BF16)

    gate_ref[...] = _sigmoid(zn_ref[:, NSA_W + ROWS_W + WIN_W:ZN_W] + gb_ref[...])


def _nsa_prep(zn, qk_g, gate_b, tm, seq=None):
    n = zn.shape[0]
    scale = HEAD_DIM ** -0.5
    gq = (jnp.tile(qk_g[0], NSA_HEADS) * scale).reshape(1, NSA_W)
    gr = jnp.tile(qk_g[2], ROWS_W // HEAD_DIM).reshape(1, ROWS_W)
    gw = jnp.tile(qk_g[3], WIN_W // HEAD_DIM).reshape(1, WIN_W)
    n_gate = 3 * NSA_GROUP
    gpad = jnp.zeros((LANES - n_gate,), F32)
    gb = jnp.concatenate([gate_b[0:n_gate], gpad, gate_b[n_gate:], gpad]).reshape(1, GATE_W)
    grp = np.arange(NSA_W) // HEAD_DIM
    ones_bd = jnp.asarray(grp[:, None] == grp[None, :], BF16)
    src = np.arange(NSA_W)
    head, d = src // HEAD_DIM, src % HEAD_DIM
    dst = head * LANES + (head // NSA_GROUP) * HEAD_DIM + d
    place = np.zeros((NSA_W, QPAD_W), np.float32)
    place[src, dst] = 1.0
    place = jnp.asarray(place, BF16)
    row = lambda w: pl.BlockSpec((tm, w), lambda i: (i, 0))
    if seq is None:
        state_spec = row
        state_shape = lambda w: jax.ShapeDtypeStruct((n, w), F32)
    else:
        batch, t_len = seq
        nblk = t_len // tm
        state_spec = lambda w: pl.BlockSpec((1, w, tm), lambda i: (i // nblk, 0, i % nblk))
        state_shape = lambda w: jax.ShapeDtypeStruct((batch, w, t_len), F32)
    return pl.pallas_call(
        functools.partial(_nsa_prep_kernel, transposed=seq is not None),
        grid=(n // tm,),
        in_specs=[row(ZN_W), _const_spec((1, NSA_W)), _const_spec((1, ROWS_W)),
                  _const_spec((1, WIN_W)), _const_spec((1, GATE_W)),
                  _const_spec((NSA_W, NSA_W)), _const_spec((NSA_W, QPAD_W))],
        out_specs=[row(QPAD_W), state_spec(ROWS_W), state_spec(WIN_W), row(4 * KV_W), row(2 * KV_W),
                   row(GATE_W)],
        out_shape=[jax.ShapeDtypeStruct((n, QPAD_W), BF16),
                   state_shape(ROWS_W),
                   state_shape(WIN_W),
                   jax.ShapeDtypeStruct((n, 4 * KV_W), BF16),
                   jax.ShapeDtypeStruct((n, 2 * KV_W), BF16),
                   jax.ShapeDtypeStruct((n, GATE_W), F32)],
        compiler_params=_params("parallel"),
        name="nsa_prep",
    )(zn, gq, gr, gw, gb, ones_bd, place)


def _gla_kernel(zg_ref, s0_ref, aw_ref, ab_ref, ng_ref, tri_ref, ones_ref, og_ref, sout_ref, s_scr,
                *, t_valid, t_pad):
    i = pl.program_id(1)
    c = SCAN_CHUNK
    d = HEAD_DIM

    @pl.when(i == 0)
    def _():
        s_scr[...] = s0_ref[0]

    q = zg_ref[:, 0:GLA_W] * (d ** -0.5)
    k = zg_ref[:, GLA_W:2 * GLA_W]
    v = zg_ref[:, 2 * GLA_W:3 * GLA_W]
    r = zg_ref[:, 3 * GLA_W:4 * GLA_W]
    ga = zg_ref[:, 4 * GLA_W:ZG_W]
    g = _log_sigmoid(_dot(ga.astype(BF16), aw_ref[...]) + ab_ref[...]) * (1.0 / GLA_TAU)
    if t_valid < t_pad:
        valid = (i * c + lax.broadcasted_iota(jnp.int32, (c, 1), 0)) < t_valid
        g = jnp.where(valid, g, 0.0)
        k = jnp.where(valid, k, 0.0)
    bcum = _dot_hi(tri_ref[...], g)
    b_end = bcum[c - 1:c, :]
    nh, w, sub = GLA_HEADS, GLA_W, GLA_SUB
    lane_head = lax.broadcasted_iota(jnp.int32, (1, w), 1) // d
    s_bd = s_scr[...]
    o = _dot((q * jnp.exp(bcum)).astype(BF16), s_bd.astype(BF16))
    vb = v.astype(BF16)
    o_sub = []
    for j in range(c // sub):
        lo, hi = j * sub, (j + 1) * sub
        base = bcum[lo - 1:lo, :] if j > 0 else jnp.zeros((1, w), F32)
        qt = q[lo:hi] * jnp.exp(bcum[lo:hi] - base)
        kt = (k[0:hi] * jnp.exp(base - bcum[0:hi])).astype(BF16)
        qs = jnp.concatenate([jnp.where(lane_head == h, qt, 0.0) for h in range(nh)], axis=0)
        sc = _dot_nt(qs.astype(BF16), kt)
        t_row = lo + (lax.broadcasted_iota(jnp.int32, (nh * sub, hi), 0) % sub)
        sc = jnp.where(lax.broadcasted_iota(jnp.int32, (nh * sub, hi), 1) <= t_row, sc, 0.0)
        ov = _dot(sc.astype(BF16), vb[0:hi])
        o_j = jnp.where(lane_head == 0, ov[0:sub], 0.0)
        for h in range(1, nh):
            o_j = jnp.where(lane_head == h, ov[h * sub:(h + 1) * sub], o_j)
        o_sub.append(o_j)
    o = o + jnp.concatenate(o_sub, axis=0)
    last = (lax.broadcasted_iota(jnp.int32, (c, LANES), 0) == c - 1).astype(F32)
    decay_col = jnp.exp(_dot_tn_hi(bcum, last))
    k_hat = k * jnp.exp(b_end - bcum)
    same_head = (lax.broadcasted_iota(jnp.int32, (w, w), 0) // d) == (lax.broadcasted_iota(jnp.int32, (w, w), 1) // d)
    s_scr[...] = (jnp.concatenate([decay_col] * (w // LANES), axis=1) * s_bd
                  + jnp.where(same_head, _dot_tn_hi(k_hat, v), 0.0))
    on = o * lax.rsqrt(_group_mean_sq(o, ones_ref[...]) + EPS) * ng_ref[...]
    og_ref[...] = (on * (r * _sigmoid(r))).astype(BF16)

    @pl.when(i == pl.num_programs(1) - 1)
    def _():
        sout_ref[0] = s_scr[...]


def _heads_to_block_diag(s):
    batch, nh, d, _ = s.shape
    return jnp.einsum('bhde,hg->bhdge', s, jnp.eye(nh, dtype=s.dtype)).reshape(batch, nh * d, nh * d)


def _block_diag_to_heads(s, nh):
    batch, w, _ = s.shape
    d = w // nh
    s5 = s.reshape(batch, nh, d, nh, d)
    return jnp.stack([s5[:, h, :, h, :] for h in range(nh)], axis=1)


def _gla(zg, s0, a_w, a_b, norm_g, batch, t_pad, t_valid):
    c = SCAN_CHUNK
    nblk = t_pad // c
    aw = jnp.concatenate([a_w, jnp.zeros((LANES - GLA_LOWRANK, GLA_W), F32)], axis=0).astype(BF16)
    tri = jnp.asarray(np.tril(np.ones((c, c), np.float32)))
    grp = np.arange(GLA_W) // HEAD_DIM
    ones_bd = jnp.asarray(grp[:, None] == grp[None, :], BF16)
    kern = functools.partial(_gla_kernel, t_valid=t_valid, t_pad=t_pad)
    state = pl.BlockSpec((1, GLA_W, GLA_W), lambda b, i: (b, 0, 0))
    og, s_f = pl.pallas_call(
        kern,
        grid=(batch, nblk),
        in_specs=[pl.BlockSpec((c, ZG_W), lambda b, i: (b * nblk + i, 0)), state,
                  _const_spec((LANES, GLA_W)), _const_spec((1, GLA_W)), _const_spec((1, GLA_W)),
                  _const_spec((c, c)), _const_spec((GLA_W, GLA_W))],
        out_specs=[pl.BlockSpec((c, GLA_W), lambda b, i: (b * nblk + i, 0)), state],
        out_shape=[jax.ShapeDtypeStruct((batch * t_pad, GLA_W), BF16),
                   jax.ShapeDtypeStruct((batch, GLA_W, GLA_W), F32)],
        scratch_shapes=[pltpu.VMEM((GLA_W, GLA_W), F32)],
        compiler_params=_params("parallel", "arbitrary"),
        name="gla_scan",
    )(zg, _heads_to_block_diag(s0), aw, a_b.reshape(1, GLA_W), jnp.tile(norm_g, GLA_HEADS).reshape(1, GLA_W),
      tri, ones_bd)
    return og, _block_diag_to_heads(s_f, GLA_HEADS)


def _mlstm_kernel(zm_ref, conv0_ref, c0_ref, n0_ref, m0_ref, cw_ref, cb_ref, wq_ref, wk_ref, gb_ref,
                  ng_ref, skip_ref, tri_ref, ones_ref, om_ref, cout_ref, nout_ref, mout_ref,
                  c_scr, n_scr, m_scr, ext_scr, *, t_valid, t_pad):
    i = pl.program_id(1)
    c = SCAN_CHUNK
    d = HEAD_DIM
    nh = MLSTM_HEADS

    @pl.when(i == 0)
    def _():
        c_scr[...] = c0_ref[0]
        n_scr[...] = n0_ref[0]
        m_scr[...] = m0_ref[0]
        ext_scr[0:8, :] = conv0_ref[0]

    mu = zm_ref[:, 0:MLSTM_W]
    ext_scr[8:8 + c, :] = mu
    u_conv = cb_ref[...] + mu * cw_ref[CONV_WIDTH - 1:CONV_WIDTH, :]
    for j in range(1, CONV_WIDTH):
        u_conv = u_conv + ext_scr[8 - j:8 - j + c, :] * cw_ref[CONV_WIDTH - 1 - j:CONV_WIDTH - j, :]
    ext_scr[0:8, :] = mu[c - 8:c, :]
    u_act = u_conv * _sigmoid(u_conv)
    ub = u_act.astype(BF16)
    q = _dot(ub, wq_ref[...])
    k = _dot(ub, wk_ref[...]) * (d ** -0.5)
    v = zm_ref[:, MLSTM_W:2 * MLSTM_W]
    og = zm_ref[:, 2 * MLSTM_W:3 * MLSTM_W]
    gz = zm_ref[:, 3 * MLSTM_W:ZM_W] + gb_ref[...]
    lane = lax.broadcasted_iota(jnp.int32, (c, LANES), 1)
    x = jnp.where(lane < nh, gz, _log_sigmoid(gz))
    if t_valid < t_pad:
        valid = (i * c + lax.broadcasted_iota(jnp.int32, (c, 1), 0)) < t_valid
        x = jnp.where(valid, x, jnp.where(lane < nh, NEG, 0.0))
    fc = _dot_hi(tri_ref[...], x)
    x = jnp.where(lane < nh, x, fc)
    sel = (lax.broadcasted_iota(jnp.int32, (8, LANES), 0)
           == lax.broadcasted_iota(jnp.int32, (8, LANES), 1)).astype(F32)
    xt = _dot_nt_hi(sel, x)
    row_i = lax.broadcasted_iota(jnp.int32, (c, c), 0)
    col_i = lax.broadcasted_iota(jnp.int32, (c, c), 1)
    w = MLSTM_W
    lane_head = lax.broadcasted_iota(jnp.int32, (1, w), 1) // d
    c_bd = c_scr[...]
    n_all = n_scr[0:1, :]
    m_all = m_scr[0:1, :]
    qb = q.astype(BF16)
    qs = jnp.concatenate([jnp.where(lane_head == h, q, 0.0) for h in range(nh)], axis=0)
    qk_all = _dot_nt(qs.astype(BF16), k.astype(BF16))
    qn_sum = _group_sum(q * n_all, ones_ref[...])
    zc, zr = jnp.zeros((c, w), F32), jnp.zeros((1, w), F32)
    mt_l, ws_l, rs_l, wl_l, wc_l, mn_l = zc, zc, zc, zc, zr, zr
    p_parts = []
    for h in range(nh):
        i_col, f_col = x[:, h:h + 1], x[:, nh + h:nh + h + 1]
        i_row, f_row = xt[h:h + 1, :], xt[nh + h:nh + h + 1, :]
        a = f_col + m_all[:, h * d:h * d + 1]
        dmat = jnp.where(col_i <= row_i, f_col - f_row + i_row, NEG)
        m_t = jnp.maximum(a, jnp.max(dmat, axis=-1, keepdims=True))
        p_h = qk_all[h * c:(h + 1) * c] * jnp.exp(dmat - m_t)
        p_parts.append(p_h)
        m_new = m_t[c - 1:c, :]
        on_head = lane_head == h
        mt_l = jnp.where(on_head, m_t, mt_l)
        ws_l = jnp.where(on_head, jnp.exp(a - m_t), ws_l)
        rs_l = jnp.where(on_head, jnp.sum(p_h, axis=-1, keepdims=True), rs_l)
        wl_l = jnp.where(on_head, jnp.exp(f_col[c - 1:c, :] - f_col + i_col - m_new), wl_l)
        wc_l = jnp.where(on_head, jnp.exp(a[c - 1:c, :] - m_new), wc_l)
        mn_l = jnp.where(on_head, m_new, mn_l)
    nv = _dot(jnp.concatenate(p_parts, axis=0).astype(BF16), v.astype(BF16))
    num = jnp.where(lane_head == 0, nv[0:c], 0.0)
    for h in range(1, nh):
        num = jnp.where(lane_head == h, nv[h * c:(h + 1) * c], num)
    num = num + ws_l * _dot(qb, c_bd.astype(BF16))
    den = rs_l + ws_l * qn_sum
    hm = num / jnp.maximum(jnp.abs(den), jnp.exp(-mt_l))
    kw = k * wl_l
    same_head = (lax.broadcasted_iota(jnp.int32, (w, w), 0) // d) == (lax.broadcasted_iota(jnp.int32, (w, w), 1) // d)
    c_scr[...] = wc_l * c_bd + jnp.where(same_head, _dot_tn_hi(kw, v), 0.0)
    n_scr[0:1, :] = wc_l * n_all + jnp.sum(kw, axis=0, keepdims=True)
    m_scr[0:1, :] = mn_l
    hn = hm * lax.rsqrt(_group_mean_sq(hm, ones_ref[...]) + EPS) * ng_ref[...]
    om_ref[...] = (_sigmoid(og) * (hn + skip_ref[...] * u_act)).astype(BF16)

    @pl.when(i == pl.num_programs(1) - 1)
    def _():
        cout_ref[0] = c_scr[...]
        nout_ref[0] = n_scr[...]
        mout_ref[0] = m_scr[...]


def _block_diag_heads(w):
    nh, d, _ = w.shape
    eye = jnp.eye(nh, dtype=w.dtype)
    return jnp.einsum('hde,hg->hdge', w, eye).reshape(nh * d, nh * d)


def _mlstm(zm, conv0, c0, n0, m0, lp, batch, t_pad, t_valid):
    c = SCAN_CHUNK
    nblk = t_pad // c
    nh, d = MLSTM_HEADS, HEAD_DIM
    conv0p = jnp.concatenate([jnp.zeros((batch, 8 - (CONV_WIDTH - 1), MLSTM_W), F32), conv0.astype(F32)], axis=1)
    pad7 = jnp.zeros((batch, 7, MLSTM_W), F32)
    n0p = jnp.concatenate([n0.reshape(batch, 1, MLSTM_W), pad7], axis=1)
    m0p = jnp.concatenate([jnp.repeat(m0, d, axis=1).reshape(batch, 1, MLSTM_W), pad7], axis=1)
    gb = jnp.concatenate([lp['ml_gate_b'][0], lp['ml_gate_b'][1],
                          jnp.zeros((LANES - 2 * nh,), F32)]).reshape(1, LANES)
    tri = jnp.asarray(np.tril(np.ones((c, c), np.float32)))
    grp = np.arange(MLSTM_W) // d
    ones_bd = jnp.asarray(grp[:, None] == grp[None, :], BF16)
    kern = functools.partial(_mlstm_kernel, t_valid=t_valid, t_pad=t_pad)
    per_b = lambda shape: pl.BlockSpec((1,) + shape, lambda b, i: (b,) + (0,) * len(shape))
    om, c_f, n_f, m_f = pl.pallas_call(
        kern,
        grid=(batch, nblk),
        in_specs=[pl.BlockSpec((c, ZM_W), lambda b, i: (b * nblk + i, 0)),
                  per_b((8, MLSTM_W)), per_b((MLSTM_W, MLSTM_W)), per_b((8, MLSTM_W)), per_b((8, MLSTM_W)),
                  _const_spec((CONV_WIDTH, MLSTM_W)), _const_spec((1, MLSTM_W)),
                  _const_spec((MLSTM_W, MLSTM_W)), _const_spec((MLSTM_W, MLSTM_W)),
                  _const_spec((1, LANES)), _const_spec((1, MLSTM_W)), _const_spec((1, MLSTM_W)),
                  _const_spec((c, c)), _const_spec((MLSTM_W, MLSTM_W))],
        out_specs=[pl.BlockSpec((c, MLSTM_W), lambda b, i: (b * nblk + i, 0)),
                   per_b((MLSTM_W, MLSTM_W)), per_b((8, MLSTM_W)), per_b((8, MLSTM_W))],
        out_shape=[jax.ShapeDtypeStruct((batch * t_pad, MLSTM_W), BF16),
                   jax.ShapeDtypeStruct((batch, MLSTM_W, MLSTM_W), F32),
                   jax.ShapeDtypeStruct((batch, 8, MLSTM_W), F32),
                   jax.ShapeDtypeStruct((batch, 8, MLSTM_W), F32)],
        scratch_shapes=[pltpu.VMEM((MLSTM_W, MLSTM_W), F32), pltpu.VMEM((8, MLSTM_W), F32),
                        pltpu.VMEM((8, MLSTM_W), F32), pltpu.VMEM((8 + c, MLSTM_W), F32)],
        compiler_params=_params("parallel", "arbitrary"),
        name="mlstm_scan",
    )(zm, conv0p, _heads_to_block_diag(c0), n0p, m0p, lp['ml_conv_w'], lp['ml_conv_b'].reshape(1, MLSTM_W),
      _block_diag_heads(lp['ml_wq']).astype(BF16), _block_diag_heads(lp['ml_wk']).astype(BF16),
      gb, jnp.tile(lp['ml_norm_g'], nh).reshape(1, MLSTM_W), lp['ml_skip'].reshape(1, MLSTM_W),
      tri, ones_bd)
    return (om, _block_diag_to_heads(c_f, nh), n_f[:, 0].reshape(batch, nh, d),
            m_f[:, 0].reshape(batch, nh, d)[:, :, 0])


CMP_IN = CMP_STRIDE * 2 * KV_W
CMP_G = 2 * 2 * KV_W


def _cmp_weights(lp):
    w1 = lp['cmp_w1'].reshape(2, 2, CMP_STRIDE, HEAD_DIM, CMP_HIDDEN)
    eye = jnp.eye(2, dtype=F32)
    w_ab = jnp.einsum('sarde,st,hg->rshdatge', w1, eye, eye).reshape(CMP_IN, CMP_G)
    pe = lp['cmp_pe'].reshape(2, 2, CMP_STRIDE, 1, HEAD_DIM)
    pe = jnp.broadcast_to(jnp.transpose(pe, (1, 2, 0, 3, 4)),
                          (2, CMP_STRIDE, 2, NSA_KV_HEADS, HEAD_DIM)).reshape(2, CMP_IN)
    pe8 = jnp.concatenate([pe, jnp.zeros((6, CMP_IN), F32)], axis=0)
    b1 = jnp.broadcast_to(lp['cmp_b1'][:, None, :], (2, NSA_KV_HEADS, CMP_HIDDEN)).reshape(1, 2 * KV_W)
    w2 = jnp.einsum('sed,st,hg->shetgd', lp['cmp_w2'], eye, eye).reshape(2 * KV_W, 2 * KV_W)
    b2 = jnp.broadcast_to(lp['cmp_b2'][:, None, :], (2, NSA_KV_HEADS, HEAD_DIM)).reshape(1, 2 * KV_W)
    g1 = jnp.tile(lp['nsa_qk_g'][1], NSA_KV_HEADS).reshape(1, KV_W)
    grp = np.arange(KV_W) // HEAD_DIM
    ones_bd = jnp.asarray(grp[:, None] == grp[None, :], BF16)
    return (w_ab.astype(BF16), pe8.astype(BF16), b1, w2.astype(BF16), b2, g1, ones_bd)


def _cmp_finish(gsum, n_rows, wab_ref, pe_ref, b1_ref, w2_ref, b2_ref, g1_ref, ones_ref):
    half = 2 * KV_W
    g_pe = _dot(pe_ref[...], wab_ref[...])
    bias = g_pe[0:1, 0:half] + g_pe[1:2, half:CMP_G] + b1_ref[...]
    hid = gsum[:, 0:half] + pltpu.roll(gsum[:, half:CMP_G], n_rows - 1, 0) + bias
    act = hid * _sigmoid(hid)
    cmp = _dot(act.astype(BF16), w2_ref[...]) + b2_ref[...]
    kc = cmp[:, 0:KV_W]
    kc = kc * lax.rsqrt(_group_mean_sq(kc, ones_ref[...]) + EPS) * g1_ref[...]
    return kc, cmp[:, KV_W:half]


def _cmp_kernel(r_ref, wab_ref, pe_ref, b1_ref, w2_ref, b2_ref, g1_ref, ones_ref, kc_ref, vc_ref, *, n_rows):
    gsum = _dot(r_ref[0], wab_ref[...])
    kc, vc = _cmp_finish(gsum, n_rows, wab_ref, pe_ref, b1_ref, w2_ref, b2_ref, g1_ref, ones_ref)
    kc_ref[0] = kc.astype(BF16)
    vc_ref[0] = vc.astype(BF16)


def _cmp_specs():
    half = 2 * KV_W
    return [_const_spec((CMP_IN, CMP_G)), _const_spec((8, CMP_IN)), _const_spec((1, half)),
            _const_spec((half, half)), _const_spec((1, half)), _const_spec((1, KV_W)),
            _const_spec((KV_W, KV_W))]


def _compress(r01, cw, batch, t_len):
    n16 = t_len // CMP_STRIDE
    x = r01.reshape(batch, n16, CMP_IN)
    blk = lambda w: pl.BlockSpec((1, n16, w), lambda b: (b, 0, 0))
    return pl.pallas_call(
        functools.partial(_cmp_kernel, n_rows=n16),
        grid=(batch,),
        in_specs=[blk(CMP_IN)] + _cmp_specs(),
        out_specs=[blk(KV_W), blk(KV_W)],
        out_shape=[jax.ShapeDtypeStruct((batch, n16, KV_W), BF16)] * 2,
        compiler_params=_params("parallel"),
        name="nsa_compress",
    )(x, *cw)


def _nsa_kernel(qpad_ref, ksel_ref, vsel_ref, kwin_ref, vwin_ref, kc_ref, vc_ref, gate_ref, mt_ref,
                et_ref, pl_ref, bound_ref, out_ref, sc_scr, lhs_scr, m_scr, l_scr, acc_scr, *, nb, fixed_max):
    qi = pl.program_id(2)
    tq, grp = NSA_TQ, NSA_GROUP
    tk = NSA_TK_FIXED if fixed_max else NSA_TK
    rows = grp * tq
    nbp, ncp = mt_ref.shape
    start = qi * tq
    q4 = jnp.concatenate([qpad_ref[:, g * LANES:(g + 1) * LANES] for g in range(grp)], axis=0)
    tpos = start + (lax.broadcasted_iota(jnp.int32, (rows, 1), 0) & (tq - 1))

    sc = _dot_nt(q4, kc_ref[0])
    ccol = lax.broadcasted_iota(jnp.int32, (rows, ncp), 1)
    vis = (ccol * CMP_STRIDE + (CMP_BLOCK - 1)) <= tpos
    s_m = jnp.where(vis, sc, NEG)
    e = jnp.where(vis, jnp.exp(s_m - jnp.max(s_m, axis=-1, keepdims=True)), 0.0)
    p_c = e / jnp.maximum(jnp.sum(e, axis=-1, keepdims=True), 1e-30)
    o_c = _dot(p_c.astype(BF16), vc_ref[0])

    pg = p_c[0:tq]
    for g in range(1, grp):
        pg = pg + p_c[g * tq:(g + 1) * tq]
    h1 = pg.astype(BF16)
    r1 = pg - h1.astype(F32)
    h2 = r1.astype(BF16)
    h3 = (r1 - h2.astype(F32)).astype(BF16)
    mt = mt_ref[...]
    p_sel = _dot_nt(mt, h1) + _dot_nt(mt, h2) + _dot_nt(mt, h3)
    blk = lax.broadcasted_iota(jnp.int32, (nbp, tq), 0)
    cur = (start + lax.broadcasted_iota(jnp.int32, (nbp, tq), 1)) // SEL_BLOCK
    forced = (blk == 0) | (blk == cur) | (blk == cur - 1)
    score = jnp.where(blk > cur, -1.0, jnp.where(forced, FORCE_SCORE, p_sel))
    if nb < nbp:
        score = jnp.where(blk >= nb, -2.0, score)
    sc_scr[...] = score

    n_grp = nbp // 8
    s_grp = [score[8 * v:8 * v + 8, :] for v in range(n_grp)]
    cnt = [jnp.zeros((8, tq), F32) for _ in range(n_grp)]
    sub = lax.broadcasted_iota(jnp.int32, (8, tq), 0)
    for i in range(nb):
        row = sc_scr[i:i + 1, :]
        vi, ri = divmod(i, 8)
        for v in range(n_grp):
            if v > vi:
                hit = jnp.where(row >= s_grp[v], 1.0, 0.0)
            elif v < vi:
                hit = jnp.where(row > s_grp[v], 1.0, 0.0)
            else:
                hit = jnp.where(sub > ri, jnp.where(row >= s_grp[v], 1.0, 0.0),
                                jnp.where(row > s_grp[v], 1.0, 0.0))
            cnt[v] = cnt[v] + hit
    rank = jnp.concatenate(cnt, axis=0)
    unsel = jnp.where(rank < SEL_TOPK, 0.0, -1.0)
    unsel_t = unsel.T.astype(BF16)
    lhs_scr[...] = jnp.concatenate([q4, jnp.concatenate([unsel_t] * grp, axis=0)], axis=1)

    l_scr[...] = jnp.zeros((rows, LANES), F32)
    acc_scr[...] = jnp.zeros((rows, LANES), F32)
    n_full = start // tk

    def key_operands(kt):
        koff = pl.multiple_of(kt * tk, tk)
        kk = jnp.concatenate([ksel_ref[pl.ds(koff, tk), :], et_ref[pl.ds(koff, tk), :]], axis=1)
        return koff, kk, vsel_ref[pl.ds(koff, tk), :]

    if fixed_max:
        bound = bound_ref[0, 0]

        def key_tile(kt, masked):
            koff, kk, vv = key_operands(kt)
            s = _dot_nt(lhs_scr[...], kk)
            if masked:
                s = jnp.where(koff + lax.broadcasted_iota(jnp.int32, (rows, tk), 1) <= tpos, s, NEG)
            p = jnp.exp(s - bound)
            part = p[:, 0:LANES]
            for c in range(1, tk // LANES):
                part = part + p[:, c * LANES:(c + 1) * LANES]
            l_scr[...] = l_scr[...] + part
            acc_scr[...] = acc_scr[...] + _dot(p.astype(BF16), vv)
    else:
        m_scr[...] = jnp.full((rows, LANES), NEG, F32)
        qpos = start + lax.broadcasted_iota(jnp.int32, (tq, tk), 0)

        def key_tile(kt, masked):
            koff, kk, vv = key_operands(kt)
            for g in range(grp):
                rs = slice(g * tq, (g + 1) * tq)
                s = _dot_nt(lhs_scr[rs, :], kk)
                if masked:
                    s = jnp.where(koff + lax.broadcasted_iota(jnp.int32, (tq, tk), 1) <= qpos, s, NEG)
                m_old = m_scr[rs, :]
                m_new = jnp.maximum(m_old, jnp.max(s, axis=-1, keepdims=True))
                p = jnp.exp(s - jnp.concatenate([m_new] * (tk // LANES), axis=1))
                alpha = jnp.exp(m_old - m_new)
                l_scr[rs, :] = alpha * l_scr[rs, :] + jnp.sum(p, axis=-1, keepdims=True)
                acc_scr[rs, :] = alpha * acc_scr[rs, :] + _dot(p.astype(BF16), vv)
                m_scr[rs, :] = m_new

    def tile_pair(kp, carry):
        key_tile(2 * kp, False)
        key_tile(2 * kp + 1, False)
        return carry
    lax.fori_loop(0, n_full // 2, tile_pair, 0)

    @pl.when(n_full % 2 == 1)
    def _():
        key_tile(n_full - 1, False)
    key_tile(n_full, True)
    if fixed_max:
        o_s = acc_scr[...] / jnp.sum(l_scr[...], axis=-1, keepdims=True)
    else:
        o_s = acc_scr[...] / l_scr[...]

    wk = WINDOW + tq
    wstart = pl.multiple_of(jnp.maximum(start - WINDOW, 0), tq)
    s = _dot_nt(q4, kwin_ref[pl.ds(wstart, wk), :])
    wpos = wstart + lax.broadcasted_iota(jnp.int32, (rows, wk), 1)
    s = jnp.where((wpos <= tpos) & (wpos > tpos - WINDOW), s, NEG)
    e = jnp.exp(s - jnp.max(s, axis=-1, keepdims=True))
    o_w = _dot(e.astype(BF16), vwin_ref[pl.ds(wstart, wk), :]) / jnp.sum(e, axis=-1, keepdims=True)

    gt = gate_ref[...]

    def gate(branch):
        return jnp.concatenate([gt[0:tq, g * 3 + branch:g * 3 + branch + 1] for g in range(grp)], axis=0)
    o = (gate(0) * o_c + gate(1) * o_s + gate(2) * o_w).astype(BF16)
    out = _dot(o[0:tq], pl_ref[0, 0])
    for g in range(1, grp):
        out = out + _dot(o[g * tq:(g + 1) * tq], pl_ref[0, g])
    out_ref[...] = out.astype(BF16)


def _nsa_constants(t_len):
    nb = t_len // SEL_BLOCK
    nbp = max(LANES, -(-nb // LANES) * LANES)
    ncp = t_len // CMP_STRIDE
    j = np.arange(nbp)[:, None]
    n = np.arange(ncp)[None, :]
    first = (n >= SEL_RATIO * j) & (n <= SEL_RATIO * j + SEL_RATIO - 1)
    second = (n >= SEL_RATIO * j - 1) & (n <= SEL_RATIO * j + SEL_RATIO - 2)
    mt = (first.astype(np.float32) + second.astype(np.float32)) * (n < ncp - 1) * (j < nb)
    et = (np.arange(t_len)[:, None] // SEL_BLOCK == np.arange(nbp)[None, :]).astype(np.float32) * SEL_BIAS
    place = np.zeros((NSA_KV_HEADS, NSA_GROUP, LANES, NSA_GROUP * HEAD_DIM), np.float32)
    for h in range(NSA_KV_HEADS):
        for g in range(NSA_GROUP):
            place[h, g, h * HEAD_DIM + np.arange(HEAD_DIM), g * HEAD_DIM + np.arange(HEAD_DIM)] = 1.0
    return nb, jnp.asarray(mt, BF16), jnp.asarray(et, BF16), jnp.asarray(place, BF16)


def _nsa_attend(qpad, kv, kc, vc, gates, qk_g, batch, t_len):
    tq = NSA_TQ
    nq = t_len // tq
    rows = NSA_GROUP * tq
    nb, mt, et, place = _nsa_constants(t_len)
    nbp, ncp = mt.shape
    seq = lambda c: pl.BlockSpec((t_len, KV_W), lambda b, h, i: (b, c))
    bound = (HEAD_DIM ** 0.5) * jnp.max(jnp.abs(qk_g[0])) * jnp.max(jnp.abs(qk_g[2])) * 1.02 + 0.1

    def attend(fixed_max):
        return pl.pallas_call(
            functools.partial(_nsa_kernel, nb=nb, fixed_max=fixed_max),
            grid=(batch, NSA_KV_HEADS, nq),
            in_specs=[pl.BlockSpec((tq, NSA_GROUP * LANES), lambda b, h, i: (b * nq + i, h)),
                      seq(0), seq(1), seq(2), seq(3),
                      pl.BlockSpec((1, ncp, KV_W), lambda b, h, i: (b, 0, 0)),
                      pl.BlockSpec((1, ncp, KV_W), lambda b, h, i: (b, 0, 0)),
                      pl.BlockSpec((tq, LANES), lambda b, h, i: (b * nq + i, h)),
                      _const_spec((nbp, ncp)), _const_spec((t_len, nbp)),
                      pl.BlockSpec((1, NSA_GROUP, LANES, NSA_GROUP * HEAD_DIM), lambda b, h, i: (h, 0, 0, 0)),
                      pl.BlockSpec(memory_space=pltpu.SMEM)],
            out_specs=pl.BlockSpec((tq, NSA_GROUP * HEAD_DIM), lambda b, h, i: (b * nq + i, h)),
            out_shape=jax.ShapeDtypeStruct((batch * t_len, NSA_W), BF16),
            scratch_shapes=[pltpu.VMEM((nbp, tq), F32), pltpu.VMEM((rows, LANES + nbp), BF16),
                            pltpu.VMEM((rows, LANES), F32), pltpu.VMEM((rows, LANES), F32),
                            pltpu.VMEM((rows, LANES), F32)],
            compiler_params=_params("parallel", "parallel", "arbitrary"),
            name="nsa_attend",
        )(qpad, kv, kv, kv, kv, kc, vc, gates, mt, et, place, bound.reshape(1, 1))
    return lax.cond(bound <= NSA_MAX_FIXED_SHIFT, lambda: attend(True), lambda: attend(False))


def _outffn_kernel(x_ref, og_ref, on_ref, om_ref, wo_ref, g2_ref, wu_ref, wd_ref, y_ref):
    x1 = (x_ref[...] + _dot(og_ref[...], wo_ref[0:GLA_W, :])
          + _dot(on_ref[...], wo_ref[GLA_W:GLA_W + NSA_W, :])
          + _dot(om_ref[...], wo_ref[GLA_W + NSA_W:GLA_W + NSA_W + MLSTM_W, :]))
    h = x1 * lax.rsqrt(jnp.mean(x1 * x1, axis=-1, keepdims=True) + EPS) * g2_ref[...]
    hid = jnp.maximum(_dot(h.astype(BF16), wu_ref[...]), 0.0)
    y_ref[...] = x1 + _dot((hid * hid).astype(BF16), wd_ref[...])


def _out_ffn(x, og, on, om, wo, g2, wu, wd, tm):
    n = x.shape[0]
    row = lambda w: pl.BlockSpec((tm, w), lambda i: (i, 0))
    return pl.pallas_call(
        _outffn_kernel,
        grid=(n // tm,),
        in_specs=[row(D_MODEL), row(GLA_W), row(NSA_W), row(MLSTM_W),
                  _const_spec((D_MODEL, D_MODEL)), _const_spec((1, D_MODEL)),
                  _const_spec((D_MODEL, D_FF)), _const_spec((D_FF, D_MODEL))],
        out_specs=row(D_MODEL),
        out_shape=jax.ShapeDtypeStruct((n, D_MODEL), F32),
        compiler_params=_params("parallel"),
        name="out_ffn",
    )(x, og, on, om, wo, g2, wu, wd)


def _layer_weights(lp):
    return {'w_in': _pack_w_in(lp['w_in']), 'w_out': lp['w_out'].astype(BF16),
            'w_up': lp['w_up'].astype(BF16), 'w_down': lp['w_down'].astype(BF16),
            'cmp': _cmp_weights(lp)}


def _rows_on_lanes_to_state(a, n_slots):
    batch, _, n_rows = a.shape
    return jnp.transpose(a.reshape(batch, n_slots, NSA_KV_HEADS, HEAD_DIM, n_rows), (0, 4, 1, 2, 3))


def _prompt_layer(x, lp, lw, batch, t_len):
    d = HEAD_DIM
    zg, zn, zm = _in_proj(x, lp['norm1_g'].reshape(1, D_MODEL), lw['w_in'], 256)
    qpad, rows_t, win_t, kv, r01, gates = _nsa_prep(zn, lp['nsa_qk_g'], lp['nsa_gate_b'], 256,
                                                    seq=(batch, t_len))
    zero = lambda *s: jnp.zeros(s, F32)
    og, s_gla = _gla(zg, zero(batch, GLA_HEADS, d, d), lp['gla_a_w'], lp['gla_a_b'], lp['gla_norm_g'],
                     batch, t_len, t_len)
    om, c_m, n_m, m_m = _mlstm(zm, zero(batch, CONV_WIDTH - 1, MLSTM_W), zero(batch, MLSTM_HEADS, d, d),
                               zero(batch, MLSTM_HEADS, d), zero(batch, MLSTM_HEADS), lp, batch, t_len, t_len)
    kc, vc = _compress(r01, lw['cmp'], batch, t_len)
    on = _nsa_attend(qpad, kv, kc, vc, gates, lp['nsa_qk_g'], batch, t_len)
    y = _out_ffn(x, og, on, om, lw['w_out'], lp['norm2_g'].reshape(1, D_MODEL), lw['w_up'], lw['w_down'], 256)
    wlen = min(WINDOW, t_len)
    new_rows = _rows_on_lanes_to_state(rows_t, 4)
    win_state = _rows_on_lanes_to_state(win_t[:, :, t_len - wlen:], 2)
    conv_state = zm.reshape(batch, t_len, ZM_W)[:, t_len - (CONV_WIDTH - 1):, 0:MLSTM_W]
    return y, (new_rows, win_state, s_gla, c_m, n_m, m_m, conv_state)


GATHER_PAGES = 32
GROUPS_PER_PAGE = PAGE_SIZE // CMP_STRIDE


def _paged_cache_view(cache_nsa_kv):
    depth, n_pool = cache_nsa_kv.shape[0], cache_nsa_kv.shape[1]
    return jnp.transpose(cache_nsa_kv, (0, 1, 3, 4, 5, 2)).reshape(depth, n_pool, ROWS_W, PAGE_SIZE)


def _window_cache_view(cache_nsa_win):
    depth, batch, wb = cache_nsa_win.shape[0:3]
    return jnp.transpose(cache_nsa_win, (0, 1, 3, 4, 5, 2)).reshape(depth, batch, WIN_W, wb)


def _cmp_gather_kernel(pt_ref, ct_ref, w_ref, perm_ref, g_ref, buf, xs, sems, *, layer):
    s = pl.program_id(0)
    slot = s % 2

    def page_copies(step, dst):
        return [pltpu.make_async_copy(ct_ref.at[layer, pt_ref[step * GATHER_PAGES + p], pl.ds(0, 2 * KV_W), :],
                                      buf.at[dst, p], sems.at[dst, p]) for p in range(GATHER_PAGES)]

    @pl.when(s == 0)
    def _():
        for c in page_copies(0, 0):
            c.start()

    @pl.when(s + 1 < pl.num_programs(0))
    def _():
        for c in page_copies(s + 1, 1 - slot):
            c.start()
    for c in page_copies(s, slot):
        c.wait()

    perm = perm_ref[...]
    for p in range(GATHER_PAGES):
        xs[p] = _dot_nt(perm, buf[slot, p].astype(BF16))
    parts = []
    n_rows = GATHER_PAGES * GROUPS_PER_PAGE
    for sl in range(2):
        acc = jnp.zeros((n_rows, 2 * KV_W), F32)
        for rp in range(CMP_STRIDE // 2):
            xr = jnp.concatenate(
                [xs[:, r * GROUPS_PER_PAGE:(r + 1) * GROUPS_PER_PAGE, sl * KV_W:(sl + 1) * KV_W]
                 .reshape(n_rows, KV_W) for r in (2 * rp, 2 * rp + 1)], axis=1)
            acc = acc + _dot(xr.astype(BF16), w_ref[rp, sl])
        parts.append(acc)
    g_ref[0] = jnp.concatenate([parts[0][:, 0:KV_W], parts[1][:, 0:KV_W],
                                parts[0][:, KV_W:2 * KV_W], parts[1][:, KV_W:2 * KV_W]], axis=1)


def _cmp_gather(ct, layer, pt_flat, w_ab, batch, n_pages):
    assert n_pages % GATHER_PAGES == 0
    steps = n_pages // GATHER_PAGES
    rows = GATHER_PAGES * GROUPS_PER_PAGE
    w4 = w_ab.reshape(CMP_STRIDE, 2, KV_W, 2, 2, KV_W)
    w_rs = jnp.stack([w4[:, s, :, :, s, :] for s in range(2)], axis=1).reshape(CMP_STRIDE, 2, KV_W, 2 * KV_W)
    w_rs = jnp.transpose(w_rs.reshape(CMP_STRIDE // 2, 2, 2, KV_W, 2 * KV_W), (0, 2, 1, 3, 4))
    w_rs = w_rs.reshape(CMP_STRIDE // 2, 2, 2 * KV_W, 2 * KV_W)
    i = np.arange(PAGE_SIZE)
    perm = np.zeros((PAGE_SIZE, PAGE_SIZE), np.float32)
    perm[i, (i % GROUPS_PER_PAGE) * CMP_STRIDE + i // GROUPS_PER_PAGE] = 1.0
    perm = jnp.asarray(perm, BF16)
    grid_spec = pltpu.PrefetchScalarGridSpec(
        num_scalar_prefetch=1,
        grid=(batch * steps,),
        in_specs=[pl.BlockSpec(memory_space=pl.ANY),
                  pl.BlockSpec((CMP_STRIDE // 2, 2, 2 * KV_W, 2 * KV_W), lambda s, pt: (0, 0, 0, 0)),
                  pl.BlockSpec((PAGE_SIZE, PAGE_SIZE), lambda s, pt: (0, 0))],
        out_specs=pl.BlockSpec((1, rows, CMP_G), lambda s, pt: (s // steps, s % steps, 0)),
        scratch_shapes=[pltpu.VMEM((2, GATHER_PAGES, 2 * KV_W, PAGE_SIZE), F32),
                        pltpu.VMEM((GATHER_PAGES, PAGE_SIZE, 2 * KV_W), F32),
                        pltpu.SemaphoreType.DMA((2, GATHER_PAGES))],
    )
    return pl.pallas_call(
        functools.partial(_cmp_gather_kernel, layer=layer),
        grid_spec=grid_spec,
        out_shape=jax.ShapeDtypeStruct((batch, n_pages * GROUPS_PER_PAGE, CMP_G), F32),
        compiler_params=_params("arbitrary"),
        name="nsa_cmp_gather",
    )(pt_flat, ct, w_rs, perm)


def _to_col(row, n):
    eye = (lax.broadcasted_iota(jnp.int32, (n, n), 0) == lax.broadcasted_iota(jnp.int32, (n, n), 1))
    return jnp.sum(jnp.where(eye, jnp.broadcast_to(row, (n, n)), 0.0), axis=1, keepdims=True)


def _dec_cmp_kernel(g_ref, q_ref, wc_ref, wnew_ref, wab_ref, pe_ref, b1_ref, w2_ref, b2_ref, g1_ref,
                    ones_ref, m_ref, oc_ref, ow_ref, idx_ref, wout_ref, *, n_groups, nb, cur):
    nh, grp = NSA_KV_HEADS, NSA_GROUP
    q8 = q_ref[0]
    kc, vc = _cmp_finish(g_ref[0], n_groups, wab_ref, pe_ref, b1_ref, w2_ref, b2_ref, g1_ref, ones_ref)
    sc = _dot_nt(q8, kc.astype(BF16))
    vis = lax.broadcasted_iota(jnp.int32, sc.shape, 1) < n_groups - 1
    s_m = jnp.where(vis, sc, NEG)
    e = jnp.where(vis, jnp.exp(s_m - jnp.max(s_m, axis=-1, keepdims=True)), 0.0)
    p_c = e / jnp.maximum(jnp.sum(e, axis=-1, keepdims=True), 1e-30)
    oc_ref[0] = _dot(p_c.astype(BF16), vc.astype(BF16))

    nsp = m_ref.shape[1]
    pg = jnp.concatenate([jnp.sum(p_c[h * grp:(h + 1) * grp], axis=0, keepdims=True) for h in range(nh)]
                         + [jnp.zeros((8 - nh, n_groups), F32)], axis=0)
    h1 = pg.astype(BF16)
    r1 = pg - h1.astype(F32)
    h2 = r1.astype(BF16)
    h3 = (r1 - h2.astype(F32)).astype(BF16)
    m = m_ref[...]
    p_sel = _dot(h1, m) + _dot(h2, m) + _dot(h3, m)
    blk = lax.broadcasted_iota(jnp.int32, (1, nsp), 1)
    forced = (blk == 0) | (blk == cur) | (blk == cur - 1)
    ii = lax.broadcasted_iota(jnp.int32, (nsp, nsp), 0)
    jj = lax.broadcasted_iota(jnp.int32, (nsp, nsp), 1)
    slot = lax.broadcasted_iota(jnp.int32, (SEL_TOPK, nsp), 0).astype(F32)
    blk_f = lax.broadcasted_iota(jnp.int32, (SEL_TOPK, nsp), 1).astype(F32)
    for h in range(nh):
        score = jnp.where(blk > cur, -1.0, jnp.where(forced, FORCE_SCORE, p_sel[h:h + 1, :]))
        score = jnp.where(blk >= nb, -2.0, score)
        s_col = _to_col(score, nsp)
        ge = jnp.where(s_col >= score, 1.0, 0.0)
        gt = jnp.where(s_col > score, 1.0, 0.0)
        rank = jnp.sum(jnp.where(jj > ii, ge, gt), axis=0, keepdims=True)
        sel = (rank < SEL_TOPK).astype(F32)
        before = jnp.sum(jnp.where(ii < jj, _to_col(sel, nsp), 0.0), axis=0, keepdims=True)
        onehot = jnp.where((before == slot) & (sel > 0.5), 1.0, 0.0)
        idx = jnp.sum(onehot * blk_f, axis=1, keepdims=True)
        idx_ref[0, h] = jnp.broadcast_to(idx, (SEL_TOPK, LANES)).astype(jnp.int32)

    wt = wc_ref[0, 0]
    wb = wt.shape[1]
    wnew = wnew_ref[0]
    s = _dot(q8, wt[0:KV_W].astype(BF16))
    s = jnp.where(lax.broadcasted_iota(jnp.int32, s.shape, 1) > wb - WINDOW, s, NEG)
    qf = q8.astype(F32)
    s_new = jnp.sum(qf * wnew[:, 0:KV_W].astype(BF16).astype(F32), axis=-1, keepdims=True)
    mx = jnp.maximum(jnp.max(s, axis=-1, keepdims=True), s_new)
    e = jnp.exp(s - mx)
    e_new = jnp.exp(s_new - mx)
    num = _dot_nt(e.astype(BF16), wt[KV_W:2 * KV_W].astype(BF16)) + e_new * wnew[:, KV_W:2 * KV_W]
    ow_ref[0] = num / (jnp.sum(e, axis=-1, keepdims=True) + e_new)
    lane = lax.broadcasted_iota(jnp.int32, wt.shape, 1)
    wout_ref[0] = jnp.where(lane == wb - 1, _to_col(wnew, 2 * KV_W), pltpu.roll(wt, wb - 1, 1))


def _dec_constants(past):
    n_groups = past // CMP_STRIDE
    nb = past // SEL_BLOCK + 1
    nsp = -(-nb // LANES) * LANES
    j = np.arange(nsp)[None, :]
    n = np.arange(n_groups)[:, None]
    first = (n >= SEL_RATIO * j) & (n <= SEL_RATIO * j + SEL_RATIO - 1)
    second = (n >= SEL_RATIO * j - 1) & (n <= SEL_RATIO * j + SEL_RATIO - 2)
    m = (first.astype(np.float32) + second.astype(np.float32)) * (n < n_groups - 1) * (j < nb)
    return n_groups, nb, jnp.asarray(m, BF16)


def _dec_cmp_attn(gsum, q8, wt, layer, win_new, cw, batch, past):
    n_groups, nb, m = _dec_constants(past)
    nsp = m.shape[1]
    wb = wt.shape[3]
    assert wb == WINDOW
    per_b = lambda shape: pl.BlockSpec((1,) + shape, lambda b: (b,) + (0,) * len(shape))
    kern = functools.partial(_dec_cmp_kernel, n_groups=n_groups, nb=nb, cur=past // SEL_BLOCK)
    return pl.pallas_call(
        kern,
        grid=(batch,),
        in_specs=[per_b((n_groups, CMP_G)), per_b((NSA_HEADS, LANES)),
                  pl.BlockSpec((1, 1, 2 * KV_W, wb), lambda b: (layer, b, 0, 0)),
                  per_b((1, 2 * KV_W))] + _cmp_specs() + [_const_spec((n_groups, nsp))],
        out_specs=[per_b((NSA_HEADS, LANES)), per_b((NSA_HEADS, LANES)),
                   per_b((NSA_KV_HEADS, SEL_TOPK, LANES)), per_b((2 * KV_W, wb))],
        out_shape=[jax.ShapeDtypeStruct((batch, NSA_HEADS, LANES), F32),
                   jax.ShapeDtypeStruct((batch, NSA_HEADS, LANES), F32),
                   jax.ShapeDtypeStruct((batch, NSA_KV_HEADS, SEL_TOPK, LANES), jnp.int32),
                   jax.ShapeDtypeStruct((batch, 2 * KV_W, wb), F32)],
        compiler_params=_params("parallel"),
        name="nsa_decode_cmp",
    )(gsum, q8, wt, win_new, *cw, m)


def _dec_sel_kernel(pt_ref, idx_ref, q_ref, new_ref, oc_ref, ow_ref, gate_ref, ct_ref, out_ref,
                    kv_buf, sems, *, layer, n_pages, n_cache_blocks):
    b = pl.program_id(0)
    buf_slot = b % 2
    nh, nq, topk = NSA_KV_HEADS, NSA_HEADS, SEL_TOPK
    half = PAGE_SIZE // SEL_BLOCK
    nblk = nh * topk

    def page_copies(seq, dst):
        copies = []
        for j in range(nblk):
            blk = jnp.minimum(idx_ref[seq * nblk + j], n_cache_blocks - 1)
            page = pt_ref[seq * n_pages + blk // half]
            copies.append(pltpu.make_async_copy(ct_ref.at[layer, page, pl.ds(2 * KV_W, 2 * KV_W), :],
                                                kv_buf.at[dst, j], sems.at[dst, j]))
        return copies

    @pl.when(b == 0)
    def _():
        for c in page_copies(0, 0):
            c.start()

    @pl.when(b + 1 < pl.num_programs(0))
    def _():
        for c in page_copies(b + 1, 1 - buf_slot):
            c.start()
    for c in page_copies(b, buf_slot):
        c.wait()

    q8 = q_ref[0]
    new = new_ref[0]
    s_new = jnp.sum(q8 * new[:, 2 * KV_W:3 * KV_W].astype(BF16).astype(F32), axis=-1, keepdims=True)
    v_new = new[:, 3 * KV_W:4 * KV_W]
    qb = q8.astype(BF16)
    page_half = lax.broadcasted_iota(jnp.int32, (1, PAGE_SIZE), 1) // SEL_BLOCK
    o_heads = []
    for h in range(nh):
        s_parts = []
        m = s_new
        for k in range(topk):
            j = h * topk + k
            blk = idx_ref[b * nblk + j]
            s_k = _dot(qb, kv_buf[buf_slot, j, 0:KV_W, :].astype(BF16))
            keep = jnp.where(blk < n_cache_blocks, 0.0, NEG)
            s_k = s_k + jnp.where(page_half == blk % half, keep, NEG)
            s_parts.append(s_k)
            m = jnp.maximum(m, jnp.max(s_k, axis=-1, keepdims=True))
        e_new = jnp.exp(s_new - m)
        den = e_new
        num = e_new * v_new
        for k in range(topk):
            e = jnp.exp(s_parts[k] - m)
            den = den + jnp.sum(e, axis=-1, keepdims=True)
            num = num + _dot_nt(e.astype(BF16), kv_buf[buf_slot, h * topk + k, KV_W:2 * KV_W, :].astype(BF16))
        o_heads.append(num / den)
    row = lax.broadcasted_iota(jnp.int32, (nq, KV_W), 0)
    o_s = jnp.where(row < NSA_GROUP, o_heads[0], o_heads[1])
    gt = gate_ref[0]
    out_ref[0] = gt[:, 0:1] * oc_ref[0] + gt[:, 1:2] * o_s + gt[:, 2:3] * ow_ref[0]


def _dec_sel_attn(ct, layer, pt_flat, idx_flat, q8, rows_new, o_c, o_w, gates, batch, n_pages):
    half = PAGE_SIZE // SEL_BLOCK
    nh, grp, nq = NSA_KV_HEADS, NSA_GROUP, NSA_HEADS
    g3 = gates.reshape(batch, nh, LANES)[:, :, 0:3 * grp].reshape(batch, nq, 3)
    g8 = jnp.concatenate([g3, jnp.zeros((batch, nq, LANES - 3), F32)], axis=-1)
    new8 = jnp.broadcast_to(rows_new.reshape(batch, 1, ROWS_W), (batch, nq, ROWS_W))

    per_b = lambda w: pl.BlockSpec((1, nq, w), lambda b, pt, idx: (b, 0, 0))
    grid_spec = pltpu.PrefetchScalarGridSpec(
        num_scalar_prefetch=2,
        grid=(batch,),
        in_specs=[per_b(LANES), per_b(ROWS_W), per_b(LANES), per_b(LANES), per_b(LANES),
                  pl.BlockSpec(memory_space=pl.ANY)],
        out_specs=per_b(LANES),
        scratch_shapes=[pltpu.VMEM((2, nh * SEL_TOPK, 2 * KV_W, PAGE_SIZE), F32),
                        pltpu.SemaphoreType.DMA((2, nh * SEL_TOPK))],
    )
    kern = functools.partial(_dec_sel_kernel, layer=layer, n_pages=n_pages, n_cache_blocks=n_pages * half)
    return pl.pallas_call(
        kern,
        grid_spec=grid_spec,
        out_shape=jax.ShapeDtypeStruct((batch, nq, LANES), F32),
        compiler_params=_params("arbitrary"),
        name="nsa_decode_sel",
    )(pt_flat, idx_flat, q8.astype(F32), new8, o_c, o_w, g8, ct)


def _sample_layer(x, lp, lw, layer, ct, wt, gla_s0, c0, n0, m0, conv0, page_table):
    batch, n_pages = page_table.shape
    past = n_pages * PAGE_SIZE
    d = HEAD_DIM
    c = SCAN_CHUNK
    zg, zn, zm = _in_proj(x, lp['norm1_g'].reshape(1, D_MODEL), lw['w_in'], batch)
    qpad, rows, win, _, _, gates = _nsa_prep(zn, lp['nsa_qk_g'], lp['nsa_gate_b'], batch)
    pad = lambda z: jnp.pad(z[:, None, :], ((0, 0), (0, c - 1), (0, 0))).reshape(batch * c, z.shape[-1])
    first = lambda o: o.reshape(batch, c, o.shape[-1])[:, 0]
    og, s_gla = _gla(pad(zg), gla_s0, lp['gla_a_w'], lp['gla_a_b'], lp['gla_norm_g'], batch, c, 1)
    om, c_m, n_m, m_m = _mlstm(pad(zm), conv0, c0, n0, m0, lp, batch, c, 1)
    pt_flat = page_table.reshape(-1)
    gsum = _cmp_gather(ct, layer, pt_flat, lw['cmp'][0], batch, n_pages)
    q8 = qpad.reshape(batch, NSA_HEADS, LANES)
    o_c, o_w, idx, win_t = _dec_cmp_attn(gsum, q8, wt, layer, win.reshape(batch, 1, 2 * KV_W), lw['cmp'],
                                         batch, past)
    o_n = _dec_sel_attn(ct, layer, pt_flat, idx[:, :, :, 0].reshape(-1), q8, rows, o_c, o_w, gates,
                        batch, n_pages)
    on = jnp.stack([o_n[:, h * NSA_GROUP:(h + 1) * NSA_GROUP, h * d:(h + 1) * d]
                    for h in range(NSA_KV_HEADS)], axis=1)
    on = on.reshape(batch, NSA_W).astype(BF16)
    y = _out_ffn(x, first(og), on, first(om), lw['w_out'], lp['norm2_g'].reshape(1, D_MODEL),
                 lw['w_up'], lw['w_down'], batch)
    new_rows = rows.reshape(batch, 1, 4, NSA_KV_HEADS, d)
    win_state = _rows_on_lanes_to_state(win_t, 2)
    conv_state = jnp.concatenate([conv0.astype(F32), zm[:, None, 0:MLSTM_W]], axis=1)[:, 1:]
    return y, (new_rows, win_state, s_gla, c_m, n_m, m_m, conv_state)


def kernel(x_prompt, x_sample, cache_nsa_kv, cache_nsa_win, state_gla, state_mlstm_c, state_mlstm_n,
           state_mlstm_m, state_mlstm_conv, page_table, norm1_g, w_in, gla_a_w, gla_a_b, gla_norm_g,
           nsa_qk_g, nsa_gate_b, cmp_pe, cmp_w1, cmp_b1, cmp_w2, cmp_b2, ml_conv_w, ml_conv_b, ml_wq, ml_wk,
           ml_gate_b, ml_norm_g, ml_skip, w_out, norm2_g, w_up, w_down):
    bp, t_len, _ = x_prompt.shape
    n_dec = x_sample.shape[0]
    depth = w_in.shape[0]
    x_p = x_prompt.reshape(bp * t_len, D_MODEL)
    x_s = x_sample.reshape(n_dec, D_MODEL)
    states_p, states_s = [], []
    ct = _paged_cache_view(cache_nsa_kv)
    wt = _window_cache_view(cache_nsa_win)
    for l in range(depth):
        lp = {'norm1_g': norm1_g[l], 'w_in': w_in[l], 'gla_a_w': gla_a_w[l], 'gla_a_b': gla_a_b[l],
              'gla_norm_g': gla_norm_g[l], 'nsa_qk_g': nsa_qk_g[l], 'nsa_gate_b': nsa_gate_b[l],
              'cmp_pe': cmp_pe[l], 'cmp_w1': cmp_w1[l], 'cmp_b1': cmp_b1[l], 'cmp_w2': cmp_w2[l],
              'cmp_b2': cmp_b2[l], 'ml_conv_w': ml_conv_w[l], 'ml_conv_b': ml_conv_b[l], 'ml_wq': ml_wq[l],
              'ml_wk': ml_wk[l], 'ml_gate_b': ml_gate_b[l], 'ml_norm_g': ml_norm_g[l], 'ml_skip': ml_skip[l],
              'w_out': w_out[l], 'norm2_g': norm2_g[l], 'w_up': w_up[l], 'w_down': w_down[l]}
        lw = _layer_weights(lp)
        x_p, st_p = _prompt_layer(x_p, lp, lw, bp, t_len)
        x_s, st_s = _sample_layer(x_s, lp, lw, l, ct, wt, state_gla[l],
                                  state_mlstm_c[l], state_mlstm_n[l], state_mlstm_m[l], state_mlstm_conv[l],
                                  page_table)
        states_p.append(st_p)
        states_s.append(st_s)
    outs = [x_p.reshape(bp, t_len, D_MODEL), x_s.reshape(n_dec, 1, D_MODEL)]
    for i in range(7):
        outs.append(jnp.stack([s[i] for s in states_p]))
        outs.append(jnp.stack([s[i] for s in states_s]))
    return tuple(outs)
```

```python
import functools

import numpy as np
import jax
import jax.numpy as jnp
from jax import lax
from jax.experimental import pallas as pl
from jax.experimental.pallas import tpu as pltpu

F32 = jnp.float32
BF16 = jnp.bfloat16

D_MODEL = 1024
HEAD_DIM = 64
PAGE_SIZE = 128
GLA_HEADS = 4
GLA_LOWRANK = 16
GLA_TAU = 16.0
NSA_HEADS = 8
NSA_KV_HEADS = 2
NSA_GROUP = NSA_HEADS // NSA_KV_HEADS
CMP_BLOCK = 32
CMP_STRIDE = 16
CMP_HIDDEN = 64
SEL_BLOCK = 64
SEL_RATIO = SEL_BLOCK // CMP_STRIDE
SEL_TOPK = 16
WINDOW = 512
FORCE_SCORE = 1.0e4
MLSTM_HEADS = 4
CONV_WIDTH = 4
D_FF = 4 * D_MODEL
EPS = 1e-6

GLA_W = GLA_HEADS * HEAD_DIM
NSA_W = NSA_HEADS * HEAD_DIM
KV_W = NSA_KV_HEADS * HEAD_DIM
MLSTM_W = MLSTM_HEADS * HEAD_DIM

LANES = 128
ZG_W = 4 * GLA_W + LANES
GATE_W = NSA_KV_HEADS * LANES
ZN_W = NSA_W + 6 * KV_W + GATE_W
ZM_W = 3 * MLSTM_W + LANES
ROWS_W = 4 * KV_W
WIN_W = 2 * KV_W
QPAD_W = NSA_HEADS * LANES

SCAN_CHUNK = 128
SCAN_CHUNK_DECODE = 32
GLA_SUB = 32
NSA_TQ = 128
NSA_TK = 256
NSA_TK_FIXED = 512
NSA_MAX_FIXED_SHIFT = 40.0
NSA_RANK_SIZES = 4
NEG = -1.0e30
SEL_BIAS = 29952.0

VMEM_LIMIT = 56 * 1024 * 1024


def _dot(a, b):
    return jnp.dot(a, b, preferred_element_type=F32)


def _dot_hi(a, b):
    return jnp.dot(a, b, preferred_element_type=F32, precision=lax.Precision.HIGHEST)


def _dot_nt(a, b):
    return lax.dot_general(a, b, (((1,), (1,)), ((), ())), preferred_element_type=F32)


def _dot_nt_hi(a, b):
    return lax.dot_general(a, b, (((1,), (1,)), ((), ())), preferred_element_type=F32,
                           precision=lax.Precision.HIGHEST)


def _dot_tn_hi(a, b):
    return lax.dot_general(a, b, (((0,), (0,)), ((), ())), preferred_element_type=F32,
                           precision=lax.Precision.HIGHEST)


def _sigmoid(x):
    return 1.0 / (1.0 + jnp.exp(-x))


def _log_sigmoid(x):
    return jnp.minimum(x, 0.0) - jnp.log(1.0 + jnp.exp(-jnp.abs(x)))


def _group_sum(x, ones_bd):
    hi = x.astype(BF16)
    lo = (x - hi.astype(F32)).astype(BF16)
    return _dot(hi, ones_bd) + _dot(lo, ones_bd)


def _group_mean_sq(x, ones_bd):
    return _group_sum(x * x, ones_bd) * (1.0 / HEAD_DIM)


def _params(*sem):
    return pltpu.CompilerParams(dimension_semantics=sem, vmem_limit_bytes=VMEM_LIMIT)


def _const_spec(shape):
    nd = len(shape)
    return pl.BlockSpec(shape, lambda *_: (0,) * nd)


def _inproj_kernel(x_ref, g_ref, w_ref, zg_ref, zn_ref, zm_ref):
    x = x_ref[...]
    h = x * lax.rsqrt(jnp.mean(x * x, axis=-1, keepdims=True) + EPS) * g_ref[...]
    hb = h.astype(BF16)
    zg_ref[...] = _dot(hb, w_ref[:, 0:ZG_W])
    zn_ref[...] = _dot(hb, w_ref[:, ZG_W:ZG_W + ZN_W])
    zm_ref[...] = _dot(hb, w_ref[:, ZG_W + ZN_W:ZG_W + ZN_W + ZM_W])


def _in_proj(x, g, w, tm):
    n = x.shape[0]
    zw = ZG_W + ZN_W + ZM_W
    return pl.pallas_call(
        _inproj_kernel,
        grid=(n // tm,),
        in_specs=[pl.BlockSpec((tm, D_MODEL), lambda i: (i, 0)),
                  _const_spec((1, D_MODEL)),
                  _const_spec((D_MODEL, zw))],
        out_specs=[pl.BlockSpec((tm, ZG_W), lambda i: (i, 0)),
                   pl.BlockSpec((tm, ZN_W), lambda i: (i, 0)),
                   pl.BlockSpec((tm, ZM_W), lambda i: (i, 0))],
        out_shape=[jax.ShapeDtypeStruct((n, ZG_W), F32),
                   jax.ShapeDtypeStruct((n, ZN_W), F32),
                   jax.ShapeDtypeStruct((n, ZM_W), F32)],
        compiler_params=_params("parallel"),
        name="in_proj",
    )(x, g, w)


def _pack_w_in(w_in):
    def cols(a, b):
        return w_in[:, a:b]

    def zeros(n):
        return jnp.zeros((D_MODEL, n), w_in.dtype)
    o_nsa = 4 * GLA_W + GLA_LOWRANK
    o_ng = o_nsa + NSA_W + 6 * KV_W
    o_ml = o_ng + 3 * NSA_HEADS
    o_mi = o_ml + 2 * MLSTM_W
    o_mo = o_mi + 2 * MLSTM_HEADS
    n_gate = 3 * NSA_GROUP
    parts = [cols(0, o_nsa), zeros(LANES - GLA_LOWRANK),
             cols(o_nsa, o_ng),
             cols(o_ng, o_ng + n_gate), zeros(LANES - n_gate),
             cols(o_ng + n_gate, o_ml), zeros(LANES - n_gate),
             cols(o_ml, o_mi), cols(o_mo, o_mo + MLSTM_W), cols(o_mi, o_mo),
             zeros(LANES - 2 * MLSTM_HEADS)]
    return jnp.concatenate(parts, axis=1).astype(BF16)


def _nsa_prep_kernel(zn_ref, gq_ref, gr_ref, gw_ref, gb_ref, ones_ref, place_ref,
                     qpad_ref, rows_ref, win_ref, kv_ref, r01_ref, gate_ref, *, transposed):
    ones_bd = ones_ref[...]
    q = zn_ref[:, 0:NSA_W]
    qn = q * lax.rsqrt(_group_mean_sq(q, ones_bd) + EPS) * gq_ref[...]
    qpad_ref[...] = _dot(qn.astype(BF16), place_ref[...]).astype(BF16)

    r = zn_ref[:, NSA_W:NSA_W + ROWS_W]
    col = lax.broadcasted_iota(jnp.int32, r.shape, 1)
    rn = r * lax.rsqrt(_group_mean_sq(r, ones_bd) + EPS) * gr_ref[...]
    rows = jnp.where((col >= 2 * KV_W) & (col < 3 * KV_W), rn, r)
    if transposed:
        rows_ref[0] = rows.T
    else:
        rows_ref[...] = rows
    r01_ref[...] = rows[:, 0:2 * KV_W].astype(BF16)

    w = zn_ref[:, NSA_W + ROWS_W:NSA_W + ROWS_W + WIN_W]
    colw = lax.broadcasted_iota(jnp.int32, w.shape, 1)
    wn = w * lax.rsqrt(_group_mean_sq(w, ones_bd[0:WIN_W, 0:WIN_W]) + EPS) * gw_ref[...]
    win = jnp.where(colw < KV_W, wn, w)
    if transposed:
        win_ref[0] = win.T
    else:
        win_ref[...] = win
    kv_ref[:, 0:2 * KV_W] = rows[:, 2 * KV_W:4 * KV_W].astype(BF16)
    kv_ref[:, 2 * KV_W:4 * KV_W] = win.astype(BF16)

    gate_ref[...] = _sigmoid(zn_ref[:, NSA_W + ROWS_W + WIN_W:ZN_W] + gb_ref[...])


def _nsa_prep(zn, qk_g, gate_b, tm, seq=None):
    n = zn.shape[0]
    scale = HEAD_DIM ** -0.5
    gq = (jnp.tile(qk_g[0], NSA_HEADS) * scale).reshape(1, NSA_W)
    gr = jnp.tile(qk_g[2], ROWS_W // HEAD_DIM).reshape(1, ROWS_W)
    gw = jnp.tile(qk_g[3], WIN_W // HEAD_DIM).reshape(1, WIN_W)
    n_gate = 3 * NSA_GROUP
    gpad = jnp.zeros((LANES - n_gate,), F32)
    gb = jnp.concatenate([gate_b[0:n_gate], gpad, gate_b[n_gate:], gpad]).reshape(1, GATE_W)
    grp = np.arange(NSA_W) // HEAD_DIM
    ones_bd = jnp.asarray(grp[:, None] == grp[None, :], BF16)
    src = np.arange(NSA_W)
    head, d = src // HEAD_DIM, src % HEAD_DIM
    dst = head * LANES + (head // NSA_GROUP) * HEAD_DIM + d
    place = np.zeros((NSA_W, QPAD_W), np.float32)
    place[src, dst] = 1.0
    place = jnp.asarray(place, BF16)
    row = lambda w: pl.BlockSpec((tm, w), lambda i: (i, 0))
    if seq is None:
        state_spec = row
        state_shape = lambda w: jax.ShapeDtypeStruct((n, w), F32)
    else:
        batch, t_len = seq
        nblk = t_len // tm
        state_spec = lambda w: pl.BlockSpec((1, w, tm), lambda i: (i // nblk, 0, i % nblk))
        state_shape = lambda w: jax.ShapeDtypeStruct((batch, w, t_len), F32)
    return pl.pallas_call(
        functools.partial(_nsa_prep_kernel, transposed=seq is not None),
        grid=(n // tm,),
        in_specs=[row(ZN_W), _const_spec((1, NSA_W)), _const_spec((1, ROWS_W)),
                  _const_spec((1, WIN_W)), _const_spec((1, GATE_W)),
                  _const_spec((NSA_W, NSA_W)), _const_spec((NSA_W, QPAD_W))],
        out_specs=[row(QPAD_W), state_spec(ROWS_W), state_spec(WIN_W), row(4 * KV_W), row(2 * KV_W),
                   row(GATE_W)],
        out_shape=[jax.ShapeDtypeStruct((n, QPAD_W), BF16),
                   state_shape(ROWS_W),
                   state_shape(WIN_W),
                   jax.ShapeDtypeStruct((n, 4 * KV_W), BF16),
                   jax.ShapeDtypeStruct((n, 2 * KV_W), BF16),
                   jax.ShapeDtypeStruct((n, GATE_W), F32)],
        compiler_params=_params("parallel"),
        name="nsa_prep",
    )(zn, gq, gr, gw, gb, ones_bd, place)


def _load_block_diag(bd_scr, heads_ref):
    nh, d = heads_ref.shape[1], heads_ref.shape[2]
    bd_scr[...] = jnp.zeros(bd_scr.shape, F32)
    for h in range(nh):
        bd_scr[h * d:(h + 1) * d, h * d:(h + 1) * d] = heads_ref[0, h]


def _store_block_diag(heads_ref, bd_scr):
    nh, d = heads_ref.shape[1], heads_ref.shape[2]
    for h in range(nh):
        heads_ref[0, h] = bd_scr[h * d:(h + 1) * d, h * d:(h + 1) * d]


def _gla_kernel(zg_ref, s0_ref, aw_ref, ab_ref, ng_ref, tri_ref, ones_ref, og_ref, sout_ref, s_scr,
                *, t_valid, t_pad):
    i = pl.program_id(1)
    c = zg_ref.shape[0]
    d = HEAD_DIM

    @pl.when(i == 0)
    def _():
        _load_block_diag(s_scr, s0_ref)

    q = zg_ref[:, 0:GLA_W] * (d ** -0.5)
    k = zg_ref[:, GLA_W:2 * GLA_W]
    v = zg_ref[:, 2 * GLA_W:3 * GLA_W]
    r = zg_ref[:, 3 * GLA_W:4 * GLA_W]
    ga = zg_ref[:, 4 * GLA_W:ZG_W]
    g = _log_sigmoid(_dot(ga.astype(BF16), aw_ref[...]) + ab_ref[...]) * (1.0 / GLA_TAU)
    if t_valid < t_pad:
        valid = (i * c + lax.broadcasted_iota(jnp.int32, (c, 1), 0)) < t_valid
        g = jnp.where(valid, g, 0.0)
        k = jnp.where(valid, k, 0.0)
    bcum = _dot_hi(tri_ref[...], g)
    b_end = bcum[c - 1:c, :]
    nh, w, sub = GLA_HEADS, GLA_W, min(GLA_SUB, c)
    lane_head = lax.broadcasted_iota(jnp.int32, (1, w), 1) // d
    s_bd = s_scr[...]
    o = _dot((q * jnp.exp(bcum)).astype(BF16), s_bd.astype(BF16))
    vb = v.astype(BF16)
    o_sub = []
    for j in range(c // sub):
        lo, hi = j * sub, (j + 1) * sub
        base = bcum[lo - 1:lo, :] if j > 0 else jnp.zeros((1, w), F32)
        qt = q[lo:hi] * jnp.exp(bcum[lo:hi] - base)
        kt = (k[0:hi] * jnp.exp(base - bcum[0:hi])).astype(BF16)
        qs = jnp.concatenate([jnp.where(lane_head == h, qt, 0.0) for h in range(nh)], axis=0)
        sc = _dot_nt(qs.astype(BF16), kt)
        t_row = lo + (lax.broadcasted_iota(jnp.int32, (nh * sub, hi), 0) % sub)
        sc = jnp.where(lax.broadcasted_iota(jnp.int32, (nh * sub, hi), 1) <= t_row, sc, 0.0)
        ov = _dot(sc.astype(BF16), vb[0:hi])
        o_j = jnp.where(lane_head == 0, ov[0:sub], 0.0)
        for h in range(1, nh):
            o_j = jnp.where(lane_head == h, ov[h * sub:(h + 1) * sub], o_j)
        o_sub.append(o_j)
    o = o + jnp.concatenate(o_sub, axis=0)
    last = (lax.broadcasted_iota(jnp.int32, (c, LANES), 0) == c - 1).astype(F32)
    decay_col = jnp.exp(_dot_tn_hi(bcum, last))
    k_hat = k * jnp.exp(b_end - bcum)
    same_head = (lax.broadcasted_iota(jnp.int32, (w, w), 0) // d) == (lax.broadcasted_iota(jnp.int32, (w, w), 1) // d)
    s_scr[...] = (jnp.concatenate([decay_col] * (w // LANES), axis=1) * s_bd
                  + jnp.where(same_head, _dot_tn_hi(k_hat, v), 0.0))
    on = o * lax.rsqrt(_group_mean_sq(o, ones_ref[...]) + EPS) * ng_ref[...]
    og_ref[...] = (on * (r * _sigmoid(r))).astype(BF16)

    @pl.when(i == pl.num_programs(1) - 1)
    def _():
        _store_block_diag(sout_ref, s_scr)


def _gla(zg, s0, a_w, a_b, norm_g, batch, t_pad, t_valid, c=SCAN_CHUNK):
    nblk = t_pad // c
    aw = jnp.concatenate([a_w, jnp.zeros((LANES - GLA_LOWRANK, GLA_W), F32)], axis=0).astype(BF16)
    tri = jnp.asarray(np.tril(np.ones((c, c), np.float32)))
    grp = np.arange(GLA_W) // HEAD_DIM
    ones_bd = jnp.asarray(grp[:, None] == grp[None, :], BF16)
    kern = functools.partial(_gla_kernel, t_valid=t_valid, t_pad=t_pad)
    state = pl.BlockSpec((1, GLA_HEADS, HEAD_DIM, HEAD_DIM), lambda b, i: (b, 0, 0, 0))
    return pl.pallas_call(
        kern,
        grid=(batch, nblk),
        in_specs=[pl.BlockSpec((c, ZG_W), lambda b, i: (b * nblk + i, 0)), state,
                  _const_spec((LANES, GLA_W)), _const_spec((1, GLA_W)), _const_spec((1, GLA_W)),
                  _const_spec((c, c)), _const_spec((GLA_W, GLA_W))],
        out_specs=[pl.BlockSpec((c, GLA_W), lambda b, i: (b * nblk + i, 0)), state],
        out_shape=[jax.ShapeDtypeStruct((batch * t_pad, GLA_W), BF16),
                   jax.ShapeDtypeStruct((batch, GLA_HEADS, HEAD_DIM, HEAD_DIM), F32)],
        scratch_shapes=[pltpu.VMEM((GLA_W, GLA_W), F32)],
        compiler_params=_params("parallel", "arbitrary"),
        name="gla_scan",
    )(zg, s0, aw, a_b.reshape(1, GLA_W), jnp.tile(norm_g, GLA_HEADS).reshape(1, GLA_W), tri, ones_bd)


def _mlstm_kernel(zm_ref, conv0_ref, c0_ref, n0_ref, m0_ref, cw_ref, cb_ref, wq_ref, wk_ref, gb_ref,
                  ng_ref, skip_ref, tri_ref, ones_ref, om_ref, cout_ref, nout_ref, mout_ref,
                  c_scr, n_scr, m_scr, ext_scr, *, t_valid, t_pad):
    i = pl.program_id(1)
    c = zm_ref.shape[0]
    d = HEAD_DIM
    nh = MLSTM_HEADS

    @pl.when(i == 0)
    def _():
        _load_block_diag(c_scr, c0_ref)
        n_scr[...] = n0_ref[0]
        m_scr[...] = m0_ref[0]
        ext_scr[0:8, :] = conv0_ref[0]

    mu = zm_ref[:, 0:MLSTM_W]
    ext_scr[8:8 + c, :] = mu
    u_conv = cb_ref[...] + mu * cw_ref[CONV_WIDTH - 1:CONV_WIDTH, :]
    for j in range(1, CONV_WIDTH):
        u_conv = u_conv + ext_scr[8 - j:8 - j + c, :] * cw_ref[CONV_WIDTH - 1 - j:CONV_WIDTH - j, :]
    ext_scr[0:8, :] = mu[c - 8:c, :]
    u_act = u_conv * _sigmoid(u_conv)
    ub = u_act.astype(BF16)
    q = _dot(ub, wq_ref[...])
    k = _dot(ub, wk_ref[...]) * (d ** -0.5)
    v = zm_ref[:, MLSTM_W:2 * MLSTM_W]
    og = zm_ref[:, 2 * MLSTM_W:3 * MLSTM_W]
    gz = zm_ref[:, 3 * MLSTM_W:ZM_W] + gb_ref[...]
    lane = lax.broadcasted_iota(jnp.int32, (c, LANES), 1)
    x = jnp.where(lane < nh, gz, _log_sigmoid(gz))
    if t_valid < t_pad:
        valid = (i * c + lax.broadcasted_iota(jnp.int32, (c, 1), 0)) < t_valid
        x = jnp.where(valid, x, jnp.where(lane < nh, NEG, 0.0))
    fc = _dot_hi(tri_ref[...], x)
    x = jnp.where(lane < nh, x, fc)
    sel = (lax.broadcasted_iota(jnp.int32, (8, LANES), 0)
           == lax.broadcasted_iota(jnp.int32, (8, LANES), 1)).astype(F32)
    xt = _dot_nt_hi(sel, x)
    row_i = lax.broadcasted_iota(jnp.int32, (c, c), 0)
    col_i = lax.broadcasted_iota(jnp.int32, (c, c), 1)
    w = MLSTM_W
    lane_head = lax.broadcasted_iota(jnp.int32, (1, w), 1) // d
    c_bd = c_scr[...]
    n_all = n_scr[0:1, :]
    m_all = m_scr[0:1, :]
    qb = q.astype(BF16)
    qs = jnp.concatenate([jnp.where(lane_head == h, q, 0.0) for h in range(nh)], axis=0)
    qk_all = _dot_nt(qs.astype(BF16), k.astype(BF16))
    qn_sum = _group_sum(q * n_all, ones_ref[...])
    zc, zr = jnp.zeros((c, w), F32), jnp.zeros((1, w), F32)
    mt_l, ws_l, rs_l, wl_l, wc_l, mn_l = zc, zc, zc, zc, zr, zr
    p_parts = []
    for h in range(nh):
        i_col, f_col = x[:, h:h + 1], x[:, nh + h:nh + h + 1]
        i_row, f_row = xt[h:h + 1, :], xt[nh + h:nh + h + 1, :]
        a = f_col + m_all[:, h * d:h * d + 1]
        dmat = jnp.where(col_i <= row_i, f_col - f_row + i_row, NEG)
        m_t = jnp.maximum(a, jnp.max(dmat, axis=-1, keepdims=True))
        p_h = qk_all[h * c:(h + 1) * c] * jnp.exp(dmat - m_t)
        p_parts.append(p_h)
        m_new = m_t[c - 1:c, :]
        on_head = lane_head == h
        mt_l = jnp.where(on_head, m_t, mt_l)
        ws_l = jnp.where(on_head, jnp.exp(a - m_t), ws_l)
        rs_l = jnp.where(on_head, jnp.sum(p_h, axis=-1, keepdims=True), rs_l)
        wl_l = jnp.where(on_head, jnp.exp(f_col[c - 1:c, :] - f_col + i_col - m_new), wl_l)
        wc_l = jnp.where(on_head, jnp.exp(a[c - 1:c, :] - m_new), wc_l)
        mn_l = jnp.where(on_head, m_new, mn_l)
    nv = _dot(jnp.concatenate(p_parts, axis=0).astype(BF16), v.astype(BF16))
    num = jnp.where(lane_head == 0, nv[0:c], 0.0)
    for h in range(1, nh):
        num = jnp.where(lane_head == h, nv[h * c:(h + 1) * c], num)
    num = num + ws_l * _dot(qb, c_bd.astype(BF16))
    den = rs_l + ws_l * qn_sum
    hm = num / jnp.maximum(jnp.abs(den), jnp.exp(-mt_l))
    kw = k * wl_l
    same_head = (lax.broadcasted_iota(jnp.int32, (w, w), 0) // d) == (lax.broadcasted_iota(jnp.int32, (w, w), 1) // d)
    c_scr[...] = wc_l * c_bd + jnp.where(same_head, _dot_tn_hi(kw, v), 0.0)
    n_scr[0:1, :] = wc_l * n_all + jnp.sum(kw, axis=0, keepdims=True)
    m_scr[0:1, :] = mn_l
    hn = hm * lax.rsqrt(_group_mean_sq(hm, ones_ref[...]) + EPS) * ng_ref[...]
    om_ref[...] = (_sigmoid(og) * (hn + skip_ref[...] * u_act)).astype(BF16)

    @pl.when(i == pl.num_programs(1) - 1)
    def _():
        _store_block_diag(cout_ref, c_scr)
        nout_ref[0] = n_scr[...]
        mout_ref[0] = m_scr[...]


def _block_diag_heads(w):
    nh, d, _ = w.shape
    eye = jnp.eye(nh, dtype=w.dtype)
    return jnp.einsum('hde,hg->hdge', w, eye).reshape(nh * d, nh * d)


def _mlstm(zm, conv0, c0, n0, m0, lp, batch, t_pad, t_valid, c=SCAN_CHUNK):
    nblk = t_pad // c
    nh, d = MLSTM_HEADS, HEAD_DIM
    conv0p = jnp.concatenate([jnp.zeros((batch, 8 - (CONV_WIDTH - 1), MLSTM_W), F32), conv0.astype(F32)], axis=1)
    pad7 = jnp.zeros((batch, 7, MLSTM_W), F32)
    n0p = jnp.concatenate([n0.reshape(batch, 1, MLSTM_W), pad7], axis=1)
    m0p = jnp.concatenate([jnp.repeat(m0, d, axis=1).reshape(batch, 1, MLSTM_W), pad7], axis=1)
    gb = jnp.concatenate([lp['ml_gate_b'][0], lp['ml_gate_b'][1],
                          jnp.zeros((LANES - 2 * nh,), F32)]).reshape(1, LANES)
    tri = jnp.asarray(np.tril(np.ones((c, c), np.float32)))
    grp = np.arange(MLSTM_W) // d
    ones_bd = jnp.asarray(grp[:, None] == grp[None, :], BF16)
    kern = functools.partial(_mlstm_kernel, t_valid=t_valid, t_pad=t_pad)
    per_b = lambda shape: pl.BlockSpec((1,) + shape, lambda b, i: (b,) + (0,) * len(shape))
    om, c_f, n_f, m_f = pl.pallas_call(
        kern,
        grid=(batch, nblk),
        in_specs=[pl.BlockSpec((c, ZM_W), lambda b, i: (b * nblk + i, 0)),
                  per_b((8, MLSTM_W)), per_b((nh, d, d)), per_b((8, MLSTM_W)), per_b((8, MLSTM_W)),
                  _const_spec((CONV_WIDTH, MLSTM_W)), _const_spec((1, MLSTM_W)),
                  _const_spec((MLSTM_W, MLSTM_W)), _const_spec((MLSTM_W, MLSTM_W)),
                  _const_spec((1, LANES)), _const_spec((1, MLSTM_W)), _const_spec((1, MLSTM_W)),
                  _const_spec((c, c)), _const_spec((MLSTM_W, MLSTM_W))],
        out_specs=[pl.BlockSpec((c, MLSTM_W), lambda b, i: (b * nblk + i, 0)),
                   per_b((nh, d, d)), per_b((8, MLSTM_W)), per_b((8, MLSTM_W))],
        out_shape=[jax.ShapeDtypeStruct((batch * t_pad, MLSTM_W), BF16),
                   jax.ShapeDtypeStruct((batch, nh, d, d), F32),
                   jax.ShapeDtypeStruct((batch, 8, MLSTM_W), F32),
                   jax.ShapeDtypeStruct((batch, 8, MLSTM_W), F32)],
        scratch_shapes=[pltpu.VMEM((MLSTM_W, MLSTM_W), F32), pltpu.VMEM((8, MLSTM_W), F32),
                        pltpu.VMEM((8, MLSTM_W), F32), pltpu.VMEM((8 + c, MLSTM_W), F32)],
        compiler_params=_params("parallel", "arbitrary"),
        name="mlstm_scan",
    )(zm, conv0p, c0, n0p, m0p, lp['ml_conv_w'], lp['ml_conv_b'].reshape(1, MLSTM_W),
      _block_diag_heads(lp['ml_wq']).astype(BF16), _block_diag_heads(lp['ml_wk']).astype(BF16),
      gb, jnp.tile(lp['ml_norm_g'], nh).reshape(1, MLSTM_W), lp['ml_skip'].reshape(1, MLSTM_W),
      tri, ones_bd)
    return om, c_f, n_f[:, 0].reshape(batch, nh, d), m_f[:, 0].reshape(batch, nh, d)[:, :, 0]


CMP_IN = CMP_STRIDE * 2 * KV_W
CMP_G = 2 * 2 * KV_W


def _cmp_weights(lp):
    w1 = lp['cmp_w1'].reshape(2, 2, CMP_STRIDE, HEAD_DIM, CMP_HIDDEN)
    eye = jnp.eye(2, dtype=F32)
    w_ab = jnp.einsum('sarde,st,hg->rshdatge', w1, eye, eye).reshape(CMP_IN, CMP_G)
    pe = lp['cmp_pe'].reshape(2, 2, CMP_STRIDE, 1, HEAD_DIM)
    pe = jnp.broadcast_to(jnp.transpose(pe, (1, 2, 0, 3, 4)),
                          (2, CMP_STRIDE, 2, NSA_KV_HEADS, HEAD_DIM)).reshape(2, CMP_IN)
    pe8 = jnp.concatenate([pe, jnp.zeros((6, CMP_IN), F32)], axis=0)
    b1 = jnp.broadcast_to(lp['cmp_b1'][:, None, :], (2, NSA_KV_HEADS, CMP_HIDDEN)).reshape(1, 2 * KV_W)
    w2 = jnp.einsum('sed,st,hg->shetgd', lp['cmp_w2'], eye, eye).reshape(2 * KV_W, 2 * KV_W)
    b2 = jnp.broadcast_to(lp['cmp_b2'][:, None, :], (2, NSA_KV_HEADS, HEAD_DIM)).reshape(1, 2 * KV_W)
    g1 = jnp.tile(lp['nsa_qk_g'][1], NSA_KV_HEADS).reshape(1, KV_W)
    grp = np.arange(KV_W) // HEAD_DIM
    ones_bd = jnp.asarray(grp[:, None] == grp[None, :], BF16)
    return (w_ab.astype(BF16), pe8.astype(BF16), b1, w2.astype(BF16), b2, g1, ones_bd)


def _cmp_finish(gsum, n_rows, wab_ref, pe_ref, b1_ref, w2_ref, b2_ref, g1_ref, ones_ref):
    half = 2 * KV_W
    g_pe = _dot(pe_ref[...], wab_ref[...])
    bias = g_pe[0:1, 0:half] + g_pe[1:2, half:CMP_G] + b1_ref[...]
    hid = gsum[:, 0:half] + pltpu.roll(gsum[:, half:CMP_G], n_rows - 1, 0) + bias
    act = hid * _sigmoid(hid)
    cmp = _dot(act.astype(BF16), w2_ref[...]) + b2_ref[...]
    kc = cmp[:, 0:KV_W]
    kc = kc * lax.rsqrt(_group_mean_sq(kc, ones_ref[...]) + EPS) * g1_ref[...]
    return kc, cmp[:, KV_W:half]


def _cmp_kernel(r_ref, wab_ref, pe_ref, b1_ref, w2_ref, b2_ref, g1_ref, ones_ref, kc_ref, vc_ref, *, n_rows):
    gsum = _dot(r_ref[0], wab_ref[...])
    kc, vc = _cmp_finish(gsum, n_rows, wab_ref, pe_ref, b1_ref, w2_ref, b2_ref, g1_ref, ones_ref)
    kc_ref[0] = kc.astype(BF16)
    vc_ref[0] = vc.astype(BF16)


def _cmp_specs():
    half = 2 * KV_W
    return [_const_spec((CMP_IN, CMP_G)), _const_spec((8, CMP_IN)), _const_spec((1, half)),
            _const_spec((half, half)), _const_spec((1, half)), _const_spec((1, KV_W)),
            _const_spec((KV_W, KV_W))]


def _compress(r01, cw, batch, t_len):
    n16 = t_len // CMP_STRIDE
    x = r01.reshape(batch, n16, CMP_IN)
    blk = lambda w: pl.BlockSpec((1, n16, w), lambda b: (b, 0, 0))
    return pl.pallas_call(
        functools.partial(_cmp_kernel, n_rows=n16),
        grid=(batch,),
        in_specs=[blk(CMP_IN)] + _cmp_specs(),
        out_specs=[blk(KV_W), blk(KV_W)],
        out_shape=[jax.ShapeDtypeStruct((batch, n16, KV_W), BF16)] * 2,
        compiler_params=_params("parallel"),
        name="nsa_compress",
    )(x, *cw)


def _nsa_kernel(qpad_ref, ksel_ref, vsel_ref, kwin_ref, vwin_ref, kc_ref, vc_ref, gate_ref, mt_ref,
                et_ref, pl_ref, bound_ref, out_ref, sc_scr, lhs_scr, m_scr, l_scr, acc_scr, *, nb, fixed_max):
    qi = pl.program_id(2)
    tq, grp = NSA_TQ, NSA_GROUP
    tk = NSA_TK_FIXED if fixed_max else NSA_TK
    rows = grp * tq
    nbp, ncp = mt_ref.shape
    start = qi * tq
    q4 = jnp.concatenate([qpad_ref[:, g * LANES:(g + 1) * LANES] for g in range(grp)], axis=0)
    tpos = start + (lax.broadcasted_iota(jnp.int32, (rows, 1), 0) & (tq - 1))

    sc = _dot_nt(q4, kc_ref[0])
    ccol = lax.broadcasted_iota(jnp.int32, (rows, ncp), 1)
    vis = (ccol * CMP_STRIDE + (CMP_BLOCK - 1)) <= tpos
    s_m = jnp.where(vis, sc, NEG)
    e = jnp.where(vis, jnp.exp(s_m - jnp.max(s_m, axis=-1, keepdims=True)), 0.0)
    p_c = e / jnp.maximum(jnp.sum(e, axis=-1, keepdims=True), 1e-30)
    o_c = _dot(p_c.astype(BF16), vc_ref[0])

    pg = p_c[0:tq]
    for g in range(1, grp):
        pg = pg + p_c[g * tq:(g + 1) * tq]
    h1 = pg.astype(BF16)
    r1 = pg - h1.astype(F32)
    h2 = r1.astype(BF16)
    h3 = (r1 - h2.astype(F32)).astype(BF16)
    mt = mt_ref[...]
    p_sel = _dot_nt(mt, h1) + _dot_nt(mt, h2) + _dot_nt(mt, h3)
    blk = lax.broadcasted_iota(jnp.int32, (nbp, tq), 0)
    cur = (start + lax.broadcasted_iota(jnp.int32, (nbp, tq), 1)) // SEL_BLOCK
    forced = (blk == 0) | (blk == cur) | (blk == cur - 1)
    score = jnp.where(blk > cur, -1.0, jnp.where(forced, FORCE_SCORE, p_sel))
    if nb < nbp:
        score = jnp.where(blk >= nb, -2.0, score)
    sc_scr[...] = score

    s_grp = [score[8 * v:8 * v + 8, :] for v in range(nbp // 8)]
    sub = lax.broadcasted_iota(jnp.int32, (8, tq), 0)
    cur_max = (start + tq - 1) // SEL_BLOCK
    sizes = [nb * (b + 1) // NSA_RANK_SIZES for b in range(NSA_RANK_SIZES)]
    for b, n_used in enumerate(sizes):
        n_prev = sizes[b - 1] if b > 0 else 0

        @pl.when((cur_max >= n_prev) & (cur_max < n_used))
        def _():
            n_grp = n_used // 8
            cnt = [jnp.zeros((8, tq), F32) for _ in range(n_grp)]
            for i in range(n_used):
                row = sc_scr[i:i + 1, :]
                vi, ri = divmod(i, 8)
                for v in range(n_grp):
                    if v > vi:
                        hit = jnp.where(row >= s_grp[v], 1.0, 0.0)
                    elif v < vi:
                        hit = jnp.where(row > s_grp[v], 1.0, 0.0)
                    else:
                        hit = jnp.where(sub > ri, jnp.where(row >= s_grp[v], 1.0, 0.0),
                                        jnp.where(row > s_grp[v], 1.0, 0.0))
                    cnt[v] = cnt[v] + hit
            parts = [jnp.where(jnp.concatenate(cnt, axis=0) < SEL_TOPK, 0.0, -1.0)]
            if n_used < nbp:
                parts.append(jnp.full((nbp - n_used, tq), -1.0, F32))
            sc_scr[...] = jnp.concatenate(parts, axis=0)
    unsel_t = sc_scr[...].T.astype(BF16)
    lhs_scr[...] = jnp.concatenate([q4, jnp.concatenate([unsel_t] * grp, axis=0)], axis=1)

    l_scr[...] = jnp.zeros((rows, LANES), F32)
    acc_scr[...] = jnp.zeros((rows, LANES), F32)
    n_full = start // tk

    def key_operands(kt):
        koff = pl.multiple_of(kt * tk, tk)
        kk = jnp.concatenate([ksel_ref[pl.ds(koff, tk), :], et_ref[pl.ds(koff, tk), :]], axis=1)
        return koff, kk, vsel_ref[pl.ds(koff, tk), :]

    if fixed_max:
        bound = bound_ref[0, 0]

        def key_tile(kt, masked):
            koff, kk, vv = key_operands(kt)
            s = _dot_nt(lhs_scr[...], kk)
            if masked:
                s = jnp.where(koff + lax.broadcasted_iota(jnp.int32, (rows, tk), 1) <= tpos, s, NEG)
            p = jnp.exp(s - bound)
            part = p[:, 0:LANES]
            for c in range(1, tk // LANES):
                part = part + p[:, c * LANES:(c + 1) * LANES]
            l_scr[...] = l_scr[...] + part
            acc_scr[...] = acc_scr[...] + _dot(p.astype(BF16), vv)
    else:
        m_scr[...] = jnp.full((rows, LANES), NEG, F32)
        qpos = start + lax.broadcasted_iota(jnp.int32, (tq, tk), 0)

        def key_tile(kt, masked):
            koff, kk, vv = key_operands(kt)
            for g in range(grp):
                rs = slice(g * tq, (g + 1) * tq)
                s = _dot_nt(lhs_scr[rs, :], kk)
                if masked:
                    s = jnp.where(koff + lax.broadcasted_iota(jnp.int32, (tq, tk), 1) <= qpos, s, NEG)
                m_old = m_scr[rs, :]
                m_new = jnp.maximum(m_old, jnp.max(s, axis=-1, keepdims=True))
                p = jnp.exp(s - jnp.concatenate([m_new] * (tk // LANES), axis=1))
                alpha = jnp.exp(m_old - m_new)
                l_scr[rs, :] = alpha * l_scr[rs, :] + jnp.sum(p, axis=-1, keepdims=True)
                acc_scr[rs, :] = alpha * acc_scr[rs, :] + _dot(p.astype(BF16), vv)
                m_scr[rs, :] = m_new

    def tile_pair(kp, carry):
        key_tile(2 * kp, False)
        key_tile(2 * kp + 1, False)
        return carry
    lax.fori_loop(0, n_full // 2, tile_pair, 0)

    @pl.when(n_full % 2 == 1)
    def _():
        key_tile(n_full - 1, False)
    key_tile(n_full, True)
    if fixed_max:
        o_s = acc_scr[...] / jnp.sum(l_scr[...], axis=-1, keepdims=True)
    else:
        o_s = acc_scr[...] / l_scr[...]

    wk = WINDOW + tq
    wstart = pl.multiple_of(jnp.maximum(start - WINDOW, 0), tq)
    s = _dot_nt(q4, kwin_ref[pl.ds(wstart, wk), :])
    wpos = wstart + lax.broadcasted_iota(jnp.int32, (rows, wk), 1)
    s = jnp.where((wpos <= tpos) & (wpos > tpos - WINDOW), s, NEG)
    e = jnp.exp(s - jnp.max(s, axis=-1, keepdims=True))
    o_w = _dot(e.astype(BF16), vwin_ref[pl.ds(wstart, wk), :]) / jnp.sum(e, axis=-1, keepdims=True)

    gt = gate_ref[...]

    def gate(branch):
        return jnp.concatenate([gt[0:tq, g * 3 + branch:g * 3 + branch + 1] for g in range(grp)], axis=0)
    o = (gate(0) * o_c + gate(1) * o_s + gate(2) * o_w).astype(BF16)
    out = _dot(o[0:tq], pl_ref[0, 0])
    for g in range(1, grp):
        out = out + _dot(o[g * tq:(g + 1) * tq], pl_ref[0, g])
    out_ref[...] = out.astype(BF16)


def _nsa_constants(t_len):
    nb = t_len // SEL_BLOCK
    nbp = max(LANES, -(-nb // LANES) * LANES)
    ncp = t_len // CMP_STRIDE
    j = np.arange(nbp)[:, None]
    n = np.arange(ncp)[None, :]
    first = (n >= SEL_RATIO * j) & (n <= SEL_RATIO * j + SEL_RATIO - 1)
    second = (n >= SEL_RATIO * j - 1) & (n <= SEL_RATIO * j + SEL_RATIO - 2)
    mt = (first.astype(np.float32) + second.astype(np.float32)) * (n < ncp - 1) * (j < nb)
    et = (np.arange(t_len)[:, None] // SEL_BLOCK == np.arange(nbp)[None, :]).astype(np.float32) * SEL_BIAS
    place = np.zeros((NSA_KV_HEADS, NSA_GROUP, LANES, NSA_GROUP * HEAD_DIM), np.float32)
    for h in range(NSA_KV_HEADS):
        for g in range(NSA_GROUP):
            place[h, g, h * HEAD_DIM + np.arange(HEAD_DIM), g * HEAD_DIM + np.arange(HEAD_DIM)] = 1.0
    return nb, jnp.asarray(mt, BF16), jnp.asarray(et, BF16), jnp.asarray(place, BF16)


def _nsa_attend(qpad, kv, kc, vc, gates, qk_g, batch, t_len):
    tq = NSA_TQ
    nq = t_len // tq
    rows = NSA_GROUP * tq
    nb, mt, et, place = _nsa_constants(t_len)
    nbp, ncp = mt.shape
    seq = lambda c: pl.BlockSpec((t_len, KV_W), lambda b, h, i: (b, c))
    bound = (HEAD_DIM ** 0.5) * jnp.max(jnp.abs(qk_g[0])) * jnp.max(jnp.abs(qk_g[2])) * 1.02 + 0.1

    def attend(fixed_max):
        return pl.pallas_call(
            functools.partial(_nsa_kernel, nb=nb, fixed_max=fixed_max),
            grid=(batch, NSA_KV_HEADS, nq),
            in_specs=[pl.BlockSpec((tq, NSA_GROUP * LANES), lambda b, h, i: (b * nq + i, h)),
                      seq(0), seq(1), seq(2), seq(3),
                      pl.BlockSpec((1, ncp, KV_W), lambda b, h, i: (b, 0, 0)),
                      pl.BlockSpec((1, ncp, KV_W), lambda b, h, i: (b, 0, 0)),
                      pl.BlockSpec((tq, LANES), lambda b, h, i: (b * nq + i, h)),
                      _const_spec((nbp, ncp)), _const_spec((t_len, nbp)),
                      pl.BlockSpec((1, NSA_GROUP, LANES, NSA_GROUP * HEAD_DIM), lambda b, h, i: (h, 0, 0, 0)),
                      pl.BlockSpec(memory_space=pltpu.SMEM)],
            out_specs=pl.BlockSpec((tq, NSA_GROUP * HEAD_DIM), lambda b, h, i: (b * nq + i, h)),
            out_shape=jax.ShapeDtypeStruct((batch * t_len, NSA_W), BF16),
            scratch_shapes=[pltpu.VMEM((nbp, tq), F32), pltpu.VMEM((rows, LANES + nbp), BF16),
                            pltpu.VMEM((rows, LANES), F32), pltpu.VMEM((rows, LANES), F32),
                            pltpu.VMEM((rows, LANES), F32)],
            compiler_params=_params("parallel", "parallel", "arbitrary"),
            name="nsa_attend",
        )(qpad, kv, kv, kv, kv, kc, vc, gates, mt, et, place, bound.reshape(1, 1))
    return lax.cond(bound <= NSA_MAX_FIXED_SHIFT, lambda: attend(True), lambda: attend(False))


def _outffn_kernel(x_ref, og_ref, on_ref, om_ref, wo_ref, g2_ref, wu_ref, wd_ref, y_ref):
    x1 = (x_ref[...] + _dot(og_ref[...], wo_ref[0:GLA_W, :])
          + _dot(on_ref[...], wo_ref[GLA_W:GLA_W + NSA_W, :])
          + _dot(om_ref[...], wo_ref[GLA_W + NSA_W:GLA_W + NSA_W + MLSTM_W, :]))
    h = x1 * lax.rsqrt(jnp.mean(x1 * x1, axis=-1, keepdims=True) + EPS) * g2_ref[...]
    hid = jnp.maximum(_dot(h.astype(BF16), wu_ref[...]), 0.0)
    y_ref[...] = x1 + _dot((hid * hid).astype(BF16), wd_ref[...])


def _out_ffn(x, og, on, om, wo, g2, wu, wd, tm):
    n = x.shape[0]
    row = lambda w: pl.BlockSpec((tm, w), lambda i: (i, 0))
    return pl.pallas_call(
        _outffn_kernel,
        grid=(n // tm,),
        in_specs=[row(D_MODEL), row(GLA_W), row(NSA_W), row(MLSTM_W),
                  _const_spec((D_MODEL, D_MODEL)), _const_spec((1, D_MODEL)),
                  _const_spec((D_MODEL, D_FF)), _const_spec((D_FF, D_MODEL))],
        out_specs=row(D_MODEL),
        out_shape=jax.ShapeDtypeStruct((n, D_MODEL), F32),
        compiler_params=_params("parallel"),
        name="out_ffn",
    )(x, og, on, om, wo, g2, wu, wd)


def _layer_weights(lp):
    return {'w_in': _pack_w_in(lp['w_in']), 'w_out': lp['w_out'].astype(BF16),
            'w_up': lp['w_up'].astype(BF16), 'w_down': lp['w_down'].astype(BF16),
            'cmp': _cmp_weights(lp)}


def _rows_on_lanes_to_state(a, n_slots):
    batch, _, n_rows = a.shape
    return jnp.transpose(a.reshape(batch, n_slots, NSA_KV_HEADS, HEAD_DIM, n_rows), (0, 4, 1, 2, 3))


def _prompt_layer(x, lp, lw, batch, t_len):
    d = HEAD_DIM
    zg, zn, zm = _in_proj(x, lp['norm1_g'].reshape(1, D_MODEL), lw['w_in'], 256)
    qpad, rows_t, win_t, kv, r01, gates = _nsa_prep(zn, lp['nsa_qk_g'], lp['nsa_gate_b'], 256,
                                                    seq=(batch, t_len))
    zero = lambda *s: jnp.zeros(s, F32)
    og, s_gla = _gla(zg, zero(batch, GLA_HEADS, d, d), lp['gla_a_w'], lp['gla_a_b'], lp['gla_norm_g'],
                     batch, t_len, t_len)
    om, c_m, n_m, m_m = _mlstm(zm, zero(batch, CONV_WIDTH - 1, MLSTM_W), zero(batch, MLSTM_HEADS, d, d),
                               zero(batch, MLSTM_HEADS, d), zero(batch, MLSTM_HEADS), lp, batch, t_len, t_len)
    kc, vc = _compress(r01, lw['cmp'], batch, t_len)
    on = _nsa_attend(qpad, kv, kc, vc, gates, lp['nsa_qk_g'], batch, t_len)
    y = _out_ffn(x, og, on, om, lw['w_out'], lp['norm2_g'].reshape(1, D_MODEL), lw['w_up'], lw['w_down'], 256)
    wlen = min(WINDOW, t_len)
    new_rows = _rows_on_lanes_to_state(rows_t, 4)
    win_state = _rows_on_lanes_to_state(win_t[:, :, t_len - wlen:], 2)
    conv_state = zm.reshape(batch, t_len, ZM_W)[:, t_len - (CONV_WIDTH - 1):, 0:MLSTM_W]
    return y, (new_rows, win_state, s_gla, c_m, n_m, m_m, conv_state)


GATHER_PAGES = 32
GROUPS_PER_PAGE = PAGE_SIZE // CMP_STRIDE


def _paged_cache_view(cache_nsa_kv):
    depth, n_pool = cache_nsa_kv.shape[0], cache_nsa_kv.shape[1]
    return jnp.transpose(cache_nsa_kv, (0, 1, 3, 4, 5, 2)).reshape(depth, n_pool, ROWS_W, PAGE_SIZE)


def _window_cache_view(cache_nsa_win):
    depth, batch, wb = cache_nsa_win.shape[0:3]
    return jnp.transpose(cache_nsa_win, (0, 1, 3, 4, 5, 2)).reshape(depth, batch, WIN_W, wb)


def _cmp_gather_kernel(pt_ref, ct_ref, w_ref, perm_ref, g_ref, buf, xs, sems, *, layer):
    s = pl.program_id(0)
    slot = s % 2

    def page_copies(step, dst):
        return [pltpu.make_async_copy(ct_ref.at[layer, pt_ref[step * GATHER_PAGES + p], pl.ds(0, 2 * KV_W), :],
                                      buf.at[dst, p], sems.at[dst, p]) for p in range(GATHER_PAGES)]

    @pl.when(s == 0)
    def _():
        for c in page_copies(0, 0):
            c.start()

    @pl.when(s + 1 < pl.num_programs(0))
    def _():
        for c in page_copies(s + 1, 1 - slot):
            c.start()
    for c in page_copies(s, slot):
        c.wait()

    perm = perm_ref[...]
    for p in range(GATHER_PAGES):
        xs[p] = _dot_nt(perm, buf[slot, p].astype(BF16))
    parts = []
    n_rows = GATHER_PAGES * GROUPS_PER_PAGE
    for sl in range(2):
        acc = jnp.zeros((n_rows, 2 * KV_W), F32)
        for rp in range(CMP_STRIDE // 2):
            xr = jnp.concatenate(
                [xs[:, r * GROUPS_PER_PAGE:(r + 1) * GROUPS_PER_PAGE, sl * KV_W:(sl + 1) * KV_W]
                 .reshape(n_rows, KV_W) for r in (2 * rp, 2 * rp + 1)], axis=1)
            acc = acc + _dot(xr.astype(BF16), w_ref[rp, sl])
        parts.append(acc)
    g_ref[0] = jnp.concatenate([parts[0][:, 0:KV_W], parts[1][:, 0:KV_W],
                                parts[0][:, KV_W:2 * KV_W], parts[1][:, KV_W:2 * KV_W]], axis=1)


def _cmp_gather(ct, layer, pt_flat, w_ab, batch, n_pages):
    assert n_pages % GATHER_PAGES == 0
    steps = n_pages // GATHER_PAGES
    rows = GATHER_PAGES * GROUPS_PER_PAGE
    w4 = w_ab.reshape(CMP_STRIDE, 2, KV_W, 2, 2, KV_W)
    w_rs = jnp.stack([w4[:, s, :, :, s, :] for s in range(2)], axis=1).reshape(CMP_STRIDE, 2, KV_W, 2 * KV_W)
    w_rs = jnp.transpose(w_rs.reshape(CMP_STRIDE // 2, 2, 2, KV_W, 2 * KV_W), (0, 2, 1, 3, 4))
    w_rs = w_rs.reshape(CMP_STRIDE // 2, 2, 2 * KV_W, 2 * KV_W)
    i = np.arange(PAGE_SIZE)
    perm = np.zeros((PAGE_SIZE, PAGE_SIZE), np.float32)
    perm[i, (i % GROUPS_PER_PAGE) * CMP_STRIDE + i // GROUPS_PER_PAGE] = 1.0
    perm = jnp.asarray(perm, BF16)
    grid_spec = pltpu.PrefetchScalarGridSpec(
        num_scalar_prefetch=1,
        grid=(batch * steps,),
        in_specs=[pl.BlockSpec(memory_space=pl.ANY),
                  pl.BlockSpec((CMP_STRIDE // 2, 2, 2 * KV_W, 2 * KV_W), lambda s, pt: (0, 0, 0, 0)),
                  pl.BlockSpec((PAGE_SIZE, PAGE_SIZE), lambda s, pt: (0, 0))],
        out_specs=pl.BlockSpec((1, rows, CMP_G), lambda s, pt: (s // steps, s % steps, 0)),
        scratch_shapes=[pltpu.VMEM((2, GATHER_PAGES, 2 * KV_W, PAGE_SIZE), F32),
                        pltpu.VMEM((GATHER_PAGES, PAGE_SIZE, 2 * KV_W), F32),
                        pltpu.SemaphoreType.DMA((2, GATHER_PAGES))],
    )
    return pl.pallas_call(
        functools.partial(_cmp_gather_kernel, layer=layer),
        grid_spec=grid_spec,
        out_shape=jax.ShapeDtypeStruct((batch, n_pages * GROUPS_PER_PAGE, CMP_G), F32),
        compiler_params=_params("arbitrary"),
        name="nsa_cmp_gather",
    )(pt_flat, ct, w_rs, perm)


def _to_col(row, n):
    eye = (lax.broadcasted_iota(jnp.int32, (n, n), 0) == lax.broadcasted_iota(jnp.int32, (n, n), 1))
    return jnp.sum(jnp.where(eye, jnp.broadcast_to(row, (n, n)), 0.0), axis=1, keepdims=True)


def _dec_cmp_kernel(g_ref, q_ref, wc_ref, wnew_ref, wab_ref, pe_ref, b1_ref, w2_ref, b2_ref, g1_ref,
                    ones_ref, m_ref, oc_ref, ow_ref, idx_ref, wout_ref, *, n_groups, nb, cur):
    nh, grp = NSA_KV_HEADS, NSA_GROUP
    q8 = q_ref[0]
    kc, vc = _cmp_finish(g_ref[0], n_groups, wab_ref, pe_ref, b1_ref, w2_ref, b2_ref, g1_ref, ones_ref)
    sc = _dot_nt(q8, kc.astype(BF16))
    vis = lax.broadcasted_iota(jnp.int32, sc.shape, 1) < n_groups - 1
    s_m = jnp.where(vis, sc, NEG)
    e = jnp.where(vis, jnp.exp(s_m - jnp.max(s_m, axis=-1, keepdims=True)), 0.0)
    p_c = e / jnp.maximum(jnp.sum(e, axis=-1, keepdims=True), 1e-30)
    oc_ref[0] = _dot(p_c.astype(BF16), vc.astype(BF16))

    nsp = m_ref.shape[1]
    pg = jnp.concatenate([jnp.sum(p_c[h * grp:(h + 1) * grp], axis=0, keepdims=True) for h in range(nh)]
                         + [jnp.zeros((8 - nh, n_groups), F32)], axis=0)
    h1 = pg.astype(BF16)
    r1 = pg - h1.astype(F32)
    h2 = r1.astype(BF16)
    h3 = (r1 - h2.astype(F32)).astype(BF16)
    m = m_ref[...]
    p_sel = _dot(h1, m) + _dot(h2, m) + _dot(h3, m)
    blk = lax.broadcasted_iota(jnp.int32, (1, nsp), 1)
    forced = (blk == 0) | (blk == cur) | (blk == cur - 1)
    ii = lax.broadcasted_iota(jnp.int32, (nsp, nsp), 0)
    jj = lax.broadcasted_iota(jnp.int32, (nsp, nsp), 1)
    slot = lax.broadcasted_iota(jnp.int32, (SEL_TOPK, nsp), 0).astype(F32)
    blk_f = lax.broadcasted_iota(jnp.int32, (SEL_TOPK, nsp), 1).astype(F32)
    for h in range(nh):
        score = jnp.where(blk > cur, -1.0, jnp.where(forced, FORCE_SCORE, p_sel[h:h + 1, :]))
        score = jnp.where(blk >= nb, -2.0, score)
        s_col = _to_col(score, nsp)
        ge = jnp.where(s_col >= score, 1.0, 0.0)
        gt = jnp.where(s_col > score, 1.0, 0.0)
        rank = jnp.sum(jnp.where(jj > ii, ge, gt), axis=0, keepdims=True)
        sel = (rank < SEL_TOPK).astype(F32)
        before = jnp.sum(jnp.where(ii < jj, _to_col(sel, nsp), 0.0), axis=0, keepdims=True)
        onehot = jnp.where((before == slot) & (sel > 0.5), 1.0, 0.0)
        idx = jnp.sum(onehot * blk_f, axis=1, keepdims=True)
        idx_ref[0, h] = jnp.broadcast_to(idx, (SEL_TOPK, LANES)).astype(jnp.int32)

    wt = wc_ref[0, 0]
    wb = wt.shape[1]
    wnew = wnew_ref[0]
    s = _dot(q8, wt[0:KV_W].astype(BF16))
    s = jnp.where(lax.broadcasted_iota(jnp.int32, s.shape, 1) > wb - WINDOW, s, NEG)
    qf = q8.astype(F32)
    s_new = jnp.sum(qf * wnew[:, 0:KV_W].astype(BF16).astype(F32), axis=-1, keepdims=True)
    mx = jnp.maximum(jnp.max(s, axis=-1, keepdims=True), s_new)
    e = jnp.exp(s - mx)
    e_new = jnp.exp(s_new - mx)
    num = _dot_nt(e.astype(BF16), wt[KV_W:2 * KV_W].astype(BF16)) + e_new * wnew[:, KV_W:2 * KV_W]
    ow_ref[0] = num / (jnp.sum(e, axis=-1, keepdims=True) + e_new)
    lane = lax.broadcasted_iota(jnp.int32, wt.shape, 1)
    wout_ref[0] = jnp.where(lane == wb - 1, _to_col(wnew, 2 * KV_W), pltpu.roll(wt, wb - 1, 1))


def _dec_constants(past):
    n_groups = past // CMP_STRIDE
    nb = past // SEL_BLOCK + 1
    nsp = -(-nb // LANES) * LANES
    j = np.arange(nsp)[None, :]
    n = np.arange(n_groups)[:, None]
    first = (n >= SEL_RATIO * j) & (n <= SEL_RATIO * j + SEL_RATIO - 1)
    second = (n >= SEL_RATIO * j - 1) & (n <= SEL_RATIO * j + SEL_RATIO - 2)
    m = (first.astype(np.float32) + second.astype(np.float32)) * (n < n_groups - 1) * (j < nb)
    return n_groups, nb, jnp.asarray(m, BF16)


def _dec_cmp_attn(gsum, q8, wt, layer, win_new, cw, batch, past):
    n_groups, nb, m = _dec_constants(past)
    nsp = m.shape[1]
    wb = wt.shape[3]
    assert wb == WINDOW
    per_b = lambda shape: pl.BlockSpec((1,) + shape, lambda b: (b,) + (0,) * len(shape))
    kern = functools.partial(_dec_cmp_kernel, n_groups=n_groups, nb=nb, cur=past // SEL_BLOCK)
    return pl.pallas_call(
        kern,
        grid=(batch,),
        in_specs=[per_b((n_groups, CMP_G)), per_b((NSA_HEADS, LANES)),
                  pl.BlockSpec((1, 1, 2 * KV_W, wb), lambda b: (layer, b, 0, 0)),
                  per_b((1, 2 * KV_W))] + _cmp_specs() + [_const_spec((n_groups, nsp))],
        out_specs=[per_b((NSA_HEADS, LANES)), per_b((NSA_HEADS, LANES)),
                   per_b((NSA_KV_HEADS, SEL_TOPK, LANES)), per_b((2 * KV_W, wb))],
        out_shape=[jax.ShapeDtypeStruct((batch, NSA_HEADS, LANES), F32),
                   jax.ShapeDtypeStruct((batch, NSA_HEADS, LANES), F32),
                   jax.ShapeDtypeStruct((batch, NSA_KV_HEADS, SEL_TOPK, LANES), jnp.int32),
                   jax.ShapeDtypeStruct((batch, 2 * KV_W, wb), F32)],
        compiler_params=_params("parallel"),
        name="nsa_decode_cmp",
    )(gsum, q8, wt, win_new, *cw, m)


def _dec_sel_kernel(pt_ref, idx_ref, q_ref, new_ref, oc_ref, ow_ref, gate_ref, ct_ref, out_ref,
                    kv_buf, sems, *, layer, n_pages, n_cache_blocks):
    b = pl.program_id(0)
    buf_slot = b % 2
    nh, nq, topk = NSA_KV_HEADS, NSA_HEADS, SEL_TOPK
    half = PAGE_SIZE // SEL_BLOCK
    nblk = nh * topk

    def page_copies(seq, dst):
        copies = []
        for j in range(nblk):
            blk = jnp.minimum(idx_ref[seq * nblk + j], n_cache_blocks - 1)
            page = pt_ref[seq * n_pages + blk // half]
            copies.append(pltpu.make_async_copy(ct_ref.at[layer, page, pl.ds(2 * KV_W, 2 * KV_W), :],
                                                kv_buf.at[dst, j], sems.at[dst, j]))
        return copies

    @pl.when(b == 0)
    def _():
        for c in page_copies(0, 0):
            c.start()

    @pl.when(b + 1 < pl.num_programs(0))
    def _():
        for c in page_copies(b + 1, 1 - buf_slot):
            c.start()
    for c in page_copies(b, buf_slot):
        c.wait()

    q8 = q_ref[0]
    new = new_ref[0]
    s_new = jnp.sum(q8 * new[:, 2 * KV_W:3 * KV_W].astype(BF16).astype(F32), axis=-1, keepdims=True)
    v_new = new[:, 3 * KV_W:4 * KV_W]
    qb = q8.astype(BF16)
    page_half = lax.broadcasted_iota(jnp.int32, (1, PAGE_SIZE), 1) // SEL_BLOCK
    o_heads = []
    for h in range(nh):
        s_parts = []
        m = s_new
        for k in range(topk):
            j = h * topk + k
            blk = idx_ref[b * nblk + j]
            s_k = _dot(qb, kv_buf[buf_slot, j, 0:KV_W, :].astype(BF16))
            keep = jnp.where(blk < n_cache_blocks, 0.0, NEG)
            s_k = s_k + jnp.where(page_half == blk % half, keep, NEG)
            s_parts.append(s_k)
            m = jnp.maximum(m, jnp.max(s_k, axis=-1, keepdims=True))
        e_new = jnp.exp(s_new - m)
        den = e_new
        num = e_new * v_new
        for k in range(topk):
            e = jnp.exp(s_parts[k] - m)
            den = den + jnp.sum(e, axis=-1, keepdims=True)
            num = num + _dot_nt(e.astype(BF16), kv_buf[buf_slot, h * topk + k, KV_W:2 * KV_W, :].astype(BF16))
        o_heads.append(num / den)
    row = lax.broadcasted_iota(jnp.int32, (nq, KV_W), 0)
    o_s = jnp.where(row < NSA_GROUP, o_heads[0], o_heads[1])
    gt = gate_ref[0]
    out_ref[0] = gt[:, 0:1] * oc_ref[0] + gt[:, 1:2] * o_s + gt[:, 2:3] * ow_ref[0]


def _dec_sel_attn(ct, layer, pt_flat, idx_flat, q8, rows_new, o_c, o_w, gates, batch, n_pages):
    half = PAGE_SIZE // SEL_BLOCK
    nh, grp, nq = NSA_KV_HEADS, NSA_GROUP, NSA_HEADS
    g3 = gates.reshape(batch, nh, LANES)[:, :, 0:3 * grp].reshape(batch, nq, 3)
    g8 = jnp.concatenate([g3, jnp.zeros((batch, nq, LANES - 3), F32)], axis=-1)
    new8 = jnp.broadcast_to(rows_new.reshape(batch, 1, ROWS_W), (batch, nq, ROWS_W))

    per_b = lambda w: pl.BlockSpec((1, nq, w), lambda b, pt, idx: (b, 0, 0))
    grid_spec = pltpu.PrefetchScalarGridSpec(
        num_scalar_prefetch=2,
        grid=(batch,),
        in_specs=[per_b(LANES), per_b(ROWS_W), per_b(LANES), per_b(LANES), per_b(LANES),
                  pl.BlockSpec(memory_space=pl.ANY)],
        out_specs=per_b(LANES),
        scratch_shapes=[pltpu.VMEM((2, nh * SEL_TOPK, 2 * KV_W, PAGE_SIZE), F32),
                        pltpu.SemaphoreType.DMA((2, nh * SEL_TOPK))],
    )
    kern = functools.partial(_dec_sel_kernel, layer=layer, n_pages=n_pages, n_cache_blocks=n_pages * half)
    return pl.pallas_call(
        kern,
        grid_spec=grid_spec,
        out_shape=jax.ShapeDtypeStruct((batch, nq, LANES), F32),
        compiler_params=_params("arbitrary"),
        name="nsa_decode_sel",
    )(pt_flat, idx_flat, q8.astype(F32), new8, o_c, o_w, g8, ct)


def _sample_layer(x, lp, lw, layer, ct, wt, gla_s0, c0, n0, m0, conv0, page_table):
    batch, n_pages = page_table.shape
    past = n_pages * PAGE_SIZE
    d = HEAD_DIM
    c = SCAN_CHUNK_DECODE
    zg, zn, zm = _in_proj(x, lp['norm1_g'].reshape(1, D_MODEL), lw['w_in'], batch)
    qpad, rows, win, _, _, gates = _nsa_prep(zn, lp['nsa_qk_g'], lp['nsa_gate_b'], batch)
    pad = lambda z: jnp.pad(z[:, None, :], ((0, 0), (0, c - 1), (0, 0))).reshape(batch * c, z.shape[-1])
    first = lambda o: o.reshape(batch, c, o.shape[-1])[:, 0]
    og, s_gla = _gla(pad(zg), gla_s0, lp['gla_a_w'], lp['gla_a_b'], lp['gla_norm_g'], batch, c, 1, c)
    om, c_m, n_m, m_m = _mlstm(pad(zm), conv0, c0, n0, m0, lp, batch, c, 1, c)
    pt_flat = page_table.reshape(-1)
    gsum = _cmp_gather(ct, layer, pt_flat, lw['cmp'][0], batch, n_pages)
    q8 = qpad.reshape(batch, NSA_HEADS, LANES)
    o_c, o_w, idx, win_t = _dec_cmp_attn(gsum, q8, wt, layer, win.reshape(batch, 1, 2 * KV_W), lw['cmp'],
                                         batch, past)
    o_n = _dec_sel_attn(ct, layer, pt_flat, idx[:, :, :, 0].reshape(-1), q8, rows, o_c, o_w, gates,
                        batch, n_pages)
    on = jnp.stack([o_n[:, h * NSA_GROUP:(h + 1) * NSA_GROUP, h * d:(h + 1) * d]
                    for h in range(NSA_KV_HEADS)], axis=1)
    on = on.reshape(batch, NSA_W).astype(BF16)
    y = _out_ffn(x, first(og), on, first(om), lw['w_out'], lp['norm2_g'].reshape(1, D_MODEL),
                 lw['w_up'], lw['w_down'], batch)
    new_rows = rows.reshape(batch, 1, 4, NSA_KV_HEADS, d)
    win_state = _rows_on_lanes_to_state(win_t, 2)
    conv_state = jnp.concatenate([conv0.astype(F32), zm[:, None, 0:MLSTM_W]], axis=1)[:, 1:]
    return y, (new_rows, win_state, s_gla, c_m, n_m, m_m, conv_state)


def kernel(x_prompt, x_sample, cache_nsa_kv, cache_nsa_win, state_gla, state_mlstm_c, state_mlstm_n,
           state_mlstm_m, state_mlstm_conv, page_table, norm1_g, w_in, gla_a_w, gla_a_b, gla_norm_g,
           nsa_qk_g, nsa_gate_b, cmp_pe, cmp_w1, cmp_b1, cmp_w2, cmp_b2, ml_conv_w, ml_conv_b, ml_wq, ml_wk,
           ml_gate_b, ml_norm_g, ml_skip, w_out, norm2_g, w_up, w_down):
    bp, t_len, _ = x_prompt.shape
    n_dec = x_sample.shape[0]
    depth = w_in.shape[0]
    x_p = x_prompt.reshape(bp * t_len, D_MODEL)
    x_s = x_sample.reshape(n_dec, D_MODEL)
    states_p, states_s = [], []
    ct = _paged_cache_view(cache_nsa_kv)
    wt = _window_cache_view(cache_nsa_win)
    for l in range(depth):
        lp = {'norm1_g': norm1_g[l], 'w_in': w_in[l], 'gla_a_w': gla_a_w[l], 'gla_a_b': gla_a_b[l],
              'gla_norm_g': gla_norm_g[l], 'nsa_qk_g': nsa_qk_g[l], 'nsa_gate_b': nsa_gate_b[l],
              'cmp_pe': cmp_pe[l], 'cmp_w1': cmp_w1[l], 'cmp_b1': cmp_b1[l], 'cmp_w2': cmp_w2[l],
              'cmp_b2': cmp_b2[l], 'ml_conv_w': ml_conv_w[l], 'ml_conv_b': ml_conv_b[l], 'ml_wq': ml_wq[l],
              'ml_wk': ml_wk[l], 'ml_gate_b': ml_gate_b[l], 'ml_norm_g': ml_norm_g[l], 'ml_skip': ml_skip[l],
              'w_out': w_out[l], 'norm2_g': norm2_g[l], 'w_up': w_up[l], 'w_down': w_down[l]}
        lw = _layer_weights(lp)
        x_p, st_p = _prompt_layer(x_p, lp, lw, bp, t_len)
        x_s, st_s = _sample_layer(x_s, lp, lw, l, ct, wt, state_gla[l],
                                  state_mlstm_c[l], state_mlstm_n[l], state_mlstm_m[l], state_mlstm_conv[l],
                                  page_table)
        states_p.append(st_p)
        states_s.append(st_s)
    outs = [x_p.reshape(bp, t_len, D_MODEL), x_s.reshape(n_dec, 1, D_MODEL)]
    for i in range(7):
        outs.append(jnp.stack([s[i] for s in states_p]))
        outs.append(jnp.stack([s[i] for s in states_s]))
    return tuple(outs)
```

```python
import functools

import numpy as np
import jax
import jax.numpy as jnp
from jax import lax
from jax.experimental import pallas as pl
from jax.experimental.pallas import tpu as pltpu

F32 = jnp.float32
BF16 = jnp.bfloat16

D_MODEL = 1024
HEAD_DIM = 64
PAGE_SIZE = 128
GLA_HEADS = 4
GLA_LOWRANK = 16
GLA_TAU = 16.0
NSA_HEADS = 8
NSA_KV_HEADS = 2
NSA_GROUP = NSA_HEADS // NSA_KV_HEADS
CMP_BLOCK = 32
CMP_STRIDE = 16
CMP_HIDDEN = 64
SEL_BLOCK = 64
SEL_RATIO = SEL_BLOCK // CMP_STRIDE
SEL_TOPK = 16
WINDOW = 512
FORCE_SCORE = 1.0e4
MLSTM_HEADS = 4
CONV_WIDTH = 4
D_FF = 4 * D_MODEL
EPS = 1e-6

GLA_W = GLA_HEADS * HEAD_DIM
NSA_W = NSA_HEADS * HEAD_DIM
KV_W = NSA_KV_HEADS * HEAD_DIM
MLSTM_W = MLSTM_HEADS * HEAD_DIM

LANES = 128
ZG_W = 4 * GLA_W + LANES
GATE_W = NSA_KV_HEADS * LANES
ZN_W = NSA_W + 6 * KV_W + GATE_W
ZM_W = 3 * MLSTM_W + LANES
ROWS_W = 4 * KV_W
WIN_W = 2 * KV_W
QPAD_W = NSA_HEADS * LANES

SCAN_CHUNK = 128
SCAN_CHUNK_DECODE = 32
GLA_SUB = 32
NSA_TQ = 128
NSA_TK = 256
NSA_TK_FIXED = 512
NSA_MAX_FIXED_SHIFT = 40.0
NSA_RANK_SIZES = 4
NSA_TILE_UNROLL = 4
NEG = -1.0e30
SEL_BIAS = 29952.0

VMEM_LIMIT = 56 * 1024 * 1024


def _dot(a, b):
    return jnp.dot(a, b, preferred_element_type=F32)


def _dot_hi(a, b):
    return jnp.dot(a, b, preferred_element_type=F32, precision=lax.Precision.HIGHEST)


def _dot_nt(a, b):
    return lax.dot_general(a, b, (((1,), (1,)), ((), ())), preferred_element_type=F32)


def _dot_nt_hi(a, b):
    return lax.dot_general(a, b, (((1,), (1,)), ((), ())), preferred_element_type=F32,
                           precision=lax.Precision.HIGHEST)


def _dot_tn_hi(a, b):
    return lax.dot_general(a, b, (((0,), (0,)), ((), ())), preferred_element_type=F32,
                           precision=lax.Precision.HIGHEST)


def _sigmoid(x):
    return 1.0 / (1.0 + jnp.exp(-x))


def _log_sigmoid(x):
    return jnp.minimum(x, 0.0) - jnp.log(1.0 + jnp.exp(-jnp.abs(x)))


def _group_sum(x, ones_bd):
    hi = x.astype(BF16)
    lo = (x - hi.astype(F32)).astype(BF16)
    return _dot(hi, ones_bd) + _dot(lo, ones_bd)


def _group_mean_sq(x, ones_bd):
    return _group_sum(x * x, ones_bd) * (1.0 / HEAD_DIM)


def _params(*sem):
    return pltpu.CompilerParams(dimension_semantics=sem, vmem_limit_bytes=VMEM_LIMIT)


def _const_spec(shape):
    nd = len(shape)
    return pl.BlockSpec(shape, lambda *_: (0,) * nd)


def _inproj_kernel(x_ref, g_ref, w_ref, zg_ref, zn_ref, zm_ref):
    x = x_ref[...]
    h = x * lax.rsqrt(jnp.mean(x * x, axis=-1, keepdims=True) + EPS) * g_ref[...]
    hb = h.astype(BF16)
    zg_ref[...] = _dot(hb, w_ref[:, 0:ZG_W])
    zn_ref[...] = _dot(hb, w_ref[:, ZG_W:ZG_W + ZN_W])
    zm_ref[...] = _dot(hb, w_ref[:, ZG_W + ZN_W:ZG_W + ZN_W + ZM_W])


def _in_proj(x, g, w, tm):
    n = x.shape[0]
    zw = ZG_W + ZN_W + ZM_W
    return pl.pallas_call(
        _inproj_kernel,
        grid=(n // tm,),
        in_specs=[pl.BlockSpec((tm, D_MODEL), lambda i: (i, 0)),
                  _const_spec((1, D_MODEL)),
                  _const_spec((D_MODEL, zw))],
        out_specs=[pl.BlockSpec((tm, ZG_W), lambda i: (i, 0)),
                   pl.BlockSpec((tm, ZN_W), lambda i: (i, 0)),
                   pl.BlockSpec((tm, ZM_W), lambda i: (i, 0))],
        out_shape=[jax.ShapeDtypeStruct((n, ZG_W), F32),
                   jax.ShapeDtypeStruct((n, ZN_W), F32),
                   jax.ShapeDtypeStruct((n, ZM_W), F32)],
        compiler_params=_params("parallel"),
        name="in_proj",
    )(x, g, w)


def _pack_w_in(w_in):
    def cols(a, b):
        return w_in[:, a:b]

    def zeros(n):
        return jnp.zeros((D_MODEL, n), w_in.dtype)
    o_nsa = 4 * GLA_W + GLA_LOWRANK
    o_ng = o_nsa + NSA_W + 6 * KV_W
    o_ml = o_ng + 3 * NSA_HEADS
    o_mi = o_ml + 2 * MLSTM_W
    o_mo = o_mi + 2 * MLSTM_HEADS
    n_gate = 3 * NSA_GROUP
    parts = [cols(0, o_nsa), zeros(LANES - GLA_LOWRANK),
             cols(o_nsa, o_ng),
             cols(o_ng, o_ng + n_gate), zeros(LANES - n_gate),
             cols(o_ng + n_gate, o_ml), zeros(LANES - n_gate),
             cols(o_ml, o_mi), cols(o_mo, o_mo + MLSTM_W), cols(o_mi, o_mo),
             zeros(LANES - 2 * MLSTM_HEADS)]
    return jnp.concatenate(parts, axis=1).astype(BF16)


def _nsa_prep_kernel(zn_ref, gq_ref, gr_ref, gw_ref, gb_ref, ones_ref, place_ref,
                     qpad_ref, rows_ref, win_ref, kv_ref, r01_ref, gate_ref, *, transposed):
    ones_bd = ones_ref[...]
    q = zn_ref[:, 0:NSA_W]
    qn = q * lax.rsqrt(_group_mean_sq(q, ones_bd) + EPS) * gq_ref[...]
    qpad_ref[...] = _dot(qn.astype(BF16), place_ref[...]).astype(BF16)

    r = zn_ref[:, NSA_W:NSA_W + ROWS_W]
    col = lax.broadcasted_iota(jnp.int32, r.shape, 1)
    rn = r * lax.rsqrt(_group_mean_sq(r, ones_bd) + EPS) * gr_ref[...]
    rows = jnp.where((col >= 2 * KV_W) & (col < 3 * KV_W), rn, r)
    if transposed:
        rows_ref[0] = rows.T
    else:
        rows_ref[...] = rows
    r01_ref[...] = rows[:, 0:2 * KV_W].astype(BF16)

    w = zn_ref[:, NSA_W + ROWS_W:NSA_W + ROWS_W + WIN_W]
    colw = lax.broadcasted_iota(jnp.int32, w.shape, 1)
    wn = w * lax.rsqrt(_group_mean_sq(w, ones_bd[0:WIN_W, 0:WIN_W]) + EPS) * gw_ref[...]
    win = jnp.where(colw < KV_W, wn, w)
    if transposed:
        win_ref[0] = win.T
    else:
        win_ref[...] = win
    kv_ref[:, 0:2 * KV_W] = rows[:, 2 * KV_W:4 * KV_W].astype(BF16)
    kv_ref[:, 2 * KV_W:4 * KV_W] = win.astype(BF16)

    gate_ref[...] = _sigmoid(zn_ref[:, NSA_W + ROWS_W + WIN_W:ZN_W] + gb_ref[...])


def _nsa_prep(zn, qk_g, gate_b, tm, seq=None):
    n = zn.shape[0]
    scale = HEAD_DIM ** -0.5
    gq = (jnp.tile(qk_g[0], NSA_HEADS) * scale).reshape(1, NSA_W)
    gr = jnp.tile(qk_g[2], ROWS_W // HEAD_DIM).reshape(1, ROWS_W)
    gw = jnp.tile(qk_g[3], WIN_W // HEAD_DIM).reshape(1, WIN_W)
    n_gate = 3 * NSA_GROUP
    gpad = jnp.zeros((LANES - n_gate,), F32)
    gb = jnp.concatenate([gate_b[0:n_gate], gpad, gate_b[n_gate:], gpad]).reshape(1, GATE_W)
    grp = np.arange(NSA_W) // HEAD_DIM
    ones_bd = jnp.asarray(grp[:, None] == grp[None, :], BF16)
    src = np.arange(NSA_W)
    head, d = src // HEAD_DIM, src % HEAD_DIM
    dst = head * LANES + (head // NSA_GROUP) * HEAD_DIM + d
    place = np.zeros((NSA_W, QPAD_W), np.float32)
    place[src, dst] = 1.0
    place = jnp.asarray(place, BF16)
    row = lambda w: pl.BlockSpec((tm, w), lambda i: (i, 0))
    if seq is None:
        state_spec = row
        state_shape = lambda w: jax.ShapeDtypeStruct((n, w), F32)
    else:
        batch, t_len = seq
        nblk = t_len // tm
        state_spec = lambda w: pl.BlockSpec((1, w, tm), lambda i: (i // nblk, 0, i % nblk))
        state_shape = lambda w: jax.ShapeDtypeStruct((batch, w, t_len), F32)
    return pl.pallas_call(
        functools.partial(_nsa_prep_kernel, transposed=seq is not None),
        grid=(n // tm,),
        in_specs=[row(ZN_W), _const_spec((1, NSA_W)), _const_spec((1, ROWS_W)),
                  _const_spec((1, WIN_W)), _const_spec((1, GATE_W)),
                  _const_spec((NSA_W, NSA_W)), _const_spec((NSA_W, QPAD_W))],
        out_specs=[row(QPAD_W), state_spec(ROWS_W), state_spec(WIN_W), row(4 * KV_W), row(2 * KV_W),
                   row(GATE_W)],
        out_shape=[jax.ShapeDtypeStruct((n, QPAD_W), BF16),
                   state_shape(ROWS_W),
                   state_shape(WIN_W),
                   jax.ShapeDtypeStruct((n, 4 * KV_W), BF16),
                   jax.ShapeDtypeStruct((n, 2 * KV_W), BF16),
                   jax.ShapeDtypeStruct((n, GATE_W), F32)],
        compiler_params=_params("parallel"),
        name="nsa_prep",
    )(zn, gq, gr, gw, gb, ones_bd, place)


def _load_block_diag(bd_scr, heads_ref):
    nh, d = heads_ref.shape[1], heads_ref.shape[2]
    bd_scr[...] = jnp.zeros(bd_scr.shape, F32)
    for h in range(nh):
        bd_scr[h * d:(h + 1) * d, h * d:(h + 1) * d] = heads_ref[0, h]


def _store_block_diag(heads_ref, bd_scr):
    nh, d = heads_ref.shape[1], heads_ref.shape[2]
    for h in range(nh):
        heads_ref[0, h] = bd_scr[h * d:(h + 1) * d, h * d:(h + 1) * d]


def _gla_kernel(zg_ref, s0_ref, aw_ref, ab_ref, ng_ref, tri_ref, ones_ref, og_ref, sout_ref, s_scr,
                *, t_valid, t_pad):
    i = pl.program_id(1)
    c = zg_ref.shape[0]
    d = HEAD_DIM

    @pl.when(i == 0)
    def _():
        _load_block_diag(s_scr, s0_ref)

    q = zg_ref[:, 0:GLA_W] * (d ** -0.5)
    k = zg_ref[:, GLA_W:2 * GLA_W]
    v = zg_ref[:, 2 * GLA_W:3 * GLA_W]
    r = zg_ref[:, 3 * GLA_W:4 * GLA_W]
    ga = zg_ref[:, 4 * GLA_W:ZG_W]
    g = _log_sigmoid(_dot(ga.astype(BF16), aw_ref[...]) + ab_ref[...]) * (1.0 / GLA_TAU)
    if t_valid < t_pad:
        valid = (i * c + lax.broadcasted_iota(jnp.int32, (c, 1), 0)) < t_valid
        g = jnp.where(valid, g, 0.0)
        k = jnp.where(valid, k, 0.0)
    bcum = _dot_hi(tri_ref[...], g)
    b_end = bcum[c - 1:c, :]
    nh, w, sub = GLA_HEADS, GLA_W, min(GLA_SUB, c)
    lane_head = lax.broadcasted_iota(jnp.int32, (1, w), 1) // d
    s_bd = s_scr[...]
    o = _dot((q * jnp.exp(bcum)).astype(BF16), s_bd.astype(BF16))
    vb = v.astype(BF16)
    o_sub = []
    for j in range(c // sub):
        lo, hi = j * sub, (j + 1) * sub
        base = bcum[lo - 1:lo, :] if j > 0 else jnp.zeros((1, w), F32)
        qt = q[lo:hi] * jnp.exp(bcum[lo:hi] - base)
        kt = (k[0:hi] * jnp.exp(base - bcum[0:hi])).astype(BF16)
        qs = jnp.concatenate([jnp.where(lane_head == h, qt, 0.0) for h in range(nh)], axis=0)
        sc = _dot_nt(qs.astype(BF16), kt)
        t_row = lo + (lax.broadcasted_iota(jnp.int32, (nh * sub, hi), 0) % sub)
        sc = jnp.where(lax.broadcasted_iota(jnp.int32, (nh * sub, hi), 1) <= t_row, sc, 0.0)
        ov = _dot(sc.astype(BF16), vb[0:hi])
        o_j = jnp.where(lane_head == 0, ov[0:sub], 0.0)
        for h in range(1, nh):
            o_j = jnp.where(lane_head == h, ov[h * sub:(h + 1) * sub], o_j)
        o_sub.append(o_j)
    o = o + jnp.concatenate(o_sub, axis=0)
    last = (lax.broadcasted_iota(jnp.int32, (c, LANES), 0) == c - 1).astype(F32)
    decay_col = jnp.exp(_dot_tn_hi(bcum, last))
    k_hat = k * jnp.exp(b_end - bcum)
    same_head = (lax.broadcasted_iota(jnp.int32, (w, w), 0) // d) == (lax.broadcasted_iota(jnp.int32, (w, w), 1) // d)
    s_scr[...] = (jnp.concatenate([decay_col] * (w // LANES), axis=1) * s_bd
                  + jnp.where(same_head, _dot_tn_hi(k_hat, v), 0.0))
    on = o * lax.rsqrt(_group_mean_sq(o, ones_ref[...]) + EPS) * ng_ref[...]
    og_ref[...] = (on * (r * _sigmoid(r))).astype(BF16)

    @pl.when(i == pl.num_programs(1) - 1)
    def _():
        _store_block_diag(sout_ref, s_scr)


def _gla(zg, s0, a_w, a_b, norm_g, batch, t_pad, t_valid, c=SCAN_CHUNK):
    nblk = t_pad // c
    aw = jnp.concatenate([a_w, jnp.zeros((LANES - GLA_LOWRANK, GLA_W), F32)], axis=0).astype(BF16)
    tri = jnp.asarray(np.tril(np.ones((c, c), np.float32)))
    grp = np.arange(GLA_W) // HEAD_DIM
    ones_bd = jnp.asarray(grp[:, None] == grp[None, :], BF16)
    kern = functools.partial(_gla_kernel, t_valid=t_valid, t_pad=t_pad)
    state = pl.BlockSpec((1, GLA_HEADS, HEAD_DIM, HEAD_DIM), lambda b, i: (b, 0, 0, 0))
    return pl.pallas_call(
        kern,
        grid=(batch, nblk),
        in_specs=[pl.BlockSpec((c, ZG_W), lambda b, i: (b * nblk + i, 0)), state,
                  _const_spec((LANES, GLA_W)), _const_spec((1, GLA_W)), _const_spec((1, GLA_W)),
                  _const_spec((c, c)), _const_spec((GLA_W, GLA_W))],
        out_specs=[pl.BlockSpec((c, GLA_W), lambda b, i: (b * nblk + i, 0)), state],
        out_shape=[jax.ShapeDtypeStruct((batch * t_pad, GLA_W), BF16),
                   jax.ShapeDtypeStruct((batch, GLA_HEADS, HEAD_DIM, HEAD_DIM), F32)],
        scratch_shapes=[pltpu.VMEM((GLA_W, GLA_W), F32)],
        compiler_params=_params("parallel", "arbitrary"),
        name="gla_scan",
    )(zg, s0, aw, a_b.reshape(1, GLA_W), jnp.tile(norm_g, GLA_HEADS).reshape(1, GLA_W), tri, ones_bd)


def _mlstm_kernel(zm_ref, conv0_ref, c0_ref, n0_ref, m0_ref, cw_ref, cb_ref, wq_ref, wk_ref, gb_ref,
                  ng_ref, skip_ref, tri_ref, ones_ref, om_ref, cout_ref, nout_ref, mout_ref,
                  c_scr, n_scr, m_scr, ext_scr, *, t_valid, t_pad):
    i = pl.program_id(1)
    c = zm_ref.shape[0]
    d = HEAD_DIM
    nh = MLSTM_HEADS

    @pl.when(i == 0)
    def _():
        _load_block_diag(c_scr, c0_ref)
        n_scr[...] = n0_ref[0]
        m_scr[...] = m0_ref[0]
        ext_scr[0:8, :] = conv0_ref[0]

    mu = zm_ref[:, 0:MLSTM_W]
    ext_scr[8:8 + c, :] = mu
    u_conv = cb_ref[...] + mu * cw_ref[CONV_WIDTH - 1:CONV_WIDTH, :]
    for j in range(1, CONV_WIDTH):
        u_conv = u_conv + ext_scr[8 - j:8 - j + c, :] * cw_ref[CONV_WIDTH - 1 - j:CONV_WIDTH - j, :]
    ext_scr[0:8, :] = mu[c - 8:c, :]
    u_act = u_conv * _sigmoid(u_conv)
    ub = u_act.astype(BF16)
    q = _dot(ub, wq_ref[...])
    k = _dot(ub, wk_ref[...]) * (d ** -0.5)
    v = zm_ref[:, MLSTM_W:2 * MLSTM_W]
    og = zm_ref[:, 2 * MLSTM_W:3 * MLSTM_W]
    gz = zm_ref[:, 3 * MLSTM_W:ZM_W] + gb_ref[...]
    lane = lax.broadcasted_iota(jnp.int32, (c, LANES), 1)
    x = jnp.where(lane < nh, gz, _log_sigmoid(gz))
    if t_valid < t_pad:
        valid = (i * c + lax.broadcasted_iota(jnp.int32, (c, 1), 0)) < t_valid
        x = jnp.where(valid, x, jnp.where(lane < nh, NEG, 0.0))
    fc = _dot_hi(tri_ref[...], x)
    x = jnp.where(lane < nh, x, fc)
    sel = (lax.broadcasted_iota(jnp.int32, (8, LANES), 0)
           == lax.broadcasted_iota(jnp.int32, (8, LANES), 1)).astype(F32)
    xt = _dot_nt_hi(sel, x)
    row_i = lax.broadcasted_iota(jnp.int32, (c, c), 0)
    col_i = lax.broadcasted_iota(jnp.int32, (c, c), 1)
    w = MLSTM_W
    lane_head = lax.broadcasted_iota(jnp.int32, (1, w), 1) // d
    c_bd = c_scr[...]
    n_all = n_scr[0:1, :]
    m_all = m_scr[0:1, :]
    qb = q.astype(BF16)
    qs = jnp.concatenate([jnp.where(lane_head == h, q, 0.0) for h in range(nh)], axis=0)
    qk_all = _dot_nt(qs.astype(BF16), k.astype(BF16))
    qn_sum = _group_sum(q * n_all, ones_ref[...])
    zc, zr = jnp.zeros((c, w), F32), jnp.zeros((1, w), F32)
    mt_l, ws_l, rs_l, wl_l, wc_l, mn_l = zc, zc, zc, zc, zr, zr
    p_parts = []
    for h in range(nh):
        i_col, f_col = x[:, h:h + 1], x[:, nh + h:nh + h + 1]
        i_row, f_row = xt[h:h + 1, :], xt[nh + h:nh + h + 1, :]
        a = f_col + m_all[:, h * d:h * d + 1]
        dmat = jnp.where(col_i <= row_i, f_col - f_row + i_row, NEG)
        m_t = jnp.maximum(a, jnp.max(dmat, axis=-1, keepdims=True))
        p_h = qk_all[h * c:(h + 1) * c] * jnp.exp(dmat - m_t)
        p_parts.append(p_h)
        m_new = m_t[c - 1:c, :]
        on_head = lane_head == h
        mt_l = jnp.where(on_head, m_t, mt_l)
        ws_l = jnp.where(on_head, jnp.exp(a - m_t), ws_l)
        rs_l = jnp.where(on_head, jnp.sum(p_h, axis=-1, keepdims=True), rs_l)
        wl_l = jnp.where(on_head, jnp.exp(f_col[c - 1:c, :] - f_col + i_col - m_new), wl_l)
        wc_l = jnp.where(on_head, jnp.exp(a[c - 1:c, :] - m_new), wc_l)
        mn_l = jnp.where(on_head, m_new, mn_l)
    nv = _dot(jnp.concatenate(p_parts, axis=0).astype(BF16), v.astype(BF16))
    num = jnp.where(lane_head == 0, nv[0:c], 0.0)
    for h in range(1, nh):
        num = jnp.where(lane_head == h, nv[h * c:(h + 1) * c], num)
    num = num + ws_l * _dot(qb, c_bd.astype(BF16))
    den = rs_l + ws_l * qn_sum
    hm = num / jnp.maximum(jnp.abs(den), jnp.exp(-mt_l))
    kw = k * wl_l
    same_head = (lax.broadcasted_iota(jnp.int32, (w, w), 0) // d) == (lax.broadcasted_iota(jnp.int32, (w, w), 1) // d)
    c_scr[...] = wc_l * c_bd + jnp.where(same_head, _dot_tn_hi(kw, v), 0.0)
    n_scr[0:1, :] = wc_l * n_all + jnp.sum(kw, axis=0, keepdims=True)
    m_scr[0:1, :] = mn_l
    hn = hm * lax.rsqrt(_group_mean_sq(hm, ones_ref[...]) + EPS) * ng_ref[...]
    om_ref[...] = (_sigmoid(og) * (hn + skip_ref[...] * u_act)).astype(BF16)

    @pl.when(i == pl.num_programs(1) - 1)
    def _():
        _store_block_diag(cout_ref, c_scr)
        nout_ref[0] = n_scr[...]
        mout_ref[0] = m_scr[...]


def _block_diag_heads(w):
    nh, d, _ = w.shape
    eye = jnp.eye(nh, dtype=w.dtype)
    return jnp.einsum('hde,hg->hdge', w, eye).reshape(nh * d, nh * d)


def _mlstm(zm, conv0, c0, n0, m0, lp, batch, t_pad, t_valid, c=SCAN_CHUNK):
    nblk = t_pad // c
    nh, d = MLSTM_HEADS, HEAD_DIM
    conv0p = jnp.concatenate([jnp.zeros((batch, 8 - (CONV_WIDTH - 1), MLSTM_W), F32), conv0.astype(F32)], axis=1)
    pad7 = jnp.zeros((batch, 7, MLSTM_W), F32)
    n0p = jnp.concatenate([n0.reshape(batch, 1, MLSTM_W), pad7], axis=1)
    m0p = jnp.concatenate([jnp.repeat(m0, d, axis=1).reshape(batch, 1, MLSTM_W), pad7], axis=1)
    gb = jnp.concatenate([lp['ml_gate_b'][0], lp['ml_gate_b'][1],
                          jnp.zeros((LANES - 2 * nh,), F32)]).reshape(1, LANES)
    tri = jnp.asarray(np.tril(np.ones((c, c), np.float32)))
    grp = np.arange(MLSTM_W) // d
    ones_bd = jnp.asarray(grp[:, None] == grp[None, :], BF16)
    kern = functools.partial(_mlstm_kernel, t_valid=t_valid, t_pad=t_pad)
    per_b = lambda shape: pl.BlockSpec((1,) + shape, lambda b, i: (b,) + (0,) * len(shape))
    om, c_f, n_f, m_f = pl.pallas_call(
        kern,
        grid=(batch, nblk),
        in_specs=[pl.BlockSpec((c, ZM_W), lambda b, i: (b * nblk + i, 0)),
                  per_b((8, MLSTM_W)), per_b((nh, d, d)), per_b((8, MLSTM_W)), per_b((8, MLSTM_W)),
                  _const_spec((CONV_WIDTH, MLSTM_W)), _const_spec((1, MLSTM_W)),
                  _const_spec((MLSTM_W, MLSTM_W)), _const_spec((MLSTM_W, MLSTM_W)),
                  _const_spec((1, LANES)), _const_spec((1, MLSTM_W)), _const_spec((1, MLSTM_W)),
                  _const_spec((c, c)), _const_spec((MLSTM_W, MLSTM_W))],
        out_specs=[pl.BlockSpec((c, MLSTM_W), lambda b, i: (b * nblk + i, 0)),
                   per_b((nh, d, d)), per_b((8, MLSTM_W)), per_b((8, MLSTM_W))],
        out_shape=[jax.ShapeDtypeStruct((batch * t_pad, MLSTM_W), BF16),
                   jax.ShapeDtypeStruct((batch, nh, d, d), F32),
                   jax.ShapeDtypeStruct((batch, 8, MLSTM_W), F32),
                   jax.ShapeDtypeStruct((batch, 8, MLSTM_W), F32)],
        scratch_shapes=[pltpu.VMEM((MLSTM_W, MLSTM_W), F32), pltpu.VMEM((8, MLSTM_W), F32),
                        pltpu.VMEM((8, MLSTM_W), F32), pltpu.VMEM((8 + c, MLSTM_W), F32)],
        compiler_params=_params("parallel", "arbitrary"),
        name="mlstm_scan",
    )(zm, conv0p, c0, n0p, m0p, lp['ml_conv_w'], lp['ml_conv_b'].reshape(1, MLSTM_W),
      _block_diag_heads(lp['ml_wq']).astype(BF16), _block_diag_heads(lp['ml_wk']).astype(BF16),
      gb, jnp.tile(lp['ml_norm_g'], nh).reshape(1, MLSTM_W), lp['ml_skip'].reshape(1, MLSTM_W),
      tri, ones_bd)
    return om, c_f, n_f[:, 0].reshape(batch, nh, d), m_f[:, 0].reshape(batch, nh, d)[:, :, 0]


CMP_IN = CMP_STRIDE * 2 * KV_W
CMP_G = 2 * 2 * KV_W


def _cmp_weights(lp):
    w1 = lp['cmp_w1'].reshape(2, 2, CMP_STRIDE, HEAD_DIM, CMP_HIDDEN)
    eye = jnp.eye(2, dtype=F32)
    w_ab = jnp.einsum('sarde,st,hg->rshdatge', w1, eye, eye).reshape(CMP_IN, CMP_G)
    pe = lp['cmp_pe'].reshape(2, 2, CMP_STRIDE, 1, HEAD_DIM)
    pe = jnp.broadcast_to(jnp.transpose(pe, (1, 2, 0, 3, 4)),
                          (2, CMP_STRIDE, 2, NSA_KV_HEADS, HEAD_DIM)).reshape(2, CMP_IN)
    pe8 = jnp.concatenate([pe, jnp.zeros((6, CMP_IN), F32)], axis=0)
    b1 = jnp.broadcast_to(lp['cmp_b1'][:, None, :], (2, NSA_KV_HEADS, CMP_HIDDEN)).reshape(1, 2 * KV_W)
    w2 = jnp.einsum('sed,st,hg->shetgd', lp['cmp_w2'], eye, eye).reshape(2 * KV_W, 2 * KV_W)
    b2 = jnp.broadcast_to(lp['cmp_b2'][:, None, :], (2, NSA_KV_HEADS, HEAD_DIM)).reshape(1, 2 * KV_W)
    g1 = jnp.tile(lp['nsa_qk_g'][1], NSA_KV_HEADS).reshape(1, KV_W)
    grp = np.arange(KV_W) // HEAD_DIM
    ones_bd = jnp.asarray(grp[:, None] == grp[None, :], BF16)
    return (w_ab.astype(BF16), pe8.astype(BF16), b1, w2.astype(BF16), b2, g1, ones_bd)


def _cmp_finish(gsum, n_rows, wab_ref, pe_ref, b1_ref, w2_ref, b2_ref, g1_ref, ones_ref):
    half = 2 * KV_W
    g_pe = _dot(pe_ref[...], wab_ref[...])
    bias = g_pe[0:1, 0:half] + g_pe[1:2, half:CMP_G] + b1_ref[...]
    hid = gsum[:, 0:half] + pltpu.roll(gsum[:, half:CMP_G], n_rows - 1, 0) + bias
    act = hid * _sigmoid(hid)
    cmp = _dot(act.astype(BF16), w2_ref[...]) + b2_ref[...]
    kc = cmp[:, 0:KV_W]
    kc = kc * lax.rsqrt(_group_mean_sq(kc, ones_ref[...]) + EPS) * g1_ref[...]
    return kc, cmp[:, KV_W:half]


def _cmp_kernel(r_ref, wab_ref, pe_ref, b1_ref, w2_ref, b2_ref, g1_ref, ones_ref, kc_ref, vc_ref, *, n_rows):
    gsum = _dot(r_ref[0], wab_ref[...])
    kc, vc = _cmp_finish(gsum, n_rows, wab_ref, pe_ref, b1_ref, w2_ref, b2_ref, g1_ref, ones_ref)
    kc_ref[0] = kc.astype(BF16)
    vc_ref[0] = vc.astype(BF16)


def _cmp_specs():
    half = 2 * KV_W
    return [_const_spec((CMP_IN, CMP_G)), _const_spec((8, CMP_IN)), _const_spec((1, half)),
            _const_spec((half, half)), _const_spec((1, half)), _const_spec((1, KV_W)),
            _const_spec((KV_W, KV_W))]


def _compress(r01, cw, batch, t_len):
    n16 = t_len // CMP_STRIDE
    x = r01.reshape(batch, n16, CMP_IN)
    blk = lambda w: pl.BlockSpec((1, n16, w), lambda b: (b, 0, 0))
    return pl.pallas_call(
        functools.partial(_cmp_kernel, n_rows=n16),
        grid=(batch,),
        in_specs=[blk(CMP_IN)] + _cmp_specs(),
        out_specs=[blk(KV_W), blk(KV_W)],
        out_shape=[jax.ShapeDtypeStruct((batch, n16, KV_W), BF16)] * 2,
        compiler_params=_params("parallel"),
        name="nsa_compress",
    )(x, *cw)


def _nsa_kernel(qpad_ref, ksel_ref, vsel_ref, kwin_ref, vwin_ref, kc_ref, vc_ref, gate_ref, mt_ref,
                et_ref, pl_ref, ge_ref, bound_ref, out_ref, sc_scr, lhs_scr, m_scr, l_scr, acc_scr, *, nb, fixed_max):
    qi = pl.program_id(2)
    tq, grp = NSA_TQ, NSA_GROUP
    tk = NSA_TK_FIXED if fixed_max else NSA_TK
    rows = grp * tq
    nbp, ncp = mt_ref.shape
    start = qi * tq
    q4 = jnp.concatenate([qpad_ref[:, g * LANES:(g + 1) * LANES] for g in range(grp)], axis=0)
    tpos = start + (lax.broadcasted_iota(jnp.int32, (rows, 1), 0) & (tq - 1))

    sc = _dot_nt(q4, kc_ref[0])
    ccol = lax.broadcasted_iota(jnp.int32, (rows, ncp), 1)
    vis = (ccol * CMP_STRIDE + (CMP_BLOCK - 1)) <= tpos
    if fixed_max:
        bound = bound_ref[0, 0]
        e = jnp.where(vis, jnp.exp(sc - bound), 0.0)
        den = jnp.sum(e, axis=-1, keepdims=True)
        p_c = e / jnp.where(den > 0.0, den, 1.0)
    else:
        s_m = jnp.where(vis, sc, NEG)
        e = jnp.where(vis, jnp.exp(s_m - jnp.max(s_m, axis=-1, keepdims=True)), 0.0)
        p_c = e / jnp.maximum(jnp.sum(e, axis=-1, keepdims=True), 1e-30)
    o_c = _dot(p_c.astype(BF16), vc_ref[0])

    pg = p_c[0:tq]
    for g in range(1, grp):
        pg = pg + p_c[g * tq:(g + 1) * tq]
    h1 = pg.astype(BF16)
    r1 = pg - h1.astype(F32)
    h2 = r1.astype(BF16)
    h3 = (r1 - h2.astype(F32)).astype(BF16)
    mt = mt_ref[...]
    p_sel = _dot_nt(mt, h1) + _dot_nt(mt, h2) + _dot_nt(mt, h3)
    blk = lax.broadcasted_iota(jnp.int32, (nbp, tq), 0)
    cur = (start + lax.broadcasted_iota(jnp.int32, (nbp, tq), 1)) // SEL_BLOCK
    forced = (blk == 0) | (blk == cur) | (blk == cur - 1)
    score = jnp.where(blk > cur, -1.0, jnp.where(forced, FORCE_SCORE, p_sel))
    if nb < nbp:
        score = jnp.where(blk >= nb, -2.0, score)
    sc_scr[...] = score

    s_grp = [score[8 * v:8 * v + 8, :] for v in range(nbp // 8)]
    sub = lax.broadcasted_iota(jnp.int32, (8, tq), 0)
    cur_max = (start + tq - 1) // SEL_BLOCK
    sizes = [nb * (b + 1) // NSA_RANK_SIZES for b in range(NSA_RANK_SIZES)]
    for b, n_used in enumerate(sizes):
        n_prev = sizes[b - 1] if b > 0 else 0

        @pl.when((cur_max >= n_prev) & (cur_max < n_used))
        def _():
            n_grp = n_used // 8
            cnt = [jnp.zeros((8, tq), F32) for _ in range(n_grp)]
            for i in range(n_used):
                row = sc_scr[i:i + 1, :]
                vi, ri = divmod(i, 8)
                for v in range(n_grp):
                    if v > vi:
                        hit = jnp.where(row >= s_grp[v], 1.0, 0.0)
                    elif v < vi:
                        hit = jnp.where(row > s_grp[v], 1.0, 0.0)
                    else:
                        hit = jnp.where(sub > ri, jnp.where(row >= s_grp[v], 1.0, 0.0),
                                        jnp.where(row > s_grp[v], 1.0, 0.0))
                    cnt[v] = cnt[v] + hit
            parts = [jnp.where(jnp.concatenate(cnt, axis=0) < SEL_TOPK, 0.0, -1.0)]
            if n_used < nbp:
                parts.append(jnp.full((nbp - n_used, tq), -1.0, F32))
            sc_scr[...] = jnp.concatenate(parts, axis=0)
    unsel_t = sc_scr[...].T.astype(BF16)
    lhs_scr[...] = jnp.concatenate([q4, jnp.concatenate([unsel_t] * grp, axis=0)], axis=1)

    l_scr[...] = jnp.zeros((rows, LANES), F32)
    acc_scr[...] = jnp.zeros((rows, LANES), F32)
    n_full = start // tk

    def key_operands(kt):
        koff = pl.multiple_of(kt * tk, tk)
        kk = jnp.concatenate([ksel_ref[pl.ds(koff, tk), :], et_ref[pl.ds(koff, tk), :]], axis=1)
        return koff, kk, vsel_ref[pl.ds(koff, tk), :]

    if fixed_max:
        bound = bound_ref[0, 0]

        def key_tile(kt, masked):
            koff, kk, vv = key_operands(kt)
            s = _dot_nt(lhs_scr[...], kk)
            if masked:
                s = jnp.where(koff + lax.broadcasted_iota(jnp.int32, (rows, tk), 1) <= tpos, s, NEG)
            p = jnp.exp(s - bound)
            part = p[:, 0:LANES]
            for c in range(1, tk // LANES):
                part = part + p[:, c * LANES:(c + 1) * LANES]
            l_scr[...] = l_scr[...] + part
            acc_scr[...] = acc_scr[...] + _dot(p.astype(BF16), vv)
    else:
        m_scr[...] = jnp.full((rows, LANES), NEG, F32)
        qpos = start + lax.broadcasted_iota(jnp.int32, (tq, tk), 0)

        def key_tile(kt, masked):
            koff, kk, vv = key_operands(kt)
            for g in range(grp):
                rs = slice(g * tq, (g + 1) * tq)
                s = _dot_nt(lhs_scr[rs, :], kk)
                if masked:
                    s = jnp.where(koff + lax.broadcasted_iota(jnp.int32, (tq, tk), 1) <= qpos, s, NEG)
                m_old = m_scr[rs, :]
                m_new = jnp.maximum(m_old, jnp.max(s, axis=-1, keepdims=True))
                p = jnp.exp(s - jnp.concatenate([m_new] * (tk // LANES), axis=1))
                alpha = jnp.exp(m_old - m_new)
                l_scr[rs, :] = alpha * l_scr[rs, :] + jnp.sum(p, axis=-1, keepdims=True)
                acc_scr[rs, :] = alpha * acc_scr[rs, :] + _dot(p.astype(BF16), vv)
                m_scr[rs, :] = m_new

    unroll = NSA_TILE_UNROLL if fixed_max else 1

    def tile_group(kq, carry):
        for u in range(unroll):
            key_tile(unroll * kq + u, False)
        return carry
    lax.fori_loop(0, n_full // unroll, tile_group, 0)
    done = (n_full // unroll) * unroll
    left = n_full - done
    size = unroll // 2
    while size >= 1:
        @pl.when((left & size) != 0)
        def _(done=done, size=size):
            for u in range(size):
                key_tile(done + u, False)
        done = done + (left & size)
        size //= 2
    key_tile(n_full, True)
    if fixed_max:
        o_s = acc_scr[...] / jnp.sum(l_scr[...], axis=-1, keepdims=True)
    else:
        o_s = acc_scr[...] / l_scr[...]

    wk = WINDOW + tq
    wstart = pl.multiple_of(jnp.maximum(start - WINDOW, 0), tq)
    s = _dot_nt(q4, kwin_ref[pl.ds(wstart, wk), :])
    wpos = wstart + lax.broadcasted_iota(jnp.int32, (rows, wk), 1)
    in_window = (wpos <= tpos) & (wpos > tpos - WINDOW)
    if fixed_max:
        e = jnp.where(in_window, jnp.exp(s - bound), 0.0)
    else:
        s = jnp.where(in_window, s, NEG)
        e = jnp.exp(s - jnp.max(s, axis=-1, keepdims=True))
    o_w = _dot(e.astype(BF16), vwin_ref[pl.ds(wstart, wk), :]) / jnp.sum(e, axis=-1, keepdims=True)

    gt = gate_ref[...]
    g_hi = gt.astype(BF16)
    g_lo = (gt - g_hi.astype(F32)).astype(BF16)
    out = None
    for branch, o_b in enumerate((o_c, o_s, o_w)):
        ob = o_b.astype(BF16)
        placed = _dot(ob[0:tq], pl_ref[0, 0])
        for g in range(1, grp):
            placed = placed + _dot(ob[g * tq:(g + 1) * tq], pl_ref[0, g])
        weight = _dot(g_hi, ge_ref[branch]) + _dot(g_lo, ge_ref[branch])
        out = placed * weight if out is None else out + placed * weight
    out_ref[...] = out.astype(BF16)


def _nsa_constants(t_len):
    nb = t_len // SEL_BLOCK
    nbp = max(LANES, -(-nb // LANES) * LANES)
    ncp = t_len // CMP_STRIDE
    j = np.arange(nbp)[:, None]
    n = np.arange(ncp)[None, :]
    first = (n >= SEL_RATIO * j) & (n <= SEL_RATIO * j + SEL_RATIO - 1)
    second = (n >= SEL_RATIO * j - 1) & (n <= SEL_RATIO * j + SEL_RATIO - 2)
    mt = (first.astype(np.float32) + second.astype(np.float32)) * (n < ncp - 1) * (j < nb)
    et = (np.arange(t_len)[:, None] // SEL_BLOCK == np.arange(nbp)[None, :]).astype(np.float32) * SEL_BIAS
    place = np.zeros((NSA_KV_HEADS, NSA_GROUP, LANES, NSA_GROUP * HEAD_DIM), np.float32)
    for h in range(NSA_KV_HEADS):
        for g in range(NSA_GROUP):
            place[h, g, h * HEAD_DIM + np.arange(HEAD_DIM), g * HEAD_DIM + np.arange(HEAD_DIM)] = 1.0
    gate_expand = np.zeros((3, LANES, NSA_GROUP * HEAD_DIM), np.float32)
    for g in range(NSA_GROUP):
        for branch in range(3):
            gate_expand[branch, g * 3 + branch, g * HEAD_DIM:(g + 1) * HEAD_DIM] = 1.0
    return (nb, jnp.asarray(mt, BF16), jnp.asarray(et, BF16), jnp.asarray(place, BF16),
            jnp.asarray(gate_expand, BF16))


def _nsa_attend(qpad, kv, kc, vc, gates, qk_g, batch, t_len):
    tq = NSA_TQ
    nq = t_len // tq
    rows = NSA_GROUP * tq
    nb, mt, et, place, gate_expand = _nsa_constants(t_len)
    nbp, ncp = mt.shape
    seq = lambda c: pl.BlockSpec((t_len, KV_W), lambda b, h, i: (b, c))
    bound = (HEAD_DIM ** 0.5) * jnp.max(jnp.abs(qk_g[0])) * jnp.max(jnp.abs(qk_g[1:4])) * 1.02 + 0.1

    def attend(fixed_max):
        return pl.pallas_call(
            functools.partial(_nsa_kernel, nb=nb, fixed_max=fixed_max),
            grid=(batch, NSA_KV_HEADS, nq),
            in_specs=[pl.BlockSpec((tq, NSA_GROUP * LANES), lambda b, h, i: (b * nq + i, h)),
                      seq(0), seq(1), seq(2), seq(3),
                      pl.BlockSpec((1, ncp, KV_W), lambda b, h, i: (b, 0, 0)),
                      pl.BlockSpec((1, ncp, KV_W), lambda b, h, i: (b, 0, 0)),
                      pl.BlockSpec((tq, LANES), lambda b, h, i: (b * nq + i, h)),
                      _const_spec((nbp, ncp)), _const_spec((t_len, nbp)),
                      pl.BlockSpec((1, NSA_GROUP, LANES, NSA_GROUP * HEAD_DIM), lambda b, h, i: (h, 0, 0, 0)),
                      _const_spec((3, LANES, NSA_GROUP * HEAD_DIM)),
                      pl.BlockSpec(memory_space=pltpu.SMEM)],
            out_specs=pl.BlockSpec((tq, NSA_GROUP * HEAD_DIM), lambda b, h, i: (b * nq + i, h)),
            out_shape=jax.ShapeDtypeStruct((batch * t_len, NSA_W), BF16),
            scratch_shapes=[pltpu.VMEM((nbp, tq), F32), pltpu.VMEM((rows, LANES + nbp), BF16),
                            pltpu.VMEM((rows, LANES), F32), pltpu.VMEM((rows, LANES), F32),
                            pltpu.VMEM((rows, LANES), F32)],
            compiler_params=_params("parallel", "parallel", "arbitrary"),
            name="nsa_attend",
        )(qpad, kv, kv, kv, kv, kc, vc, gates, mt, et, place, gate_expand, bound.reshape(1, 1))
    return lax.cond(bound <= NSA_MAX_FIXED_SHIFT, lambda: attend(True), lambda: attend(False))


def _outffn_kernel(x_ref, og_ref, on_ref, om_ref, wo_ref, g2_ref, wu_ref, wd_ref, y_ref):
    x1 = (x_ref[...] + _dot(og_ref[...], wo_ref[0:GLA_W, :])
          + _dot(on_ref[...], wo_ref[GLA_W:GLA_W + NSA_W, :])
          + _dot(om_ref[...], wo_ref[GLA_W + NSA_W:GLA_W + NSA_W + MLSTM_W, :]))
    h = x1 * lax.rsqrt(jnp.mean(x1 * x1, axis=-1, keepdims=True) + EPS) * g2_ref[...]
    hid = jnp.maximum(_dot(h.astype(BF16), wu_ref[...]), 0.0)
    y_ref[...] = x1 + _dot((hid * hid).astype(BF16), wd_ref[...])


def _out_ffn(x, og, on, om, wo, g2, wu, wd, tm):
    n = x.shape[0]
    row = lambda w: pl.BlockSpec((tm, w), lambda i: (i, 0))
    return pl.pallas_call(
        _outffn_kernel,
        grid=(n // tm,),
        in_specs=[row(D_MODEL), row(GLA_W), row(NSA_W), row(MLSTM_W),
                  _const_spec((D_MODEL, D_MODEL)), _const_spec((1, D_MODEL)),
                  _const_spec((D_MODEL, D_FF)), _const_spec((D_FF, D_MODEL))],
        out_specs=row(D_MODEL),
        out_shape=jax.ShapeDtypeStruct((n, D_MODEL), F32),
        compiler_params=_params("parallel"),
        name="out_ffn",
    )(x, og, on, om, wo, g2, wu, wd)


def _layer_weights(lp):
    return {'w_in': _pack_w_in(lp['w_in']), 'w_out': lp['w_out'].astype(BF16),
            'w_up': lp['w_up'].astype(BF16), 'w_down': lp['w_down'].astype(BF16),
            'cmp': _cmp_weights(lp)}


def _rows_on_lanes_to_state(a, n_slots):
    batch, _, n_rows = a.shape
    return jnp.transpose(a.reshape(batch, n_slots, NSA_KV_HEADS, HEAD_DIM, n_rows), (0, 4, 1, 2, 3))


def _prompt_layer(x, lp, lw, batch, t_len):
    d = HEAD_DIM
    zg, zn, zm = _in_proj(x, lp['norm1_g'].reshape(1, D_MODEL), lw['w_in'], 256)
    qpad, rows_t, win_t, kv, r01, gates = _nsa_prep(zn, lp['nsa_qk_g'], lp['nsa_gate_b'], 256,
                                                    seq=(batch, t_len))
    zero = lambda *s: jnp.zeros(s, F32)
    og, s_gla = _gla(zg, zero(batch, GLA_HEADS, d, d), lp['gla_a_w'], lp['gla_a_b'], lp['gla_norm_g'],
                     batch, t_len, t_len)
    om, c_m, n_m, m_m = _mlstm(zm, zero(batch, CONV_WIDTH - 1, MLSTM_W), zero(batch, MLSTM_HEADS, d, d),
                               zero(batch, MLSTM_HEADS, d), zero(batch, MLSTM_HEADS), lp, batch, t_len, t_len)
    kc, vc = _compress(r01, lw['cmp'], batch, t_len)
    on = _nsa_attend(qpad, kv, kc, vc, gates, lp['nsa_qk_g'], batch, t_len)
    y = _out_ffn(x, og, on, om, lw['w_out'], lp['norm2_g'].reshape(1, D_MODEL), lw['w_up'], lw['w_down'], 256)
    wlen = min(WINDOW, t_len)
    new_rows = _rows_on_lanes_to_state(rows_t, 4)
    win_state = _rows_on_lanes_to_state(win_t[:, :, t_len - wlen:], 2)
    conv_state = zm.reshape(batch, t_len, ZM_W)[:, t_len - (CONV_WIDTH - 1):, 0:MLSTM_W]
    return y, (new_rows, win_state, s_gla, c_m, n_m, m_m, conv_state)


GATHER_PAGES = 32
GROUPS_PER_PAGE = PAGE_SIZE // CMP_STRIDE


def _paged_cache_view(cache_nsa_kv):
    depth, n_pool = cache_nsa_kv.shape[0], cache_nsa_kv.shape[1]
    return jnp.transpose(cache_nsa_kv, (0, 1, 3, 4, 5, 2)).reshape(depth, n_pool, ROWS_W, PAGE_SIZE)


def _window_cache_view(cache_nsa_win):
    depth, batch, wb = cache_nsa_win.shape[0:3]
    return jnp.transpose(cache_nsa_win, (0, 1, 3, 4, 5, 2)).reshape(depth, batch, WIN_W, wb)


def _cmp_gather_kernel(pt_ref, ct_ref, w_ref, perm_ref, g_ref, buf, xs, sems, *, layer):
    s = pl.program_id(0)
    slot = s % 2

    def page_copies(step, dst):
        return [pltpu.make_async_copy(ct_ref.at[layer, pt_ref[step * GATHER_PAGES + p], pl.ds(0, 2 * KV_W), :],
                                      buf.at[dst, p], sems.at[dst, p]) for p in range(GATHER_PAGES)]

    @pl.when(s == 0)
    def _():
        for c in page_copies(0, 0):
            c.start()

    @pl.when(s + 1 < pl.num_programs(0))
    def _():
        for c in page_copies(s + 1, 1 - slot):
            c.start()
    for c in page_copies(s, slot):
        c.wait()

    perm = perm_ref[...]
    for p in range(GATHER_PAGES):
        xs[p] = _dot_nt(perm, buf[slot, p].astype(BF16))
    parts = []
    n_rows = GATHER_PAGES * GROUPS_PER_PAGE
    for sl in range(2):
        acc = jnp.zeros((n_rows, 2 * KV_W), F32)
        for rp in range(CMP_STRIDE // 2):
            xr = jnp.concatenate(
                [xs[:, r * GROUPS_PER_PAGE:(r + 1) * GROUPS_PER_PAGE, sl * KV_W:(sl + 1) * KV_W]
                 .reshape(n_rows, KV_W) for r in (2 * rp, 2 * rp + 1)], axis=1)
            acc = acc + _dot(xr.astype(BF16), w_ref[rp, sl])
        parts.append(acc)
    g_ref[0] = jnp.concatenate([parts[0][:, 0:KV_W], parts[1][:, 0:KV_W],
                                parts[0][:, KV_W:2 * KV_W], parts[1][:, KV_W:2 * KV_W]], axis=1)


def _cmp_gather(ct, layer, pt_flat, w_ab, batch, n_pages):
    assert n_pages % GATHER_PAGES == 0
    steps = n_pages // GATHER_PAGES
    rows = GATHER_PAGES * GROUPS_PER_PAGE
    w4 = w_ab.reshape(CMP_STRIDE, 2, KV_W, 2, 2, KV_W)
    w_rs = jnp.stack([w4[:, s, :, :, s, :] for s in range(2)], axis=1).reshape(CMP_STRIDE, 2, KV_W, 2 * KV_W)
    w_rs = jnp.transpose(w_rs.reshape(CMP_STRIDE // 2, 2, 2, KV_W, 2 * KV_W), (0, 2, 1, 3, 4))
    w_rs = w_rs.reshape(CMP_STRIDE // 2, 2, 2 * KV_W, 2 * KV_W)
    i = np.arange(PAGE_SIZE)
    perm = np.zeros((PAGE_SIZE, PAGE_SIZE), np.float32)
    perm[i, (i % GROUPS_PER_PAGE) * CMP_STRIDE + i // GROUPS_PER_PAGE] = 1.0
    perm = jnp.asarray(perm, BF16)
    grid_spec = pltpu.PrefetchScalarGridSpec(
        num_scalar_prefetch=1,
        grid=(batch * steps,),
        in_specs=[pl.BlockSpec(memory_space=pl.ANY),
                  pl.BlockSpec((CMP_STRIDE // 2, 2, 2 * KV_W, 2 * KV_W), lambda s, pt: (0, 0, 0, 0)),
                  pl.BlockSpec((PAGE_SIZE, PAGE_SIZE), lambda s, pt: (0, 0))],
        out_specs=pl.BlockSpec((1, rows, CMP_G), lambda s, pt: (s // steps, s % steps, 0)),
        scratch_shapes=[pltpu.VMEM((2, GATHER_PAGES, 2 * KV_W, PAGE_SIZE), F32),
                        pltpu.VMEM((GATHER_PAGES, PAGE_SIZE, 2 * KV_W), F32),
                        pltpu.SemaphoreType.DMA((2, GATHER_PAGES))],
    )
    return pl.pallas_call(
        functools.partial(_cmp_gather_kernel, layer=layer),
        grid_spec=grid_spec,
        out_shape=jax.ShapeDtypeStruct((batch, n_pages * GROUPS_PER_PAGE, CMP_G), F32),
        compiler_params=_params("arbitrary"),
        name="nsa_cmp_gather",
    )(pt_flat, ct, w_rs, perm)


def _to_col(row, n):
    eye = (lax.broadcasted_iota(jnp.int32, (n, n), 0) == lax.broadcasted_iota(jnp.int32, (n, n), 1))
    return jnp.sum(jnp.where(eye, jnp.broadcast_to(row, (n, n)), 0.0), axis=1, keepdims=True)


def _dec_cmp_kernel(g_ref, q_ref, wc_ref, wnew_ref, wab_ref, pe_ref, b1_ref, w2_ref, b2_ref, g1_ref,
                    ones_ref, m_ref, oc_ref, ow_ref, idx_ref, wout_ref, *, n_groups, nb, cur):
    nh, grp = NSA_KV_HEADS, NSA_GROUP
    q8 = q_ref[0]
    kc, vc = _cmp_finish(g_ref[0], n_groups, wab_ref, pe_ref, b1_ref, w2_ref, b2_ref, g1_ref, ones_ref)
    sc = _dot_nt(q8, kc.astype(BF16))
    vis = lax.broadcasted_iota(jnp.int32, sc.shape, 1) < n_groups - 1
    s_m = jnp.where(vis, sc, NEG)
    e = jnp.where(vis, jnp.exp(s_m - jnp.max(s_m, axis=-1, keepdims=True)), 0.0)
    p_c = e / jnp.maximum(jnp.sum(e, axis=-1, keepdims=True), 1e-30)
    oc_ref[0] = _dot(p_c.astype(BF16), vc.astype(BF16))

    nsp = m_ref.shape[1]
    pg = jnp.concatenate([jnp.sum(p_c[h * grp:(h + 1) * grp], axis=0, keepdims=True) for h in range(nh)]
                         + [jnp.zeros((8 - nh, n_groups), F32)], axis=0)
    h1 = pg.astype(BF16)
    r1 = pg - h1.astype(F32)
    h2 = r1.astype(BF16)
    h3 = (r1 - h2.astype(F32)).astype(BF16)
    m = m_ref[...]
    p_sel = _dot(h1, m) + _dot(h2, m) + _dot(h3, m)
    blk = lax.broadcasted_iota(jnp.int32, (1, nsp), 1)
    forced = (blk == 0) | (blk == cur) | (blk == cur - 1)
    ii = lax.broadcasted_iota(jnp.int32, (nsp, nsp), 0)
    jj = lax.broadcasted_iota(jnp.int32, (nsp, nsp), 1)
    slot = lax.broadcasted_iota(jnp.int32, (SEL_TOPK, nsp), 0).astype(F32)
    blk_f = lax.broadcasted_iota(jnp.int32, (SEL_TOPK, nsp), 1).astype(F32)
    for h in range(nh):
        score = jnp.where(blk > cur, -1.0, jnp.where(forced, FORCE_SCORE, p_sel[h:h + 1, :]))
        score = jnp.where(blk >= nb, -2.0, score)
        s_col = _to_col(score, nsp)
        ge = jnp.where(s_col >= score, 1.0, 0.0)
        gt = jnp.where(s_col > score, 1.0, 0.0)
        rank = jnp.sum(jnp.where(jj > ii, ge, gt), axis=0, keepdims=True)
        sel = (rank < SEL_TOPK).astype(F32)
        before = jnp.sum(jnp.where(ii < jj, _to_col(sel, nsp), 0.0), axis=0, keepdims=True)
        onehot = jnp.where((before == slot) & (sel > 0.5), 1.0, 0.0)
        idx = jnp.sum(onehot * blk_f, axis=1, keepdims=True)
        idx_ref[0, h] = jnp.broadcast_to(idx, (SEL_TOPK, LANES)).astype(jnp.int32)

    wt = wc_ref[0, 0]
    wb = wt.shape[1]
    wnew = wnew_ref[0]
    s = _dot(q8, wt[0:KV_W].astype(BF16))
    s = jnp.where(lax.broadcasted_iota(jnp.int32, s.shape, 1) > wb - WINDOW, s, NEG)
    qf = q8.astype(F32)
    s_new = jnp.sum(qf * wnew[:, 0:KV_W].astype(BF16).astype(F32), axis=-1, keepdims=True)
    mx = jnp.maximum(jnp.max(s, axis=-1, keepdims=True), s_new)
    e = jnp.exp(s - mx)
    e_new = jnp.exp(s_new - mx)
    num = _dot_nt(e.astype(BF16), wt[KV_W:2 * KV_W].astype(BF16)) + e_new * wnew[:, KV_W:2 * KV_W]
    ow_ref[0] = num / (jnp.sum(e, axis=-1, keepdims=True) + e_new)
    lane = lax.broadcasted_iota(jnp.int32, wt.shape, 1)
    wout_ref[0] = jnp.where(lane == wb - 1, _to_col(wnew, 2 * KV_W), pltpu.roll(wt, wb - 1, 1))


def _dec_constants(past):
    n_groups = past // CMP_STRIDE
    nb = past // SEL_BLOCK + 1
    nsp = -(-nb // LANES) * LANES
    j = np.arange(nsp)[None, :]
    n = np.arange(n_groups)[:, None]
    first = (n >= SEL_RATIO * j) & (n <= SEL_RATIO * j + SEL_RATIO - 1)
    second = (n >= SEL_RATIO * j - 1) & (n <= SEL_RATIO * j + SEL_RATIO - 2)
    m = (first.astype(np.float32) + second.astype(np.float32)) * (n < n_groups - 1) * (j < nb)
    return n_groups, nb, jnp.asarray(m, BF16)


def _dec_cmp_attn(gsum, q8, wt, layer, win_new, cw, batch, past):
    n_groups, nb, m = _dec_constants(past)
    nsp = m.shape[1]
    wb = wt.shape[3]
    assert wb == WINDOW
    per_b = lambda shape: pl.BlockSpec((1,) + shape, lambda b: (b,) + (0,) * len(shape))
    kern = functools.partial(_dec_cmp_kernel, n_groups=n_groups, nb=nb, cur=past // SEL_BLOCK)
    return pl.pallas_call(
        kern,
        grid=(batch,),
        in_specs=[per_b((n_groups, CMP_G)), per_b((NSA_HEADS, LANES)),
                  pl.BlockSpec((1, 1, 2 * KV_W, wb), lambda b: (layer, b, 0, 0)),
                  per_b((1, 2 * KV_W))] + _cmp_specs() + [_const_spec((n_groups, nsp))],
        out_specs=[per_b((NSA_HEADS, LANES)), per_b((NSA_HEADS, LANES)),
                   per_b((NSA_KV_HEADS, SEL_TOPK, LANES)), per_b((2 * KV_W, wb))],
        out_shape=[jax.ShapeDtypeStruct((batch, NSA_HEADS, LANES), F32),
                   jax.ShapeDtypeStruct((batch, NSA_HEADS, LANES), F32),
                   jax.ShapeDtypeStruct((batch, NSA_KV_HEADS, SEL_TOPK, LANES), jnp.int32),
                   jax.ShapeDtypeStruct((batch, 2 * KV_W, wb), F32)],
        compiler_params=_params("parallel"),
        name="nsa_decode_cmp",
    )(gsum, q8, wt, win_new, *cw, m)


def _dec_sel_kernel(pt_ref, idx_ref, q_ref, new_ref, oc_ref, ow_ref, gate_ref, ct_ref, out_ref,
                    kv_buf, sems, *, layer, n_pages, n_cache_blocks):
    b = pl.program_id(0)
    buf_slot = b % 2
    nh, nq, topk = NSA_KV_HEADS, NSA_HEADS, SEL_TOPK
    half = PAGE_SIZE // SEL_BLOCK
    nblk = nh * topk

    def page_copies(seq, dst):
        copies = []
        for j in range(nblk):
            blk = jnp.minimum(idx_ref[seq * nblk + j], n_cache_blocks - 1)
            page = pt_ref[seq * n_pages + blk // half]
            copies.append(pltpu.make_async_copy(ct_ref.at[layer, page, pl.ds(2 * KV_W, 2 * KV_W), :],
                                                kv_buf.at[dst, j], sems.at[dst, j]))
        return copies

    @pl.when(b == 0)
    def _():
        for c in page_copies(0, 0):
            c.start()

    @pl.when(b + 1 < pl.num_programs(0))
    def _():
        for c in page_copies(b + 1, 1 - buf_slot):
            c.start()
    for c in page_copies(b, buf_slot):
        c.wait()

    q8 = q_ref[0]
    new = new_ref[0]
    s_new = jnp.sum(q8 * new[:, 2 * KV_W:3 * KV_W].astype(BF16).astype(F32), axis=-1, keepdims=True)
    v_new = new[:, 3 * KV_W:4 * KV_W]
    qb = q8.astype(BF16)
    page_half = lax.broadcasted_iota(jnp.int32, (1, PAGE_SIZE), 1) // SEL_BLOCK
    o_heads = []
    for h in range(nh):
        s_parts = []
        m = s_new
        for k in range(topk):
            j = h * topk + k
            blk = idx_ref[b * nblk + j]
            s_k = _dot(qb, kv_buf[buf_slot, j, 0:KV_W, :].astype(BF16))
            keep = jnp.where(blk < n_cache_blocks, 0.0, NEG)
            s_k = s_k + jnp.where(page_half == blk % half, keep, NEG)
            s_parts.append(s_k)
            m = jnp.maximum(m, jnp.max(s_k, axis=-1, keepdims=True))
        e_new = jnp.exp(s_new - m)
        den = e_new
        num = e_new * v_new
        for k in range(topk):
            e = jnp.exp(s_parts[k] - m)
            den = den + jnp.sum(e, axis=-1, keepdims=True)
            num = num + _dot_nt(e.astype(BF16), kv_buf[buf_slot, h * topk + k, KV_W:2 * KV_W, :].astype(BF16))
        o_heads.append(num / den)
    row = lax.broadcasted_iota(jnp.int32, (nq, KV_W), 0)
    o_s = jnp.where(row < NSA_GROUP, o_heads[0], o_heads[1])
    gt = gate_ref[0]
    out_ref[0] = gt[:, 0:1] * oc_ref[0] + gt[:, 1:2] * o_s + gt[:, 2:3] * ow_ref[0]


def _dec_sel_attn(ct, layer, pt_flat, idx_flat, q8, rows_new, o_c, o_w, gates, batch, n_pages):
    half = PAGE_SIZE // SEL_BLOCK
    nh, grp, nq = NSA_KV_HEADS, NSA_GROUP, NSA_HEADS
    g3 = gates.reshape(batch, nh, LANES)[:, :, 0:3 * grp].reshape(batch, nq, 3)
    g8 = jnp.concatenate([g3, jnp.zeros((batch, nq, LANES - 3), F32)], axis=-1)
    new8 = jnp.broadcast_to(rows_new.reshape(batch, 1, ROWS_W), (batch, nq, ROWS_W))

    per_b = lambda w: pl.BlockSpec((1, nq, w), lambda b, pt, idx: (b, 0, 0))
    grid_spec = pltpu.PrefetchScalarGridSpec(
        num_scalar_prefetch=2,
        grid=(batch,),
        in_specs=[per_b(LANES), per_b(ROWS_W), per_b(LANES), per_b(LANES), per_b(LANES),
                  pl.BlockSpec(memory_space=pl.ANY)],
        out_specs=per_b(LANES),
        scratch_shapes=[pltpu.VMEM((2, nh * SEL_TOPK, 2 * KV_W, PAGE_SIZE), F32),
                        pltpu.SemaphoreType.DMA((2, nh * SEL_TOPK))],
    )
    kern = functools.partial(_dec_sel_kernel, layer=layer, n_pages=n_pages, n_cache_blocks=n_pages * half)
    return pl.pallas_call(
        kern,
        grid_spec=grid_spec,
        out_shape=jax.ShapeDtypeStruct((batch, nq, LANES), F32),
        compiler_params=_params("arbitrary"),
        name="nsa_decode_sel",
    )(pt_flat, idx_flat, q8.astype(F32), new8, o_c, o_w, g8, ct)


def _sample_layer(x, lp, lw, layer, ct, wt, gla_s0, c0, n0, m0, conv0, page_table):
    batch, n_pages = page_table.shape
    past = n_pages * PAGE_SIZE
    d = HEAD_DIM
    c = SCAN_CHUNK_DECODE
    zg, zn, zm = _in_proj(x, lp['norm1_g'].reshape(1, D_MODEL), lw['w_in'], batch)
    qpad, rows, win, _, _, gates = _nsa_prep(zn, lp['nsa_qk_g'], lp['nsa_gate_b'], batch)
    pad = lambda z: jnp.pad(z[:, None, :], ((0, 0), (0, c - 1), (0, 0))).reshape(batch * c, z.shape[-1])
    first = lambda o: o.reshape(batch, c, o.shape[-1])[:, 0]
    og, s_gla = _gla(pad(zg), gla_s0, lp['gla_a_w'], lp['gla_a_b'], lp['gla_norm_g'], batch, c, 1, c)
    om, c_m, n_m, m_m = _mlstm(pad(zm), conv0, c0, n0, m0, lp, batch, c, 1, c)
    pt_flat = page_table.reshape(-1)
    gsum = _cmp_gather(ct, layer, pt_flat, lw['cmp'][0], batch, n_pages)
    q8 = qpad.reshape(batch, NSA_HEADS, LANES)
    o_c, o_w, idx, win_t = _dec_cmp_attn(gsum, q8, wt, layer, win.reshape(batch, 1, 2 * KV_W), lw['cmp'],
                                         batch, past)
    o_n = _dec_sel_attn(ct, layer, pt_flat, idx[:, :, :, 0].reshape(-1), q8, rows, o_c, o_w, gates,
                        batch, n_pages)
    on = jnp.stack([o_n[:, h * NSA_GROUP:(h + 1) * NSA_GROUP, h * d:(h + 1) * d]
                    for h in range(NSA_KV_HEADS)], axis=1)
    on = on.reshape(batch, NSA_W).astype(BF16)
    y = _out_ffn(x, first(og), on, first(om), lw['w_out'], lp['norm2_g'].reshape(1, D_MODEL),
                 lw['w_up'], lw['w_down'], batch)
    new_rows = rows.reshape(batch, 1, 4, NSA_KV_HEADS, d)
    win_state = _rows_on_lanes_to_state(win_t, 2)
    conv_state = jnp.concatenate([conv0.astype(F32), zm[:, None, 0:MLSTM_W]], axis=1)[:, 1:]
    return y, (new_rows, win_state, s_gla, c_m, n_m, m_m, conv_state)


def kernel(x_prompt, x_sample, cache_nsa_kv, cache_nsa_win, state_gla, state_mlstm_c, state_mlstm_n,
           state_mlstm_m, state_mlstm_conv, page_table, norm1_g, w_in, gla_a_w, gla_a_b, gla_norm_g,
           nsa_qk_g, nsa_gate_b, cmp_pe, cmp_w1, cmp_b1, cmp_w2, cmp_b2, ml_conv_w, ml_conv_b, ml_wq, ml_wk,
           ml_gate_b, ml_norm_g, ml_skip, w_out, norm2_g, w_up, w_down):
    bp, t_len, _ = x_prompt.shape
    n_dec = x_sample.shape[0]
    depth = w_in.shape[0]
    x_p = x_prompt.reshape(bp * t_len, D_MODEL)
    x_s = x_sample.reshape(n_dec, D_MODEL)
    states_p, states_s = [], []
    ct = _paged_cache_view(cache_nsa_kv)
    wt = _window_cache_view(cache_nsa_win)
    for l in range(depth):
        lp = {'norm1_g': norm1_g[l], 'w_in': w_in[l], 'gla_a_w': gla_a_w[l], 'gla_a_b': gla_a_b[l],
              'gla_norm_g': gla_norm_g[l], 'nsa_qk_g': nsa_qk_g[l], 'nsa_gate_b': nsa_gate_b[l],
              'cmp_pe': cmp_pe[l], 'cmp_w1': cmp_w1[l], 'cmp_b1': cmp_b1[l], 'cmp_w2': cmp_w2[l],
              'cmp_b2': cmp_b2[l], 'ml_conv_w': ml_conv_w[l], 'ml_conv_b': ml_conv_b[l], 'ml_wq': ml_wq[l],
              'ml_wk': ml_wk[l], 'ml_gate_b': ml_gate_b[l], 'ml_norm_g': ml_norm_g[l], 'ml_skip': ml_skip[l],
              'w_out': w_out[l], 'norm2_g': norm2_g[l], 'w_up': w_up[l], 'w_down': w_down[l]}
        lw = _layer_weights(lp)
        x_p, st_p = _prompt_layer(x_p, lp, lw, bp, t_len)
        x_s, st_s = _sample_layer(x_s, lp, lw, l, ct, wt, state_gla[l],
                                  state_mlstm_c[l], state_mlstm_n[l], state_mlstm_m[l], state_mlstm_conv[l],
                                  page_table)
        states_p.append(st_p)
        states_s.append(st_s)
    outs = [x_p.reshape(bp, t_len, D_MODEL), x_s.reshape(n_dec, 1, D_MODEL)]
    for i in range(7):
        outs.append(jnp.stack([s[i] for s in states_p]))
        outs.append(jnp.stack([s[i] for s in states_s]))
    return tuple(outs)
```

```python
import functools

import numpy as np
import jax
import jax.numpy as jnp
from jax import lax
from jax.experimental import pallas as pl
from jax.experimental.pallas import tpu as pltpu

F32 = jnp.float32
BF16 = jnp.bfloat16

D_MODEL = 1024
HEAD_DIM = 64
PAGE_SIZE = 128
GLA_HEADS = 4
GLA_LOWRANK = 16
GLA_TAU = 16.0
NSA_HEADS = 8
NSA_KV_HEADS = 2
NSA_GROUP = NSA_HEADS // NSA_KV_HEADS
CMP_BLOCK = 32
CMP_STRIDE = 16
CMP_HIDDEN = 64
SEL_BLOCK = 64
SEL_RATIO = SEL_BLOCK // CMP_STRIDE
SEL_TOPK = 16
WINDOW = 512
FORCE_SCORE = 1.0e4
MLSTM_HEADS = 4
CONV_WIDTH = 4
D_FF = 4 * D_MODEL
EPS = 1e-6

GLA_W = GLA_HEADS * HEAD_DIM
NSA_W = NSA_HEADS * HEAD_DIM
KV_W = NSA_KV_HEADS * HEAD_DIM
MLSTM_W = MLSTM_HEADS * HEAD_DIM

LANES = 128
ZG_W = 4 * GLA_W + LANES
GATE_W = NSA_KV_HEADS * LANES
ZN_W = NSA_W + 6 * KV_W + GATE_W
ZM_W = 3 * MLSTM_W + LANES
ROWS_W = 4 * KV_W
WIN_W = 2 * KV_W
QPAD_W = NSA_HEADS * LANES

SCAN_CHUNK = 128
SCAN_CHUNK_DECODE = 32
GLA_SUB = 32
NSA_TQ = 128
NSA_TK = 256
NSA_TK_FIXED = 512
NSA_MAX_FIXED_SHIFT = 40.0
NSA_RANK_SIZES = 4
NSA_TILE_UNROLL = 8
NEG = -1.0e30
SEL_BIAS = 29952.0

VMEM_LIMIT = 56 * 1024 * 1024


def _dot(a, b):
    return jnp.dot(a, b, preferred_element_type=F32)


def _dot_hi(a, b):
    return jnp.dot(a, b, preferred_element_type=F32, precision=lax.Precision.HIGHEST)


def _dot_nt(a, b):
    return lax.dot_general(a, b, (((1,), (1,)), ((), ())), preferred_element_type=F32)


def _dot_nt_hi(a, b):
    return lax.dot_general(a, b, (((1,), (1,)), ((), ())), preferred_element_type=F32,
                           precision=lax.Precision.HIGHEST)


def _dot_tn_hi(a, b):
    return lax.dot_general(a, b, (((0,), (0,)), ((), ())), preferred_element_type=F32,
                           precision=lax.Precision.HIGHEST)


def _sigmoid(x):
    return 1.0 / (1.0 + jnp.exp(-x))


def _log_sigmoid(x):
    return jnp.minimum(x, 0.0) - jnp.log(1.0 + jnp.exp(-jnp.abs(x)))


def _group_sum(x, ones_bd):
    hi = x.astype(BF16)
    lo = (x - hi.astype(F32)).astype(BF16)
    return _dot(hi, ones_bd) + _dot(lo, ones_bd)


def _group_mean_sq(x, ones_bd):
    return _group_sum(x * x, ones_bd) * (1.0 / HEAD_DIM)


def _params(*sem):
    return pltpu.CompilerParams(dimension_semantics=sem, vmem_limit_bytes=VMEM_LIMIT)


def _const_spec(shape):
    nd = len(shape)
    return pl.BlockSpec(shape, lambda *_: (0,) * nd)


def _inproj_kernel(x_ref, g_ref, w_ref, zg_ref, zn_ref, zm_ref):
    x = x_ref[...]
    h = x * lax.rsqrt(jnp.mean(x * x, axis=-1, keepdims=True) + EPS) * g_ref[...]
    hb = h.astype(BF16)
    zg_ref[...] = _dot(hb, w_ref[:, 0:ZG_W])
    zn_ref[...] = _dot(hb, w_ref[:, ZG_W:ZG_W + ZN_W])
    zm_ref[...] = _dot(hb, w_ref[:, ZG_W + ZN_W:ZG_W + ZN_W + ZM_W])


def _in_proj(x, g, w, tm):
    n = x.shape[0]
    zw = ZG_W + ZN_W + ZM_W
    return pl.pallas_call(
        _inproj_kernel,
        grid=(n // tm,),
        in_specs=[pl.BlockSpec((tm, D_MODEL), lambda i: (i, 0)),
                  _const_spec((1, D_MODEL)),
                  _const_spec((D_MODEL, zw))],
        out_specs=[pl.BlockSpec((tm, ZG_W), lambda i: (i, 0)),
                   pl.BlockSpec((tm, ZN_W), lambda i: (i, 0)),
                   pl.BlockSpec((tm, ZM_W), lambda i: (i, 0))],
        out_shape=[jax.ShapeDtypeStruct((n, ZG_W), F32),
                   jax.ShapeDtypeStruct((n, ZN_W), F32),
                   jax.ShapeDtypeStruct((n, ZM_W), F32)],
        compiler_params=_params("parallel"),
        name="in_proj",
    )(x, g, w)


def _pack_w_in(w_in):
    def cols(a, b):
        return w_in[:, a:b]

    def zeros(n):
        return jnp.zeros((D_MODEL, n), w_in.dtype)
    o_nsa = 4 * GLA_W + GLA_LOWRANK
    o_ng = o_nsa + NSA_W + 6 * KV_W
    o_ml = o_ng + 3 * NSA_HEADS
    o_mi = o_ml + 2 * MLSTM_W
    o_mo = o_mi + 2 * MLSTM_HEADS
    n_gate = 3 * NSA_GROUP
    parts = [cols(0, o_nsa), zeros(LANES - GLA_LOWRANK),
             cols(o_nsa, o_ng),
             cols(o_ng, o_ng + n_gate), zeros(LANES - n_gate),
             cols(o_ng + n_gate, o_ml), zeros(LANES - n_gate),
             cols(o_ml, o_mi), cols(o_mo, o_mo + MLSTM_W), cols(o_mi, o_mo),
             zeros(LANES - 2 * MLSTM_HEADS)]
    return jnp.concatenate(parts, axis=1).astype(BF16)


def _nsa_prep_kernel(zn_ref, gq_ref, gr_ref, gw_ref, gb_ref, ones_ref, place_ref,
                     qpad_ref, rows_ref, win_ref, kv_ref, r01_ref, gate_ref, *, transposed):
    ones_bd = ones_ref[...]
    q = zn_ref[:, 0:NSA_W]
    qn = q * lax.rsqrt(_group_mean_sq(q, ones_bd) + EPS) * gq_ref[...]
    qpad_ref[...] = _dot(qn.astype(BF16), place_ref[...]).astype(BF16)

    r = zn_ref[:, NSA_W:NSA_W + ROWS_W]
    col = lax.broadcasted_iota(jnp.int32, r.shape, 1)
    rn = r * lax.rsqrt(_group_mean_sq(r, ones_bd) + EPS) * gr_ref[...]
    rows = jnp.where((col >= 2 * KV_W) & (col < 3 * KV_W), rn, r)
    if transposed:
        rows_ref[0] = rows.T
    else:
        rows_ref[...] = rows
    r01_ref[...] = rows[:, 0:2 * KV_W].astype(BF16)

    w = zn_ref[:, NSA_W + ROWS_W:NSA_W + ROWS_W + WIN_W]
    colw = lax.broadcasted_iota(jnp.int32, w.shape, 1)
    wn = w * lax.rsqrt(_group_mean_sq(w, ones_bd[0:WIN_W, 0:WIN_W]) + EPS) * gw_ref[...]
    win = jnp.where(colw < KV_W, wn, w)
    if transposed:
        win_ref[0] = win.T
    else:
        win_ref[...] = win
    kv_ref[:, 0:2 * KV_W] = rows[:, 2 * KV_W:4 * KV_W].astype(BF16)
    kv_ref[:, 2 * KV_W:4 * KV_W] = win.astype(BF16)

    gate_ref[...] = _sigmoid(zn_ref[:, NSA_W + ROWS_W + WIN_W:ZN_W] + gb_ref[...])


def _nsa_prep(zn, qk_g, gate_b, tm, seq=None):
    n = zn.shape[0]
    scale = HEAD_DIM ** -0.5
    gq = (jnp.tile(qk_g[0], NSA_HEADS) * scale).reshape(1, NSA_W)
    gr = jnp.tile(qk_g[2], ROWS_W // HEAD_DIM).reshape(1, ROWS_W)
    gw = jnp.tile(qk_g[3], WIN_W // HEAD_DIM).reshape(1, WIN_W)
    n_gate = 3 * NSA_GROUP
    gpad = jnp.zeros((LANES - n_gate,), F32)
    gb = jnp.concatenate([gate_b[0:n_gate], gpad, gate_b[n_gate:], gpad]).reshape(1, GATE_W)
    grp = np.arange(NSA_W) // HEAD_DIM
    ones_bd = jnp.asarray(grp[:, None] == grp[None, :], BF16)
    src = np.arange(NSA_W)
    head, d = src // HEAD_DIM, src % HEAD_DIM
    dst = head * LANES + (head // NSA_GROUP) * HEAD_DIM + d
    place = np.zeros((NSA_W, QPAD_W), np.float32)
    place[src, dst] = 1.0
    place = jnp.asarray(place, BF16)
    row = lambda w: pl.BlockSpec((tm, w), lambda i: (i, 0))
    if seq is None:
        state_spec = row
        state_shape = lambda w: jax.ShapeDtypeStruct((n, w), F32)
    else:
        batch, t_len = seq
        nblk = t_len // tm
        state_spec = lambda w: pl.BlockSpec((1, w, tm), lambda i: (i // nblk, 0, i % nblk))
        state_shape = lambda w: jax.ShapeDtypeStruct((batch, w, t_len), F32)
    return pl.pallas_call(
        functools.partial(_nsa_prep_kernel, transposed=seq is not None),
        grid=(n // tm,),
        in_specs=[row(ZN_W), _const_spec((1, NSA_W)), _const_spec((1, ROWS_W)),
                  _const_spec((1, WIN_W)), _const_spec((1, GATE_W)),
                  _const_spec((NSA_W, NSA_W)), _const_spec((NSA_W, QPAD_W))],
        out_specs=[row(QPAD_W), state_spec(ROWS_W), state_spec(WIN_W), row(4 * KV_W), row(2 * KV_W),
                   row(GATE_W)],
        out_shape=[jax.ShapeDtypeStruct((n, QPAD_W), BF16),
                   state_shape(ROWS_W),
                   state_shape(WIN_W),
                   jax.ShapeDtypeStruct((n, 4 * KV_W), BF16),
                   jax.ShapeDtypeStruct((n, 2 * KV_W), BF16),
                   jax.ShapeDtypeStruct((n, GATE_W), F32)],
        compiler_params=_params("parallel"),
        name="nsa_prep",
    )(zn, gq, gr, gw, gb, ones_bd, place)


def _load_block_diag(bd_scr, heads_ref):
    nh, d = heads_ref.shape[1], heads_ref.shape[2]
    bd_scr[...] = jnp.zeros(bd_scr.shape, F32)
    for h in range(nh):
        bd_scr[h * d:(h + 1) * d, h * d:(h + 1) * d] = heads_ref[0, h]


def _store_block_diag(heads_ref, bd_scr):
    nh, d = heads_ref.shape[1], heads_ref.shape[2]
    for h in range(nh):
        heads_ref[0, h] = bd_scr[h * d:(h + 1) * d, h * d:(h + 1) * d]


def _gla_init(zg_ref, s0_ref, aw_ref, ab_ref, ng_ref, tri_ref, ones_ref, og_ref, sout_ref, s_scr):
    @pl.when(pl.program_id(1) == 0)
    def _():
        _load_block_diag(s_scr, s0_ref)


def _gla_fin(zg_ref, s0_ref, aw_ref, ab_ref, ng_ref, tri_ref, ones_ref, og_ref, sout_ref, s_scr):
    @pl.when(pl.program_id(1) == pl.num_programs(1) - 1)
    def _():
        _store_block_diag(sout_ref, s_scr)


def _gla_body(zg_ref, s0_ref, aw_ref, ab_ref, ng_ref, tri_ref, ones_ref, og_ref, sout_ref, s_scr,
              *, t_valid, t_pad):
    i = pl.program_id(1)
    c = zg_ref.shape[0]
    d = HEAD_DIM
    q = zg_ref[:, 0:GLA_W] * (d ** -0.5)
    k = zg_ref[:, GLA_W:2 * GLA_W]
    v = zg_ref[:, 2 * GLA_W:3 * GLA_W]
    r = zg_ref[:, 3 * GLA_W:4 * GLA_W]
    ga = zg_ref[:, 4 * GLA_W:ZG_W]
    g = _log_sigmoid(_dot(ga.astype(BF16), aw_ref[...]) + ab_ref[...]) * (1.0 / GLA_TAU)
    if t_valid < t_pad:
        valid = (i * c + lax.broadcasted_iota(jnp.int32, (c, 1), 0)) < t_valid
        g = jnp.where(valid, g, 0.0)
        k = jnp.where(valid, k, 0.0)
    bcum = _dot_hi(tri_ref[...], g)
    b_end = bcum[c - 1:c, :]
    nh, w, sub = GLA_HEADS, GLA_W, min(GLA_SUB, c)
    lane_head = lax.broadcasted_iota(jnp.int32, (1, w), 1) // d
    s_bd = s_scr[...]
    o = _dot((q * jnp.exp(bcum)).astype(BF16), s_bd.astype(BF16))
    vb = v.astype(BF16)
    o_sub = []
    for j in range(c // sub):
        lo, hi = j * sub, (j + 1) * sub
        base = bcum[lo - 1:lo, :] if j > 0 else jnp.zeros((1, w), F32)
        qt = q[lo:hi] * jnp.exp(bcum[lo:hi] - base)
        kt = (k[0:hi] * jnp.exp(base - bcum[0:hi])).astype(BF16)
        qs = jnp.concatenate([jnp.where(lane_head == h, qt, 0.0) for h in range(nh)], axis=0)
        sc = _dot_nt(qs.astype(BF16), kt)
        t_row = lo + (lax.broadcasted_iota(jnp.int32, (nh * sub, hi), 0) % sub)
        sc = jnp.where(lax.broadcasted_iota(jnp.int32, (nh * sub, hi), 1) <= t_row, sc, 0.0)
        ov = _dot(sc.astype(BF16), vb[0:hi])
        o_j = jnp.where(lane_head == 0, ov[0:sub], 0.0)
        for h in range(1, nh):
            o_j = jnp.where(lane_head == h, ov[h * sub:(h + 1) * sub], o_j)
        o_sub.append(o_j)
    o = o + jnp.concatenate(o_sub, axis=0)
    last = (lax.broadcasted_iota(jnp.int32, (c, LANES), 0) == c - 1).astype(F32)
    decay_col = jnp.exp(_dot_tn_hi(bcum, last))
    k_hat = k * jnp.exp(b_end - bcum)
    same_head = (lax.broadcasted_iota(jnp.int32, (w, w), 0) // d) == (lax.broadcasted_iota(jnp.int32, (w, w), 1) // d)
    s_scr[...] = (jnp.concatenate([decay_col] * (w // LANES), axis=1) * s_bd
                  + jnp.where(same_head, _dot_tn_hi(k_hat, v), 0.0))
    on = o * lax.rsqrt(_group_mean_sq(o, ones_ref[...]) + EPS) * ng_ref[...]
    og_ref[...] = (on * (r * _sigmoid(r))).astype(BF16)


def _run_scans(scans, batch, nblk, name):
    counts = [(len(s['in_specs']), len(s['out_specs']), len(s['scratch'])) for s in scans]
    n_in, n_out = sum(c[0] for c in counts), sum(c[1] for c in counts)

    def kernel(*refs):
        views, o_in, o_out, o_scr = [], 0, n_in, n_in + n_out
        for ni, no, ns in counts:
            views.append(refs[o_in:o_in + ni] + refs[o_out:o_out + no] + refs[o_scr:o_scr + ns])
            o_in, o_out, o_scr = o_in + ni, o_out + no, o_scr + ns
        for phase in range(3):
            for s, view in zip(scans, views):
                s['phases'][phase](*view)
    outs = pl.pallas_call(
        kernel,
        grid=(batch, nblk),
        in_specs=[x for s in scans for x in s['in_specs']],
        out_specs=[x for s in scans for x in s['out_specs']],
        out_shape=[x for s in scans for x in s['out_shape']],
        scratch_shapes=[x for s in scans for x in s['scratch']],
        compiler_params=_params("parallel", "arbitrary"),
        name=name,
    )(*[x for s in scans for x in s['operands']])
    split, o = [], 0
    for _, no, _ in counts:
        split.append(outs[o:o + no])
        o += no
    return split


def _gla_scan(zg, s0, a_w, a_b, norm_g, batch, t_pad, t_valid, c):
    nblk = t_pad // c
    aw = jnp.concatenate([a_w, jnp.zeros((LANES - GLA_LOWRANK, GLA_W), F32)], axis=0).astype(BF16)
    tri = jnp.asarray(np.tril(np.ones((c, c), np.float32)))
    grp = np.arange(GLA_W) // HEAD_DIM
    ones_bd = jnp.asarray(grp[:, None] == grp[None, :], BF16)
    state = pl.BlockSpec((1, GLA_HEADS, HEAD_DIM, HEAD_DIM), lambda b, i: (b, 0, 0, 0))
    return dict(
        phases=(_gla_init, functools.partial(_gla_body, t_valid=t_valid, t_pad=t_pad), _gla_fin),
        in_specs=[pl.BlockSpec((c, ZG_W), lambda b, i: (b * nblk + i, 0)), state,
                  _const_spec((LANES, GLA_W)), _const_spec((1, GLA_W)), _const_spec((1, GLA_W)),
                  _const_spec((c, c)), _const_spec((GLA_W, GLA_W))],
        out_specs=[pl.BlockSpec((c, GLA_W), lambda b, i: (b * nblk + i, 0)), state],
        out_shape=[jax.ShapeDtypeStruct((batch * t_pad, GLA_W), BF16),
                   jax.ShapeDtypeStruct((batch, GLA_HEADS, HEAD_DIM, HEAD_DIM), F32)],
        scratch=[pltpu.VMEM((GLA_W, GLA_W), F32)],
        operands=(zg, s0, aw, a_b.reshape(1, GLA_W), jnp.tile(norm_g, GLA_HEADS).reshape(1, GLA_W), tri, ones_bd))


def _gla(zg, s0, a_w, a_b, norm_g, batch, t_pad, t_valid, c=SCAN_CHUNK):
    (out,) = _run_scans([_gla_scan(zg, s0, a_w, a_b, norm_g, batch, t_pad, t_valid, c)], batch, t_pad // c,
                        "gla_scan")
    return out


def _mlstm_init(zm_ref, conv0_ref, c0_ref, n0_ref, m0_ref, cw_ref, cb_ref, wq_ref, wk_ref, gb_ref,
                ng_ref, skip_ref, tri_ref, ones_ref, om_ref, cout_ref, nout_ref, mout_ref,
                c_scr, n_scr, m_scr, ext_scr):
    @pl.when(pl.program_id(1) == 0)
    def _():
        _load_block_diag(c_scr, c0_ref)
        n_scr[...] = n0_ref[0]
        m_scr[...] = m0_ref[0]
        ext_scr[0:8, :] = conv0_ref[0]


def _mlstm_fin(zm_ref, conv0_ref, c0_ref, n0_ref, m0_ref, cw_ref, cb_ref, wq_ref, wk_ref, gb_ref,
               ng_ref, skip_ref, tri_ref, ones_ref, om_ref, cout_ref, nout_ref, mout_ref,
               c_scr, n_scr, m_scr, ext_scr):
    @pl.when(pl.program_id(1) == pl.num_programs(1) - 1)
    def _():
        _store_block_diag(cout_ref, c_scr)
        nout_ref[0] = n_scr[...]
        mout_ref[0] = m_scr[...]


def _mlstm_body(zm_ref, conv0_ref, c0_ref, n0_ref, m0_ref, cw_ref, cb_ref, wq_ref, wk_ref, gb_ref,
                ng_ref, skip_ref, tri_ref, ones_ref, om_ref, cout_ref, nout_ref, mout_ref,
                c_scr, n_scr, m_scr, ext_scr, *, t_valid, t_pad):
    i = pl.program_id(1)
    c = zm_ref.shape[0]
    d = HEAD_DIM
    nh = MLSTM_HEADS
    mu = zm_ref[:, 0:MLSTM_W]
    ext_scr[8:8 + c, :] = mu
    u_conv = cb_ref[...] + mu * cw_ref[CONV_WIDTH - 1:CONV_WIDTH, :]
    for j in range(1, CONV_WIDTH):
        u_conv = u_conv + ext_scr[8 - j:8 - j + c, :] * cw_ref[CONV_WIDTH - 1 - j:CONV_WIDTH - j, :]
    ext_scr[0:8, :] = mu[c - 8:c, :]
    u_act = u_conv * _sigmoid(u_conv)
    ub = u_act.astype(BF16)
    q = _dot(ub, wq_ref[...])
    k = _dot(ub, wk_ref[...]) * (d ** -0.5)
    v = zm_ref[:, MLSTM_W:2 * MLSTM_W]
    og = zm_ref[:, 2 * MLSTM_W:3 * MLSTM_W]
    gz = zm_ref[:, 3 * MLSTM_W:ZM_W] + gb_ref[...]
    lane = lax.broadcasted_iota(jnp.int32, (c, LANES), 1)
    x = jnp.where(lane < nh, gz, _log_sigmoid(gz))
    if t_valid < t_pad:
        valid = (i * c + lax.broadcasted_iota(jnp.int32, (c, 1), 0)) < t_valid
        x = jnp.where(valid, x, jnp.where(lane < nh, NEG, 0.0))
    fc = _dot_hi(tri_ref[...], x)
    x = jnp.where(lane < nh, x, fc)
    sel = (lax.broadcasted_iota(jnp.int32, (8, LANES), 0)
           == lax.broadcasted_iota(jnp.int32, (8, LANES), 1)).astype(F32)
    xt = _dot_nt_hi(sel, x)
    row_i = lax.broadcasted_iota(jnp.int32, (c, c), 0)
    col_i = lax.broadcasted_iota(jnp.int32, (c, c), 1)
    w = MLSTM_W
    lane_head = lax.broadcasted_iota(jnp.int32, (1, w), 1) // d
    c_bd = c_scr[...]
    n_all = n_scr[0:1, :]
    m_all = m_scr[0:1, :]
    qb = q.astype(BF16)
    qs = jnp.concatenate([jnp.where(lane_head == h, q, 0.0) for h in range(nh)], axis=0)
    qk_all = _dot_nt(qs.astype(BF16), k.astype(BF16))
    qn_sum = _group_sum(q * n_all, ones_ref[...])
    zc, zr = jnp.zeros((c, w), F32), jnp.zeros((1, w), F32)
    mt_l, ws_l, rs_l, wl_l, wc_l, mn_l = zc, zc, zc, zc, zr, zr
    p_parts = []
    for h in range(nh):
        i_col, f_col = x[:, h:h + 1], x[:, nh + h:nh + h + 1]
        i_row, f_row = xt[h:h + 1, :], xt[nh + h:nh + h + 1, :]
        a = f_col + m_all[:, h * d:h * d + 1]
        dmat = jnp.where(col_i <= row_i, f_col - f_row + i_row, NEG)
        m_t = jnp.maximum(a, jnp.max(dmat, axis=-1, keepdims=True))
        p_h = qk_all[h * c:(h + 1) * c] * jnp.exp(dmat - m_t)
        p_parts.append(p_h)
        m_new = m_t[c - 1:c, :]
        on_head = lane_head == h
        mt_l = jnp.where(on_head, m_t, mt_l)
        ws_l = jnp.where(on_head, jnp.exp(a - m_t), ws_l)
        rs_l = jnp.where(on_head, jnp.sum(p_h, axis=-1, keepdims=True), rs_l)
        wl_l = jnp.where(on_head, jnp.exp(f_col[c - 1:c, :] - f_col + i_col - m_new), wl_l)
        wc_l = jnp.where(on_head, jnp.exp(a[c - 1:c, :] - m_new), wc_l)
        mn_l = jnp.where(on_head, m_new, mn_l)
    nv = _dot(jnp.concatenate(p_parts, axis=0).astype(BF16), v.astype(BF16))
    num = jnp.where(lane_head == 0, nv[0:c], 0.0)
    for h in range(1, nh):
        num = jnp.where(lane_head == h, nv[h * c:(h + 1) * c], num)
    num = num + ws_l * _dot(qb, c_bd.astype(BF16))
    den = rs_l + ws_l * qn_sum
    hm = num / jnp.maximum(jnp.abs(den), jnp.exp(-mt_l))
    kw = k * wl_l
    same_head = (lax.broadcasted_iota(jnp.int32, (w, w), 0) // d) == (lax.broadcasted_iota(jnp.int32, (w, w), 1) // d)
    c_scr[...] = wc_l * c_bd + jnp.where(same_head, _dot_tn_hi(kw, v), 0.0)
    n_scr[0:1, :] = wc_l * n_all + jnp.sum(kw, axis=0, keepdims=True)
    m_scr[0:1, :] = mn_l
    hn = hm * lax.rsqrt(_group_mean_sq(hm, ones_ref[...]) + EPS) * ng_ref[...]
    om_ref[...] = (_sigmoid(og) * (hn + skip_ref[...] * u_act)).astype(BF16)


def _block_diag_heads(w):
    nh, d, _ = w.shape
    eye = jnp.eye(nh, dtype=w.dtype)
    return jnp.einsum('hde,hg->hdge', w, eye).reshape(nh * d, nh * d)


def _mlstm_scan(zm, conv0, c0, n0, m0, lp, batch, t_pad, t_valid, c):
    nblk = t_pad // c
    nh, d = MLSTM_HEADS, HEAD_DIM
    conv0p = jnp.concatenate([jnp.zeros((batch, 8 - (CONV_WIDTH - 1), MLSTM_W), F32), conv0.astype(F32)], axis=1)
    pad7 = jnp.zeros((batch, 7, MLSTM_W), F32)
    n0p = jnp.concatenate([n0.reshape(batch, 1, MLSTM_W), pad7], axis=1)
    m0p = jnp.concatenate([jnp.repeat(m0, d, axis=1).reshape(batch, 1, MLSTM_W), pad7], axis=1)
    gb = jnp.concatenate([lp['ml_gate_b'][0], lp['ml_gate_b'][1],
                          jnp.zeros((LANES - 2 * nh,), F32)]).reshape(1, LANES)
    tri = jnp.asarray(np.tril(np.ones((c, c), np.float32)))
    grp = np.arange(MLSTM_W) // d
    ones_bd = jnp.asarray(grp[:, None] == grp[None, :], BF16)
    per_b = lambda shape: pl.BlockSpec((1,) + shape, lambda b, i: (b,) + (0,) * len(shape))
    return dict(
        phases=(_mlstm_init, functools.partial(_mlstm_body, t_valid=t_valid, t_pad=t_pad), _mlstm_fin),
        in_specs=[pl.BlockSpec((c, ZM_W), lambda b, i: (b * nblk + i, 0)),
                  per_b((8, MLSTM_W)), per_b((nh, d, d)), per_b((8, MLSTM_W)), per_b((8, MLSTM_W)),
                  _const_spec((CONV_WIDTH, MLSTM_W)), _const_spec((1, MLSTM_W)),
                  _const_spec((MLSTM_W, MLSTM_W)), _const_spec((MLSTM_W, MLSTM_W)),
                  _const_spec((1, LANES)), _const_spec((1, MLSTM_W)), _const_spec((1, MLSTM_W)),
                  _const_spec((c, c)), _const_spec((MLSTM_W, MLSTM_W))],
        out_specs=[pl.BlockSpec((c, MLSTM_W), lambda b, i: (b * nblk + i, 0)),
                   per_b((nh, d, d)), per_b((8, MLSTM_W)), per_b((8, MLSTM_W))],
        out_shape=[jax.ShapeDtypeStruct((batch * t_pad, MLSTM_W), BF16),
                   jax.ShapeDtypeStruct((batch, nh, d, d), F32),
                   jax.ShapeDtypeStruct((batch, 8, MLSTM_W), F32),
                   jax.ShapeDtypeStruct((batch, 8, MLSTM_W), F32)],
        scratch=[pltpu.VMEM((MLSTM_W, MLSTM_W), F32), pltpu.VMEM((8, MLSTM_W), F32),
                 pltpu.VMEM((8, MLSTM_W), F32), pltpu.VMEM((8 + c, MLSTM_W), F32)],
        operands=(zm, conv0p, c0, n0p, m0p, lp['ml_conv_w'], lp['ml_conv_b'].reshape(1, MLSTM_W),
                  _block_diag_heads(lp['ml_wq']).astype(BF16), _block_diag_heads(lp['ml_wk']).astype(BF16),
                  gb, jnp.tile(lp['ml_norm_g'], nh).reshape(1, MLSTM_W), lp['ml_skip'].reshape(1, MLSTM_W),
                  tri, ones_bd))


def _mlstm_states(outs, batch):
    om, c_f, n_f, m_f = outs
    nh, d = MLSTM_HEADS, HEAD_DIM
    return om, c_f, n_f[:, 0].reshape(batch, nh, d), m_f[:, 0].reshape(batch, nh, d)[:, :, 0]


def _mlstm(zm, conv0, c0, n0, m0, lp, batch, t_pad, t_valid, c=SCAN_CHUNK):
    (outs,) = _run_scans([_mlstm_scan(zm, conv0, c0, n0, m0, lp, batch, t_pad, t_valid, c)], batch, t_pad // c,
                         "mlstm_scan")
    return _mlstm_states(outs, batch)


def _gla_mlstm(zg, zm, gla_s0, conv0, c0, n0, m0, lp, batch, t_pad, t_valid, c=SCAN_CHUNK):
    g_out, m_out = _run_scans(
        [_gla_scan(zg, gla_s0, lp['gla_a_w'], lp['gla_a_b'], lp['gla_norm_g'], batch, t_pad, t_valid, c),
         _mlstm_scan(zm, conv0, c0, n0, m0, lp, batch, t_pad, t_valid, c)], batch, t_pad // c, "gla_mlstm_scan")
    return tuple(g_out) + _mlstm_states(m_out, batch)


CMP_IN = CMP_STRIDE * 2 * KV_W
CMP_G = 2 * 2 * KV_W


def _cmp_weights(lp):
    w1 = lp['cmp_w1'].reshape(2, 2, CMP_STRIDE, HEAD_DIM, CMP_HIDDEN)
    eye = jnp.eye(2, dtype=F32)
    w_ab = jnp.einsum('sarde,st,hg->rshdatge', w1, eye, eye).reshape(CMP_IN, CMP_G)
    pe = lp['cmp_pe'].reshape(2, 2, CMP_STRIDE, 1, HEAD_DIM)
    pe = jnp.broadcast_to(jnp.transpose(pe, (1, 2, 0, 3, 4)),
                          (2, CMP_STRIDE, 2, NSA_KV_HEADS, HEAD_DIM)).reshape(2, CMP_IN)
    pe8 = jnp.concatenate([pe, jnp.zeros((6, CMP_IN), F32)], axis=0)
    b1 = jnp.broadcast_to(lp['cmp_b1'][:, None, :], (2, NSA_KV_HEADS, CMP_HIDDEN)).reshape(1, 2 * KV_W)
    w2 = jnp.einsum('sed,st,hg->shetgd', lp['cmp_w2'], eye, eye).reshape(2 * KV_W, 2 * KV_W)
    b2 = jnp.broadcast_to(lp['cmp_b2'][:, None, :], (2, NSA_KV_HEADS, HEAD_DIM)).reshape(1, 2 * KV_W)
    g1 = jnp.tile(lp['nsa_qk_g'][1], NSA_KV_HEADS).reshape(1, KV_W)
    grp = np.arange(KV_W) // HEAD_DIM
    ones_bd = jnp.asarray(grp[:, None] == grp[None, :], BF16)
    return (w_ab.astype(BF16), pe8.astype(BF16), b1, w2.astype(BF16), b2, g1, ones_bd)


def _cmp_finish(gsum, n_rows, wab_ref, pe_ref, b1_ref, w2_ref, b2_ref, g1_ref, ones_ref):
    half = 2 * KV_W
    g_pe = _dot(pe_ref[...], wab_ref[...])
    bias = g_pe[0:1, 0:half] + g_pe[1:2, half:CMP_G] + b1_ref[...]
    hid = gsum[:, 0:half] + pltpu.roll(gsum[:, half:CMP_G], n_rows - 1, 0) + bias
    act = hid * _sigmoid(hid)
    cmp = _dot(act.astype(BF16), w2_ref[...]) + b2_ref[...]
    kc = cmp[:, 0:KV_W]
    kc = kc * lax.rsqrt(_group_mean_sq(kc, ones_ref[...]) + EPS) * g1_ref[...]
    return kc, cmp[:, KV_W:half]


def _cmp_kernel(r_ref, wab_ref, pe_ref, b1_ref, w2_ref, b2_ref, g1_ref, ones_ref, kc_ref, vc_ref, *, n_rows):
    gsum = _dot(r_ref[0], wab_ref[...])
    kc, vc = _cmp_finish(gsum, n_rows, wab_ref, pe_ref, b1_ref, w2_ref, b2_ref, g1_ref, ones_ref)
    kc_ref[0] = kc.astype(BF16)
    vc_ref[0] = vc.astype(BF16)


def _cmp_specs():
    half = 2 * KV_W
    return [_const_spec((CMP_IN, CMP_G)), _const_spec((8, CMP_IN)), _const_spec((1, half)),
            _const_spec((half, half)), _const_spec((1, half)), _const_spec((1, KV_W)),
            _const_spec((KV_W, KV_W))]


def _compress(r01, cw, batch, t_len):
    n16 = t_len // CMP_STRIDE
    x = r01.reshape(batch, n16, CMP_IN)
    blk = lambda w: pl.BlockSpec((1, n16, w), lambda b: (b, 0, 0))
    return pl.pallas_call(
        functools.partial(_cmp_kernel, n_rows=n16),
        grid=(batch,),
        in_specs=[blk(CMP_IN)] + _cmp_specs(),
        out_specs=[blk(KV_W), blk(KV_W)],
        out_shape=[jax.ShapeDtypeStruct((batch, n16, KV_W), BF16)] * 2,
        compiler_params=_params("parallel"),
        name="nsa_compress",
    )(x, *cw)


def _nsa_kernel(qpad_ref, ksel_ref, vsel_ref, kwin_ref, vwin_ref, kc_ref, vc_ref, gate_ref, mt_ref,
                et_ref, pl_ref, ge_ref, bound_ref, out_ref, sc_scr, lhs_scr, m_scr, l_scr, acc_scr, *, nb, fixed_max):
    qi = pl.program_id(2)
    tq, grp = NSA_TQ, NSA_GROUP
    tk = NSA_TK_FIXED if fixed_max else NSA_TK
    rows = grp * tq
    nbp, ncp = mt_ref.shape
    start = qi * tq
    q4 = jnp.concatenate([qpad_ref[:, g * LANES:(g + 1) * LANES] for g in range(grp)], axis=0)
    tpos = start + (lax.broadcasted_iota(jnp.int32, (rows, 1), 0) & (tq - 1))

    sc = _dot_nt(q4, kc_ref[0])
    ccol = lax.broadcasted_iota(jnp.int32, (rows, ncp), 1)
    vis = (ccol * CMP_STRIDE + (CMP_BLOCK - 1)) <= tpos
    if fixed_max:
        bound = bound_ref[0, 0]
        e = jnp.where(vis, jnp.exp(sc - bound), 0.0)
        den = jnp.sum(e, axis=-1, keepdims=True)
        p_c = e / jnp.where(den > 0.0, den, 1.0)
    else:
        s_m = jnp.where(vis, sc, NEG)
        e = jnp.where(vis, jnp.exp(s_m - jnp.max(s_m, axis=-1, keepdims=True)), 0.0)
        p_c = e / jnp.maximum(jnp.sum(e, axis=-1, keepdims=True), 1e-30)
    o_c = _dot(p_c.astype(BF16), vc_ref[0])

    pg = p_c[0:tq]
    for g in range(1, grp):
        pg = pg + p_c[g * tq:(g + 1) * tq]
    h1 = pg.astype(BF16)
    r1 = pg - h1.astype(F32)
    h2 = r1.astype(BF16)
    h3 = (r1 - h2.astype(F32)).astype(BF16)
    mt = mt_ref[...]
    p_sel = _dot_nt(mt, h1) + _dot_nt(mt, h2) + _dot_nt(mt, h3)
    blk = lax.broadcasted_iota(jnp.int32, (nbp, tq), 0)
    cur = (start + lax.broadcasted_iota(jnp.int32, (nbp, tq), 1)) // SEL_BLOCK
    forced = (blk == 0) | (blk == cur) | (blk == cur - 1)
    score = jnp.where(blk > cur, -1.0, jnp.where(forced, FORCE_SCORE, p_sel))
    if nb < nbp:
        score = jnp.where(blk >= nb, -2.0, score)
    sc_scr[...] = score

    s_grp = [score[8 * v:8 * v + 8, :] for v in range(nbp // 8)]
    sub = lax.broadcasted_iota(jnp.int32, (8, tq), 0)
    cur_max = (start + tq - 1) // SEL_BLOCK
    sizes = [nb * (b + 1) // NSA_RANK_SIZES for b in range(NSA_RANK_SIZES)]
    for b, n_used in enumerate(sizes):
        n_prev = sizes[b - 1] if b > 0 else 0

        @pl.when((cur_max >= n_prev) & (cur_max < n_used))
        def _():
            n_grp = n_used // 8
            cnt = [jnp.zeros((8, tq), F32) for _ in range(n_grp)]
            for i in range(n_used):
                row = sc_scr[i:i + 1, :]
                vi, ri = divmod(i, 8)
                for v in range(n_grp):
                    if v > vi:
                        hit = jnp.where(row >= s_grp[v], 1.0, 0.0)
                    elif v < vi:
                        hit = jnp.where(row > s_grp[v], 1.0, 0.0)
                    else:
                        hit = jnp.where(sub > ri, jnp.where(row >= s_grp[v], 1.0, 0.0),
                                        jnp.where(row > s_grp[v], 1.0, 0.0))
                    cnt[v] = cnt[v] + hit
            parts = [jnp.where(jnp.concatenate(cnt, axis=0) < SEL_TOPK, 0.0, -1.0)]
            if n_used < nbp:
                parts.append(jnp.full((nbp - n_used, tq), -1.0, F32))
            sc_scr[...] = jnp.concatenate(parts, axis=0)
    unsel_t = sc_scr[...].T.astype(BF16)
    lhs_scr[...] = jnp.concatenate([q4, jnp.concatenate([unsel_t] * grp, axis=0)], axis=1)

    l_scr[...] = jnp.zeros((rows, LANES), F32)
    acc_scr[...] = jnp.zeros((rows, LANES), F32)
    n_full = start // tk

    def key_operands(kt):
        koff = pl.multiple_of(kt * tk, tk)
        kk = jnp.concatenate([ksel_ref[pl.ds(koff, tk), :], et_ref[pl.ds(koff, tk), :]], axis=1)
        return koff, kk, vsel_ref[pl.ds(koff, tk), :]

    if fixed_max:
        bound = bound_ref[0, 0]

        def key_tile(kt, masked):
            koff, kk, vv = key_operands(kt)
            s = _dot_nt(lhs_scr[...], kk)
            if masked:
                s = jnp.where(koff + lax.broadcasted_iota(jnp.int32, (rows, tk), 1) <= tpos, s, NEG)
            p = jnp.exp(s - bound)
            part = p[:, 0:LANES]
            for c in range(1, tk // LANES):
                part = part + p[:, c * LANES:(c + 1) * LANES]
            l_scr[...] = l_scr[...] + part
            acc_scr[...] = acc_scr[...] + _dot(p.astype(BF16), vv)
    else:
        m_scr[...] = jnp.full((rows, LANES), NEG, F32)
        qpos = start + lax.broadcasted_iota(jnp.int32, (tq, tk), 0)

        def key_tile(kt, masked):
            koff, kk, vv = key_operands(kt)
            for g in range(grp):
                rs = slice(g * tq, (g + 1) * tq)
                s = _dot_nt(lhs_scr[rs, :], kk)
                if masked:
                    s = jnp.where(koff + lax.broadcasted_iota(jnp.int32, (tq, tk), 1) <= qpos, s, NEG)
                m_old = m_scr[rs, :]
                m_new = jnp.maximum(m_old, jnp.max(s, axis=-1, keepdims=True))
                p = jnp.exp(s - jnp.concatenate([m_new] * (tk // LANES), axis=1))
                alpha = jnp.exp(m_old - m_new)
                l_scr[rs, :] = alpha * l_scr[rs, :] + jnp.sum(p, axis=-1, keepdims=True)
                acc_scr[rs, :] = alpha * acc_scr[rs, :] + _dot(p.astype(BF16), vv)
                m_scr[rs, :] = m_new

    unroll = NSA_TILE_UNROLL if fixed_max else 1

    def tile_group(kq, carry):
        for u in range(unroll):
            key_tile(unroll * kq + u, False)
        return carry
    lax.fori_loop(0, n_full // unroll, tile_group, 0)
    done = (n_full // unroll) * unroll
    left = n_full - done
    size = unroll // 2
    while size >= 1:
        @pl.when((left & size) != 0)
        def _(done=done, size=size):
            for u in range(size):
                key_tile(done + u, False)
        done = done + (left & size)
        size //= 2
    key_tile(n_full, True)
    if fixed_max:
        o_s = acc_scr[...] / jnp.sum(l_scr[...], axis=-1, keepdims=True)
    else:
        o_s = acc_scr[...] / l_scr[...]

    wk = WINDOW + tq
    wstart = pl.multiple_of(jnp.maximum(start - WINDOW, 0), tq)
    s = _dot_nt(q4, kwin_ref[pl.ds(wstart, wk), :])
    wpos = wstart + lax.broadcasted_iota(jnp.int32, (rows, wk), 1)
    in_window = (wpos <= tpos) & (wpos > tpos - WINDOW)
    if fixed_max:
        e = jnp.where(in_window, jnp.exp(s - bound), 0.0)
    else:
        s = jnp.where(in_window, s, NEG)
        e = jnp.exp(s - jnp.max(s, axis=-1, keepdims=True))
    o_w = _dot(e.astype(BF16), vwin_ref[pl.ds(wstart, wk), :]) / jnp.sum(e, axis=-1, keepdims=True)

    gt = gate_ref[...]
    g_hi = gt.astype(BF16)
    g_lo = (gt - g_hi.astype(F32)).astype(BF16)
    out = None
    for branch, o_b in enumerate((o_c, o_s, o_w)):
        ob = o_b.astype(BF16)
        placed = _dot(ob[0:tq], pl_ref[0, 0])
        for g in range(1, grp):
            placed = placed + _dot(ob[g * tq:(g + 1) * tq], pl_ref[0, g])
        weight = _dot(g_hi, ge_ref[branch]) + _dot(g_lo, ge_ref[branch])
        out = placed * weight if out is None else out + placed * weight
    out_ref[...] = out.astype(BF16)


def _nsa_constants(t_len):
    nb = t_len // SEL_BLOCK
    nbp = max(LANES, -(-nb // LANES) * LANES)
    ncp = t_len // CMP_STRIDE
    j = np.arange(nbp)[:, None]
    n = np.arange(ncp)[None, :]
    first = (n >= SEL_RATIO * j) & (n <= SEL_RATIO * j + SEL_RATIO - 1)
    second = (n >= SEL_RATIO * j - 1) & (n <= SEL_RATIO * j + SEL_RATIO - 2)
    mt = (first.astype(np.float32) + second.astype(np.float32)) * (n < ncp - 1) * (j < nb)
    et = (np.arange(t_len)[:, None] // SEL_BLOCK == np.arange(nbp)[None, :]).astype(np.float32) * SEL_BIAS
    place = np.zeros((NSA_KV_HEADS, NSA_GROUP, LANES, NSA_GROUP * HEAD_DIM), np.float32)
    for h in range(NSA_KV_HEADS):
        for g in range(NSA_GROUP):
            place[h, g, h * HEAD_DIM + np.arange(HEAD_DIM), g * HEAD_DIM + np.arange(HEAD_DIM)] = 1.0
    gate_expand = np.zeros((3, LANES, NSA_GROUP * HEAD_DIM), np.float32)
    for g in range(NSA_GROUP):
        for branch in range(3):
            gate_expand[branch, g * 3 + branch, g * HEAD_DIM:(g + 1) * HEAD_DIM] = 1.0
    return (nb, jnp.asarray(mt, BF16), jnp.asarray(et, BF16), jnp.asarray(place, BF16),
            jnp.asarray(gate_expand, BF16))


def _nsa_attend(qpad, kv, kc, vc, gates, qk_g, batch, t_len):
    tq = NSA_TQ
    nq = t_len // tq
    rows = NSA_GROUP * tq
    nb, mt, et, place, gate_expand = _nsa_constants(t_len)
    nbp, ncp = mt.shape
    seq = lambda c: pl.BlockSpec((t_len, KV_W), lambda b, h, i: (b, c))
    bound = (HEAD_DIM ** 0.5) * jnp.max(jnp.abs(qk_g[0])) * jnp.max(jnp.abs(qk_g[1:4])) * 1.02 + 0.1

    def attend(fixed_max):
        return pl.pallas_call(
            functools.partial(_nsa_kernel, nb=nb, fixed_max=fixed_max),
            grid=(batch, NSA_KV_HEADS, nq),
            in_specs=[pl.BlockSpec((tq, NSA_GROUP * LANES), lambda b, h, i: (b * nq + i, h)),
                      seq(0), seq(1), seq(2), seq(3),
                      pl.BlockSpec((1, ncp, KV_W), lambda b, h, i: (b, 0, 0)),
                      pl.BlockSpec((1, ncp, KV_W), lambda b, h, i: (b, 0, 0)),
                      pl.BlockSpec((tq, LANES), lambda b, h, i: (b * nq + i, h)),
                      _const_spec((nbp, ncp)), _const_spec((t_len, nbp)),
                      pl.BlockSpec((1, NSA_GROUP, LANES, NSA_GROUP * HEAD_DIM), lambda b, h, i: (h, 0, 0, 0)),
                      _const_spec((3, LANES, NSA_GROUP * HEAD_DIM)),
                      pl.BlockSpec(memory_space=pltpu.SMEM)],
            out_specs=pl.BlockSpec((tq, NSA_GROUP * HEAD_DIM), lambda b, h, i: (b * nq + i, h)),
            out_shape=jax.ShapeDtypeStruct((batch * t_len, NSA_W), BF16),
            scratch_shapes=[pltpu.VMEM((nbp, tq), F32), pltpu.VMEM((rows, LANES + nbp), BF16),
                            pltpu.VMEM((rows, LANES), F32), pltpu.VMEM((rows, LANES), F32),
                            pltpu.VMEM((rows, LANES), F32)],
            compiler_params=_params("parallel", "parallel", "arbitrary"),
            name="nsa_attend",
        )(qpad, kv, kv, kv, kv, kc, vc, gates, mt, et, place, gate_expand, bound.reshape(1, 1))
    return lax.cond(bound <= NSA_MAX_FIXED_SHIFT, lambda: attend(True), lambda: attend(False))


def _outffn_kernel(x_ref, og_ref, on_ref, om_ref, wo_ref, g2_ref, wu_ref, wd_ref, y_ref):
    x1 = (x_ref[...] + _dot(og_ref[...], wo_ref[0:GLA_W, :])
          + _dot(on_ref[...], wo_ref[GLA_W:GLA_W + NSA_W, :])
          + _dot(om_ref[...], wo_ref[GLA_W + NSA_W:GLA_W + NSA_W + MLSTM_W, :]))
    h = x1 * lax.rsqrt(jnp.mean(x1 * x1, axis=-1, keepdims=True) + EPS) * g2_ref[...]
    hid = jnp.maximum(_dot(h.astype(BF16), wu_ref[...]), 0.0)
    y_ref[...] = x1 + _dot((hid * hid).astype(BF16), wd_ref[...])


def _out_ffn(x, og, on, om, wo, g2, wu, wd, tm):
    n = x.shape[0]
    row = lambda w: pl.BlockSpec((tm, w), lambda i: (i, 0))
    return pl.pallas_call(
        _outffn_kernel,
        grid=(n // tm,),
        in_specs=[row(D_MODEL), row(GLA_W), row(NSA_W), row(MLSTM_W),
                  _const_spec((D_MODEL, D_MODEL)), _const_spec((1, D_MODEL)),
                  _const_spec((D_MODEL, D_FF)), _const_spec((D_FF, D_MODEL))],
        out_specs=row(D_MODEL),
        out_shape=jax.ShapeDtypeStruct((n, D_MODEL), F32),
        compiler_params=_params("parallel"),
        name="out_ffn",
    )(x, og, on, om, wo, g2, wu, wd)


def _layer_weights(lp):
    return {'w_in': _pack_w_in(lp['w_in']), 'w_out': lp['w_out'].astype(BF16),
            'w_up': lp['w_up'].astype(BF16), 'w_down': lp['w_down'].astype(BF16),
            'cmp': _cmp_weights(lp)}


def _rows_on_lanes_to_state(a, n_slots):
    batch, _, n_rows = a.shape
    return jnp.transpose(a.reshape(batch, n_slots, NSA_KV_HEADS, HEAD_DIM, n_rows), (0, 4, 1, 2, 3))


def _prompt_layer(x, lp, lw, batch, t_len):
    d = HEAD_DIM
    zg, zn, zm = _in_proj(x, lp['norm1_g'].reshape(1, D_MODEL), lw['w_in'], 256)
    qpad, rows_t, win_t, kv, r01, gates = _nsa_prep(zn, lp['nsa_qk_g'], lp['nsa_gate_b'], 256,
                                                    seq=(batch, t_len))
    zero = lambda *s: jnp.zeros(s, F32)
    og, s_gla, om, c_m, n_m, m_m = _gla_mlstm(
        zg, zm, zero(batch, GLA_HEADS, d, d), zero(batch, CONV_WIDTH - 1, MLSTM_W),
        zero(batch, MLSTM_HEADS, d, d), zero(batch, MLSTM_HEADS, d), zero(batch, MLSTM_HEADS), lp,
        batch, t_len, t_len)
    kc, vc = _compress(r01, lw['cmp'], batch, t_len)
    on = _nsa_attend(qpad, kv, kc, vc, gates, lp['nsa_qk_g'], batch, t_len)
    y = _out_ffn(x, og, on, om, lw['w_out'], lp['norm2_g'].reshape(1, D_MODEL), lw['w_up'], lw['w_down'], 256)
    wlen = min(WINDOW, t_len)
    new_rows = _rows_on_lanes_to_state(rows_t, 4)
    win_state = _rows_on_lanes_to_state(win_t[:, :, t_len - wlen:], 2)
    conv_state = zm.reshape(batch, t_len, ZM_W)[:, t_len - (CONV_WIDTH - 1):, 0:MLSTM_W]
    return y, (new_rows, win_state, s_gla, c_m, n_m, m_m, conv_state)


GATHER_PAGES = 32
GROUPS_PER_PAGE = PAGE_SIZE // CMP_STRIDE


def _paged_cache_view(cache_nsa_kv):
    depth, n_pool = cache_nsa_kv.shape[0], cache_nsa_kv.shape[1]
    return jnp.transpose(cache_nsa_kv, (0, 1, 3, 4, 5, 2)).reshape(depth, n_pool, ROWS_W, PAGE_SIZE)


def _window_cache_view(cache_nsa_win):
    depth, batch, wb = cache_nsa_win.shape[0:3]
    return jnp.transpose(cache_nsa_win, (0, 1, 3, 4, 5, 2)).reshape(depth, batch, WIN_W, wb)


def _cmp_gather_kernel(pt_ref, ct_ref, w_ref, perm_ref, g_ref, buf, xs, sems, *, layer):
    s = pl.program_id(0)
    slot = s % 2

    def page_copies(step, dst):
        return [pltpu.make_async_copy(ct_ref.at[layer, pt_ref[step * GATHER_PAGES + p], pl.ds(0, 2 * KV_W), :],
                                      buf.at[dst, p], sems.at[dst, p]) for p in range(GATHER_PAGES)]

    @pl.when(s == 0)
    def _():
        for c in page_copies(0, 0):
            c.start()

    @pl.when(s + 1 < pl.num_programs(0))
    def _():
        for c in page_copies(s + 1, 1 - slot):
            c.start()
    for c in page_copies(s, slot):
        c.wait()

    perm = perm_ref[...]
    for p in range(GATHER_PAGES):
        xs[p] = _dot_nt(perm, buf[slot, p].astype(BF16))
    parts = []
    n_rows = GATHER_PAGES * GROUPS_PER_PAGE
    for sl in range(2):
        acc = jnp.zeros((n_rows, 2 * KV_W), F32)
        for rp in range(CMP_STRIDE // 2):
            xr = jnp.concatenate(
                [xs[:, r * GROUPS_PER_PAGE:(r + 1) * GROUPS_PER_PAGE, sl * KV_W:(sl + 1) * KV_W]
                 .reshape(n_rows, KV_W) for r in (2 * rp, 2 * rp + 1)], axis=1)
            acc = acc + _dot(xr.astype(BF16), w_ref[rp, sl])
        parts.append(acc)
    g_ref[0] = jnp.concatenate([parts[0][:, 0:KV_W], parts[1][:, 0:KV_W],
                                parts[0][:, KV_W:2 * KV_W], parts[1][:, KV_W:2 * KV_W]], axis=1)


def _cmp_gather(ct, layer, pt_flat, w_ab, batch, n_pages):
    assert n_pages % GATHER_PAGES == 0
    steps = n_pages // GATHER_PAGES
    rows = GATHER_PAGES * GROUPS_PER_PAGE
    w4 = w_ab.reshape(CMP_STRIDE, 2, KV_W, 2, 2, KV_W)
    w_rs = jnp.stack([w4[:, s, :, :, s, :] for s in range(2)], axis=1).reshape(CMP_STRIDE, 2, KV_W, 2 * KV_W)
    w_rs = jnp.transpose(w_rs.reshape(CMP_STRIDE // 2, 2, 2, KV_W, 2 * KV_W), (0, 2, 1, 3, 4))
    w_rs = w_rs.reshape(CMP_STRIDE // 2, 2, 2 * KV_W, 2 * KV_W)
    i = np.arange(PAGE_SIZE)
    perm = np.zeros((PAGE_SIZE, PAGE_SIZE), np.float32)
    perm[i, (i % GROUPS_PER_PAGE) * CMP_STRIDE + i // GROUPS_PER_PAGE] = 1.0
    perm = jnp.asarray(perm, BF16)
    grid_spec = pltpu.PrefetchScalarGridSpec(
        num_scalar_prefetch=1,
        grid=(batch * steps,),
        in_specs=[pl.BlockSpec(memory_space=pl.ANY),
                  pl.BlockSpec((CMP_STRIDE // 2, 2, 2 * KV_W, 2 * KV_W), lambda s, pt: (0, 0, 0, 0)),
                  pl.BlockSpec((PAGE_SIZE, PAGE_SIZE), lambda s, pt: (0, 0))],
        out_specs=pl.BlockSpec((1, rows, CMP_G), lambda s, pt: (s // steps, s % steps, 0)),
        scratch_shapes=[pltpu.VMEM((2, GATHER_PAGES, 2 * KV_W, PAGE_SIZE), F32),
                        pltpu.VMEM((GATHER_PAGES, PAGE_SIZE, 2 * KV_W), F32),
                        pltpu.SemaphoreType.DMA((2, GATHER_PAGES))],
    )
    return pl.pallas_call(
        functools.partial(_cmp_gather_kernel, layer=layer),
        grid_spec=grid_spec,
        out_shape=jax.ShapeDtypeStruct((batch, n_pages * GROUPS_PER_PAGE, CMP_G), F32),
        compiler_params=_params("arbitrary"),
        name="nsa_cmp_gather",
    )(pt_flat, ct, w_rs, perm)


def _to_col(row, n):
    eye = (lax.broadcasted_iota(jnp.int32, (n, n), 0) == lax.broadcasted_iota(jnp.int32, (n, n), 1))
    return jnp.sum(jnp.where(eye, jnp.broadcast_to(row, (n, n)), 0.0), axis=1, keepdims=True)


def _dec_cmp_kernel(g_ref, q_ref, wc_ref, wnew_ref, wab_ref, pe_ref, b1_ref, w2_ref, b2_ref, g1_ref,
                    ones_ref, m_ref, oc_ref, ow_ref, idx_ref, wout_ref, *, n_groups, nb, cur):
    nh, grp = NSA_KV_HEADS, NSA_GROUP
    q8 = q_ref[0]
    kc, vc = _cmp_finish(g_ref[0], n_groups, wab_ref, pe_ref, b1_ref, w2_ref, b2_ref, g1_ref, ones_ref)
    sc = _dot_nt(q8, kc.astype(BF16))
    vis = lax.broadcasted_iota(jnp.int32, sc.shape, 1) < n_groups - 1
    s_m = jnp.where(vis, sc, NEG)
    e = jnp.where(vis, jnp.exp(s_m - jnp.max(s_m, axis=-1, keepdims=True)), 0.0)
    p_c = e / jnp.maximum(jnp.sum(e, axis=-1, keepdims=True), 1e-30)
    oc_ref[0] = _dot(p_c.astype(BF16), vc.astype(BF16))

    nsp = m_ref.shape[1]
    pg = jnp.concatenate([jnp.sum(p_c[h * grp:(h + 1) * grp], axis=0, keepdims=True) for h in range(nh)]
                         + [jnp.zeros((8 - nh, n_groups), F32)], axis=0)
    h1 = pg.astype(BF16)
    r1 = pg - h1.astype(F32)
    h2 = r1.astype(BF16)
    h3 = (r1 - h2.astype(F32)).astype(BF16)
    m = m_ref[...]
    p_sel = _dot(h1, m) + _dot(h2, m) + _dot(h3, m)
    blk = lax.broadcasted_iota(jnp.int32, (1, nsp), 1)
    forced = (blk == 0) | (blk == cur) | (blk == cur - 1)
    ii = lax.broadcasted_iota(jnp.int32, (nsp, nsp), 0)
    jj = lax.broadcasted_iota(jnp.int32, (nsp, nsp), 1)
    slot = lax.broadcasted_iota(jnp.int32, (SEL_TOPK, nsp), 0).astype(F32)
    blk_f = lax.broadcasted_iota(jnp.int32, (SEL_TOPK, nsp), 1).astype(F32)
    for h in range(nh):
        score = jnp.where(blk > cur, -1.0, jnp.where(forced, FORCE_SCORE, p_sel[h:h + 1, :]))
        score = jnp.where(blk >= nb, -2.0, score)
        s_col = _to_col(score, nsp)
        ge = jnp.where(s_col >= score, 1.0, 0.0)
        gt = jnp.where(s_col > score, 1.0, 0.0)
        rank = jnp.sum(jnp.where(jj > ii, ge, gt), axis=0, keepdims=True)
        sel = (rank < SEL_TOPK).astype(F32)
        before = jnp.sum(jnp.where(ii < jj, _to_col(sel, nsp), 0.0), axis=0, keepdims=True)
        onehot = jnp.where((before == slot) & (sel > 0.5), 1.0, 0.0)
        idx = jnp.sum(onehot * blk_f, axis=1, keepdims=True)
        idx_ref[0, h] = jnp.broadcast_to(idx, (SEL_TOPK, LANES)).astype(jnp.int32)

    wt = wc_ref[0, 0]
    wb = wt.shape[1]
    wnew = wnew_ref[0]
    s = _dot(q8, wt[0:KV_W].astype(BF16))
    s = jnp.where(lax.broadcasted_iota(jnp.int32, s.shape, 1) > wb - WINDOW, s, NEG)
    qf = q8.astype(F32)
    s_new = jnp.sum(qf * wnew[:, 0:KV_W].astype(BF16).astype(F32), axis=-1, keepdims=True)
    mx = jnp.maximum(jnp.max(s, axis=-1, keepdims=True), s_new)
    e = jnp.exp(s - mx)
    e_new = jnp.exp(s_new - mx)
    num = _dot_nt(e.astype(BF16), wt[KV_W:2 * KV_W].astype(BF16)) + e_new * wnew[:, KV_W:2 * KV_W]
    ow_ref[0] = num / (jnp.sum(e, axis=-1, keepdims=True) + e_new)
    lane = lax.broadcasted_iota(jnp.int32, wt.shape, 1)
    wout_ref[0] = jnp.where(lane == wb - 1, _to_col(wnew, 2 * KV_W), pltpu.roll(wt, wb - 1, 1))


def _dec_constants(past):
    n_groups = past // CMP_STRIDE
    nb = past // SEL_BLOCK + 1
    nsp = -(-nb // LANES) * LANES
    j = np.arange(nsp)[None, :]
    n = np.arange(n_groups)[:, None]
    first = (n >= SEL_RATIO * j) & (n <= SEL_RATIO * j + SEL_RATIO - 1)
    second = (n >= SEL_RATIO * j - 1) & (n <= SEL_RATIO * j + SEL_RATIO - 2)
    m = (first.astype(np.float32) + second.astype(np.float32)) * (n < n_groups - 1) * (j < nb)
    return n_groups, nb, jnp.asarray(m, BF16)


def _dec_cmp_attn(gsum, q8, wt, layer, win_new, cw, batch, past):
    n_groups, nb, m = _dec_constants(past)
    nsp = m.shape[1]
    wb = wt.shape[3]
    assert wb == WINDOW
    per_b = lambda shape: pl.BlockSpec((1,) + shape, lambda b: (b,) + (0,) * len(shape))
    kern = functools.partial(_dec_cmp_kernel, n_groups=n_groups, nb=nb, cur=past // SEL_BLOCK)
    return pl.pallas_call(
        kern,
        grid=(batch,),
        in_specs=[per_b((n_groups, CMP_G)), per_b((NSA_HEADS, LANES)),
                  pl.BlockSpec((1, 1, 2 * KV_W, wb), lambda b: (layer, b, 0, 0)),
                  per_b((1, 2 * KV_W))] + _cmp_specs() + [_const_spec((n_groups, nsp))],
        out_specs=[per_b((NSA_HEADS, LANES)), per_b((NSA_HEADS, LANES)),
                   per_b((NSA_KV_HEADS, SEL_TOPK, LANES)), per_b((2 * KV_W, wb))],
        out_shape=[jax.ShapeDtypeStruct((batch, NSA_HEADS, LANES), F32),
                   jax.ShapeDtypeStruct((batch, NSA_HEADS, LANES), F32),
                   jax.ShapeDtypeStruct((batch, NSA_KV_HEADS, SEL_TOPK, LANES), jnp.int32),
                   jax.ShapeDtypeStruct((batch, 2 * KV_W, wb), F32)],
        compiler_params=_params("parallel"),
        name="nsa_decode_cmp",
    )(gsum, q8, wt, win_new, *cw, m)


def _dec_sel_kernel(pt_ref, idx_ref, q_ref, new_ref, oc_ref, ow_ref, gate_ref, ct_ref, out_ref,
                    kv_buf, sems, *, layer, n_pages, n_cache_blocks):
    b = pl.program_id(0)
    buf_slot = b % 2
    nh, nq, topk = NSA_KV_HEADS, NSA_HEADS, SEL_TOPK
    half = PAGE_SIZE // SEL_BLOCK
    nblk = nh * topk

    def page_copies(seq, dst):
        copies = []
        for j in range(nblk):
            blk = jnp.minimum(idx_ref[seq * nblk + j], n_cache_blocks - 1)
            page = pt_ref[seq * n_pages + blk // half]
            copies.append(pltpu.make_async_copy(ct_ref.at[layer, page, pl.ds(2 * KV_W, 2 * KV_W), :],
                                                kv_buf.at[dst, j], sems.at[dst, j]))
        return copies

    @pl.when(b == 0)
    def _():
        for c in page_copies(0, 0):
            c.start()

    @pl.when(b + 1 < pl.num_programs(0))
    def _():
        for c in page_copies(b + 1, 1 - buf_slot):
            c.start()
    for c in page_copies(b, buf_slot):
        c.wait()

    q8 = q_ref[0]
    new = new_ref[0]
    s_new = jnp.sum(q8 * new[:, 2 * KV_W:3 * KV_W].astype(BF16).astype(F32), axis=-1, keepdims=True)
    v_new = new[:, 3 * KV_W:4 * KV_W]
    qb = q8.astype(BF16)
    page_half = lax.broadcasted_iota(jnp.int32, (1, PAGE_SIZE), 1) // SEL_BLOCK
    o_heads = []
    for h in range(nh):
        s_parts = []
        m = s_new
        for k in range(topk):
            j = h * topk + k
            blk = idx_ref[b * nblk + j]
            s_k = _dot(qb, kv_buf[buf_slot, j, 0:KV_W, :].astype(BF16))
            keep = jnp.where(blk < n_cache_blocks, 0.0, NEG)
            s_k = s_k + jnp.where(page_half == blk % half, keep, NEG)
            s_parts.append(s_k)
            m = jnp.maximum(m, jnp.max(s_k, axis=-1, keepdims=True))
        e_new = jnp.exp(s_new - m)
        den = e_new
        num = e_new * v_new
        for k in range(topk):
            e = jnp.exp(s_parts[k] - m)
            den = den + jnp.sum(e, axis=-1, keepdims=True)
            num = num + _dot_nt(e.astype(BF16), kv_buf[buf_slot, h * topk + k, KV_W:2 * KV_W, :].astype(BF16))
        o_heads.append(num / den)
    row = lax.broadcasted_iota(jnp.int32, (nq, KV_W), 0)
    o_s = jnp.where(row < NSA_GROUP, o_heads[0], o_heads[1])
    gt = gate_ref[0]
    out_ref[0] = gt[:, 0:1] * oc_ref[0] + gt[:, 1:2] * o_s + gt[:, 2:3] * ow_ref[0]


def _dec_sel_attn(ct, layer, pt_flat, idx_flat, q8, rows_new, o_c, o_w, gates, batch, n_pages):
    half = PAGE_SIZE // SEL_BLOCK
    nh, grp, nq = NSA_KV_HEADS, NSA_GROUP, NSA_HEADS
    g3 = gates.reshape(batch, nh, LANES)[:, :, 0:3 * grp].reshape(batch, nq, 3)
    g8 = jnp.concatenate([g3, jnp.zeros((batch, nq, LANES - 3), F32)], axis=-1)
    new8 = jnp.broadcast_to(rows_new.reshape(batch, 1, ROWS_W), (batch, nq, ROWS_W))

    per_b = lambda w: pl.BlockSpec((1, nq, w), lambda b, pt, idx: (b, 0, 0))
    grid_spec = pltpu.PrefetchScalarGridSpec(
        num_scalar_prefetch=2,
        grid=(batch,),
        in_specs=[per_b(LANES), per_b(ROWS_W), per_b(LANES), per_b(LANES), per_b(LANES),
                  pl.BlockSpec(memory_space=pl.ANY)],
        out_specs=per_b(LANES),
        scratch_shapes=[pltpu.VMEM((2, nh * SEL_TOPK, 2 * KV_W, PAGE_SIZE), F32),
                        pltpu.SemaphoreType.DMA((2, nh * SEL_TOPK))],
    )
    kern = functools.partial(_dec_sel_kernel, layer=layer, n_pages=n_pages, n_cache_blocks=n_pages * half)
    return pl.pallas_call(
        kern,
        grid_spec=grid_spec,
        out_shape=jax.ShapeDtypeStruct((batch, nq, LANES), F32),
        compiler_params=_params("arbitrary"),
        name="nsa_decode_sel",
    )(pt_flat, idx_flat, q8.astype(F32), new8, o_c, o_w, g8, ct)


def _sample_layer(x, lp, lw, layer, ct, wt, gla_s0, c0, n0, m0, conv0, page_table):
    batch, n_pages = page_table.shape
    past = n_pages * PAGE_SIZE
    d = HEAD_DIM
    c = SCAN_CHUNK_DECODE
    zg, zn, zm = _in_proj(x, lp['norm1_g'].reshape(1, D_MODEL), lw['w_in'], batch)
    qpad, rows, win, _, _, gates = _nsa_prep(zn, lp['nsa_qk_g'], lp['nsa_gate_b'], batch)
    pad = lambda z: jnp.pad(z[:, None, :], ((0, 0), (0, c - 1), (0, 0))).reshape(batch * c, z.shape[-1])
    first = lambda o: o.reshape(batch, c, o.shape[-1])[:, 0]
    og, s_gla, om, c_m, n_m, m_m = _gla_mlstm(pad(zg), pad(zm), gla_s0, conv0, c0, n0, m0, lp, batch, c, 1, c)
    pt_flat = page_table.reshape(-1)
    gsum = _cmp_gather(ct, layer, pt_flat, lw['cmp'][0], batch, n_pages)
    q8 = qpad.reshape(batch, NSA_HEADS, LANES)
    o_c, o_w, idx, win_t = _dec_cmp_attn(gsum, q8, wt, layer, win.reshape(batch, 1, 2 * KV_W), lw['cmp'],
                                         batch, past)
    o_n = _dec_sel_attn(ct, layer, pt_flat, idx[:, :, :, 0].reshape(-1), q8, rows, o_c, o_w, gates,
                        batch, n_pages)
    on = jnp.stack([o_n[:, h * NSA_GROUP:(h + 1) * NSA_GROUP, h * d:(h + 1) * d]
                    for h in range(NSA_KV_HEADS)], axis=1)
    on = on.reshape(batch, NSA_W).astype(BF16)
    y = _out_ffn(x, first(og), on, first(om), lw['w_out'], lp['norm2_g'].reshape(1, D_MODEL),
                 lw['w_up'], lw['w_down'], batch)
    new_rows = rows.reshape(batch, 1, 4, NSA_KV_HEADS, d)
    win_state = _rows_on_lanes_to_state(win_t, 2)
    conv_state = jnp.concatenate([conv0.astype(F32), zm[:, None, 0:MLSTM_W]], axis=1)[:, 1:]
    return y, (new_rows, win_state, s_gla, c_m, n_m, m_m, conv_state)


def kernel(x_prompt, x_sample, cache_nsa_kv, cache_nsa_win, state_gla, state_mlstm_c, state_mlstm_n,
           state_mlstm_m, state_mlstm_conv, page_table, norm1_g, w_in, gla_a_w, gla_a_b, gla_norm_g,
           nsa_qk_g, nsa_gate_b, cmp_pe, cmp_w1, cmp_b1, cmp_w2, cmp_b2, ml_conv_w, ml_conv_b, ml_wq, ml_wk,
           ml_gate_b, ml_norm_g, ml_skip, w_out, norm2_g, w_up, w_down):
    bp, t_len, _ = x_prompt.shape
    n_dec = x_sample.shape[0]
    depth = w_in.shape[0]
    x_p = x_prompt.reshape(bp * t_len, D_MODEL)
    x_s = x_sample.reshape(n_dec, D_MODEL)
    states_p, states_s = [], []
    ct = _paged_cache_view(cache_nsa_kv)
    wt = _window_cache_view(cache_nsa_win)
    for l in range(depth):
        lp = {'norm1_g': norm1_g[l], 'w_in': w_in[l], 'gla_a_w': gla_a_w[l], 'gla_a_b': gla_a_b[l],
              'gla_norm_g': gla_norm_g[l], 'nsa_qk_g': nsa_qk_g[l], 'nsa_gate_b': nsa_gate_b[l],
              'cmp_pe': cmp_pe[l], 'cmp_w1': cmp_w1[l], 'cmp_b1': cmp_b1[l], 'cmp_w2': cmp_w2[l],
              'cmp_b2': cmp_b2[l], 'ml_conv_w': ml_conv_w[l], 'ml_conv_b': ml_conv_b[l], 'ml_wq': ml_wq[l],
              'ml_wk': ml_wk[l], 'ml_gate_b': ml_gate_b[l], 'ml_norm_g': ml_norm_g[l], 'ml_skip': ml_skip[l],
              'w_out': w_out[l], 'norm2_g': norm2_g[l], 'w_up': w_up[l], 'w_down': w_down[l]}
        lw = _layer_weights(lp)
        x_p, st_p = _prompt_layer(x_p, lp, lw, bp, t_len)
        x_s, st_s = _sample_layer(x_s, lp, lw, l, ct, wt, state_gla[l],
                                  state_mlstm_c[l], state_mlstm_n[l], state_mlstm_m[l], state_mlstm_conv[l],
                                  page_table)
        states_p.append(st_p)
        states_s.append(st_s)
    outs = [x_p.reshape(bp, t_len, D_MODEL), x_s.reshape(n_dec, 1, D_MODEL)]
    for i in range(7):
        outs.append(jnp.stack([s[i] for s in states_p]))
        outs.append(jnp.stack([s[i] for s in states_s]))
    return tuple(outs)
```

```python
import functools

import numpy as np
import jax
import jax.numpy as jnp
from jax import lax
from jax.experimental import pallas as pl
from jax.experimental.pallas import tpu as pltpu

F32 = jnp.float32
BF16 = jnp.bfloat16

D_MODEL = 1024
HEAD_DIM = 64
PAGE_SIZE = 128
GLA_HEADS = 4
GLA_LOWRANK = 16
GLA_TAU = 16.0
NSA_HEADS = 8
NSA_KV_HEADS = 2
NSA_GROUP = NSA_HEADS // NSA_KV_HEADS
CMP_BLOCK = 32
CMP_STRIDE = 16
CMP_HIDDEN = 64
SEL_BLOCK = 64
SEL_RATIO = SEL_BLOCK // CMP_STRIDE
SEL_TOPK = 16
WINDOW = 512
FORCE_SCORE = 1.0e4
MLSTM_HEADS = 4
CONV_WIDTH = 4
D_FF = 4 * D_MODEL
EPS = 1e-6

GLA_W = GLA_HEADS * HEAD_DIM
NSA_W = NSA_HEADS * HEAD_DIM
KV_W = NSA_KV_HEADS * HEAD_DIM
MLSTM_W = MLSTM_HEADS * HEAD_DIM

LANES = 128
ZG_W = 4 * GLA_W + LANES
GATE_W = NSA_KV_HEADS * LANES
ZN_W = NSA_W + 6 * KV_W + GATE_W
ZM_W = 3 * MLSTM_W + LANES
ROWS_W = 4 * KV_W
WIN_W = 2 * KV_W
QPAD_W = NSA_HEADS * LANES

SCAN_CHUNK = 128
SCAN_CHUNK_DECODE = 32
GLA_SUB = 32
NSA_TQ = 128
NSA_TK = 256
NSA_TK_FIXED = 512
NSA_MAX_FIXED_SHIFT = 40.0
NSA_RANK_SIZES = 4
NSA_TILE_UNROLL = 8
NEG = -1.0e30
SEL_BIAS = 29952.0

VMEM_LIMIT = 56 * 1024 * 1024


def _dot(a, b):
    return jnp.dot(a, b, preferred_element_type=F32)


def _dot_hi(a, b):
    return jnp.dot(a, b, preferred_element_type=F32, precision=lax.Precision.HIGHEST)


def _dot_nt(a, b):
    return lax.dot_general(a, b, (((1,), (1,)), ((), ())), preferred_element_type=F32)


def _dot_nt_hi(a, b):
    return lax.dot_general(a, b, (((1,), (1,)), ((), ())), preferred_element_type=F32,
                           precision=lax.Precision.HIGHEST)


def _dot_tn_hi(a, b):
    return lax.dot_general(a, b, (((0,), (0,)), ((), ())), preferred_element_type=F32,
                           precision=lax.Precision.HIGHEST)


def _sigmoid(x):
    return 1.0 / (1.0 + jnp.exp(-x))


def _log_sigmoid(x):
    return jnp.minimum(x, 0.0) - jnp.log(1.0 + jnp.exp(-jnp.abs(x)))


def _group_sum(x, ones_bd):
    hi = x.astype(BF16)
    lo = (x - hi.astype(F32)).astype(BF16)
    return _dot(hi, ones_bd) + _dot(lo, ones_bd)


def _group_mean_sq(x, ones_bd):
    return _group_sum(x * x, ones_bd) * (1.0 / HEAD_DIM)


def _params(*sem):
    return pltpu.CompilerParams(dimension_semantics=sem, vmem_limit_bytes=VMEM_LIMIT)


def _const_spec(shape):
    nd = len(shape)
    return pl.BlockSpec(shape, lambda *_: (0,) * nd)


def _pack_w_in(w_in):
    def cols(a, b):
        return w_in[:, a:b]

    def zeros(n):
        return jnp.zeros((D_MODEL, n), w_in.dtype)
    o_nsa = 4 * GLA_W + GLA_LOWRANK
    o_ng = o_nsa + NSA_W + 6 * KV_W
    o_ml = o_ng + 3 * NSA_HEADS
    o_mi = o_ml + 2 * MLSTM_W
    o_mo = o_mi + 2 * MLSTM_HEADS
    n_gate = 3 * NSA_GROUP
    parts = [cols(0, o_nsa), zeros(LANES - GLA_LOWRANK),
             cols(o_nsa, o_ng),
             cols(o_ng, o_ng + n_gate), zeros(LANES - n_gate),
             cols(o_ng + n_gate, o_ml), zeros(LANES - n_gate),
             cols(o_ml, o_mi), cols(o_mo, o_mo + MLSTM_W), cols(o_mi, o_mo),
             zeros(LANES - 2 * MLSTM_HEADS)]
    return jnp.concatenate(parts, axis=1).astype(BF16)


def _nsa_prep_kernel(zn_ref, gq_ref, gr_ref, gw_ref, gb_ref, ones_ref, place_ref,
                     qpad_ref, rows_ref, win_ref, kv_ref, r01_ref, gate_ref, *, transposed):
    ones_bd = ones_ref[...]
    q = zn_ref[:, 0:NSA_W]
    qn = q * lax.rsqrt(_group_mean_sq(q, ones_bd) + EPS) * gq_ref[...]
    qpad_ref[...] = _dot(qn.astype(BF16), place_ref[...]).astype(BF16)

    r = zn_ref[:, NSA_W:NSA_W + ROWS_W]
    col = lax.broadcasted_iota(jnp.int32, r.shape, 1)
    rn = r * lax.rsqrt(_group_mean_sq(r, ones_bd) + EPS) * gr_ref[...]
    rows = jnp.where((col >= 2 * KV_W) & (col < 3 * KV_W), rn, r)
    if transposed:
        rows_ref[0] = rows.T
    else:
        rows_ref[...] = rows
    r01_ref[...] = rows[:, 0:2 * KV_W].astype(BF16)

    w = zn_ref[:, NSA_W + ROWS_W:NSA_W + ROWS_W + WIN_W]
    colw = lax.broadcasted_iota(jnp.int32, w.shape, 1)
    wn = w * lax.rsqrt(_group_mean_sq(w, ones_bd[0:WIN_W, 0:WIN_W]) + EPS) * gw_ref[...]
    win = jnp.where(colw < KV_W, wn, w)
    if transposed:
        win_ref[0] = win.T
    else:
        win_ref[...] = win
    kv_ref[:, 0:2 * KV_W] = rows[:, 2 * KV_W:4 * KV_W].astype(BF16)
    kv_ref[:, 2 * KV_W:4 * KV_W] = win.astype(BF16)

    gate_ref[...] = _sigmoid(zn_ref[:, NSA_W + ROWS_W + WIN_W:ZN_W] + gb_ref[...])


def _nsa_prep(zn, qk_g, gate_b, tm, seq=None, proj=None):
    n = zn.shape[0] if proj is None else proj[0].shape[0]
    scale = HEAD_DIM ** -0.5
    gq = (jnp.tile(qk_g[0], NSA_HEADS) * scale).reshape(1, NSA_W)
    gr = jnp.tile(qk_g[2], ROWS_W // HEAD_DIM).reshape(1, ROWS_W)
    gw = jnp.tile(qk_g[3], WIN_W // HEAD_DIM).reshape(1, WIN_W)
    n_gate = 3 * NSA_GROUP
    gpad = jnp.zeros((LANES - n_gate,), F32)
    gb = jnp.concatenate([gate_b[0:n_gate], gpad, gate_b[n_gate:], gpad]).reshape(1, GATE_W)
    grp = np.arange(NSA_W) // HEAD_DIM
    ones_bd = jnp.asarray(grp[:, None] == grp[None, :], BF16)
    src = np.arange(NSA_W)
    head, d = src // HEAD_DIM, src % HEAD_DIM
    dst = head * LANES + (head // NSA_GROUP) * HEAD_DIM + d
    place = np.zeros((NSA_W, QPAD_W), np.float32)
    place[src, dst] = 1.0
    place = jnp.asarray(place, BF16)
    row = lambda w: pl.BlockSpec((tm, w), lambda i: (i, 0))
    if seq is None:
        state_spec = row
        state_shape = lambda w: jax.ShapeDtypeStruct((n, w), F32)
    else:
        batch, t_len = seq
        nblk = t_len // tm
        state_spec = lambda w: pl.BlockSpec((1, w, tm), lambda i: (i // nblk, 0, i % nblk))
        state_shape = lambda w: jax.ShapeDtypeStruct((batch, w, t_len), F32)
    prep_in_specs = [_const_spec((1, NSA_W)), _const_spec((1, ROWS_W)), _const_spec((1, WIN_W)),
                     _const_spec((1, GATE_W)), _const_spec((NSA_W, NSA_W)), _const_spec((NSA_W, QPAD_W))]
    prep_out_specs = [row(QPAD_W), state_spec(ROWS_W), state_spec(WIN_W), row(4 * KV_W), row(2 * KV_W),
                      row(GATE_W)]
    prep_out_shape = [jax.ShapeDtypeStruct((n, QPAD_W), BF16), state_shape(ROWS_W), state_shape(WIN_W),
                      jax.ShapeDtypeStruct((n, 4 * KV_W), BF16), jax.ShapeDtypeStruct((n, 2 * KV_W), BF16),
                      jax.ShapeDtypeStruct((n, GATE_W), F32)]
    if proj is None:
        return pl.pallas_call(
            functools.partial(_nsa_prep_kernel, transposed=seq is not None),
            grid=(n // tm,),
            in_specs=[row(ZN_W)] + prep_in_specs,
            out_specs=prep_out_specs,
            out_shape=prep_out_shape,
            compiler_params=_params("parallel"),
            name="nsa_prep",
        )(zn, gq, gr, gw, gb, ones_bd, place)
    x, g, w = proj
    return pl.pallas_call(
        functools.partial(_inproj_prep_kernel, transposed=seq is not None),
        grid=(n // tm,),
        in_specs=[row(D_MODEL), _const_spec((1, D_MODEL)), _const_spec(w.shape)] + prep_in_specs,
        out_specs=[row(ZG_W), row(ZM_W)] + prep_out_specs,
        out_shape=[jax.ShapeDtypeStruct((n, ZG_W), F32), jax.ShapeDtypeStruct((n, ZM_W), F32)] + prep_out_shape,
        compiler_params=_params("parallel"),
        name="in_proj_nsa_prep",
    )(x, g, w, gq, gr, gw, gb, ones_bd, place)


def _inproj_prep_kernel(x_ref, g_ref, w_ref, gq_ref, gr_ref, gw_ref, gb_ref, ones_ref, place_ref,
                        zg_ref, zm_ref, qpad_ref, rows_ref, win_ref, kv_ref, r01_ref, gate_ref, *, transposed):
    x = x_ref[...]
    h = x * lax.rsqrt(jnp.mean(x * x, axis=-1, keepdims=True) + EPS) * g_ref[...]
    hb = h.astype(BF16)
    zg_ref[...] = _dot(hb, w_ref[:, 0:ZG_W])
    zm_ref[...] = _dot(hb, w_ref[:, ZG_W + ZN_W:ZG_W + ZN_W + ZM_W])
    zn = _dot(hb, w_ref[:, ZG_W:ZG_W + ZN_W])
    _nsa_prep_kernel(zn, gq_ref, gr_ref, gw_ref, gb_ref, ones_ref, place_ref,
                     qpad_ref, rows_ref, win_ref, kv_ref, r01_ref, gate_ref, transposed=transposed)


def _load_block_diag(bd_scr, heads_ref):
    nh, d = heads_ref.shape[1], heads_ref.shape[2]
    bd_scr[...] = jnp.zeros(bd_scr.shape, F32)
    for h in range(nh):
        bd_scr[h * d:(h + 1) * d, h * d:(h + 1) * d] = heads_ref[0, h]


def _store_block_diag(heads_ref, bd_scr):
    nh, d = heads_ref.shape[1], heads_ref.shape[2]
    for h in range(nh):
        heads_ref[0, h] = bd_scr[h * d:(h + 1) * d, h * d:(h + 1) * d]


def _gla_init(zg_ref, s0_ref, aw_ref, ab_ref, ng_ref, tri_ref, ones_ref, og_ref, sout_ref, s_scr):
    @pl.when(pl.program_id(1) == 0)
    def _():
        _load_block_diag(s_scr, s0_ref)


def _gla_fin(zg_ref, s0_ref, aw_ref, ab_ref, ng_ref, tri_ref, ones_ref, og_ref, sout_ref, s_scr):
    @pl.when(pl.program_id(1) == pl.num_programs(1) - 1)
    def _():
        _store_block_diag(sout_ref, s_scr)


def _gla_body(zg_ref, s0_ref, aw_ref, ab_ref, ng_ref, tri_ref, ones_ref, og_ref, sout_ref, s_scr,
              *, t_valid, t_pad):
    i = pl.program_id(1)
    c = zg_ref.shape[0]
    d = HEAD_DIM
    q = zg_ref[:, 0:GLA_W] * (d ** -0.5)
    k = zg_ref[:, GLA_W:2 * GLA_W]
    v = zg_ref[:, 2 * GLA_W:3 * GLA_W]
    r = zg_ref[:, 3 * GLA_W:4 * GLA_W]
    ga = zg_ref[:, 4 * GLA_W:ZG_W]
    g = _log_sigmoid(_dot(ga.astype(BF16), aw_ref[...]) + ab_ref[...]) * (1.0 / GLA_TAU)
    if t_valid < t_pad:
        valid = (i * c + lax.broadcasted_iota(jnp.int32, (c, 1), 0)) < t_valid
        g = jnp.where(valid, g, 0.0)
        k = jnp.where(valid, k, 0.0)
    bcum = _dot_hi(tri_ref[...], g)
    b_end = bcum[c - 1:c, :]
    nh, w, sub = GLA_HEADS, GLA_W, min(GLA_SUB, c)
    lane_head = lax.broadcasted_iota(jnp.int32, (1, w), 1) // d
    s_bd = s_scr[...]
    o = _dot((q * jnp.exp(bcum)).astype(BF16), s_bd.astype(BF16))
    vb = v.astype(BF16)
    o_sub = []
    for j in range(c // sub):
        lo, hi = j * sub, (j + 1) * sub
        base = bcum[lo - 1:lo, :] if j > 0 else jnp.zeros((1, w), F32)
        qt = q[lo:hi] * jnp.exp(bcum[lo:hi] - base)
        kt = (k[0:hi] * jnp.exp(base - bcum[0:hi])).astype(BF16)
        qs = jnp.concatenate([jnp.where(lane_head == h, qt, 0.0) for h in range(nh)], axis=0)
        sc = _dot_nt(qs.astype(BF16), kt)
        t_row = lo + (lax.broadcasted_iota(jnp.int32, (nh * sub, hi), 0) % sub)
        sc = jnp.where(lax.broadcasted_iota(jnp.int32, (nh * sub, hi), 1) <= t_row, sc, 0.0)
        ov = _dot(sc.astype(BF16), vb[0:hi])
        o_j = jnp.where(lane_head == 0, ov[0:sub], 0.0)
        for h in range(1, nh):
            o_j = jnp.where(lane_head == h, ov[h * sub:(h + 1) * sub], o_j)
        o_sub.append(o_j)
    o = o + jnp.concatenate(o_sub, axis=0)
    last = (lax.broadcasted_iota(jnp.int32, (c, LANES), 0) == c - 1).astype(F32)
    decay_col = jnp.exp(_dot_tn_hi(bcum, last))
    k_hat = k * jnp.exp(b_end - bcum)
    same_head = (lax.broadcasted_iota(jnp.int32, (w, w), 0) // d) == (lax.broadcasted_iota(jnp.int32, (w, w), 1) // d)
    s_scr[...] = (jnp.concatenate([decay_col] * (w // LANES), axis=1) * s_bd
                  + jnp.where(same_head, _dot_tn_hi(k_hat, v), 0.0))
    on = o * lax.rsqrt(_group_mean_sq(o, ones_ref[...]) + EPS) * ng_ref[...]
    og_ref[...] = (on * (r * _sigmoid(r))).astype(BF16)


def _run_scans(scans, batch, nblk, name):
    counts = [(len(s['in_specs']), len(s['out_specs']), len(s['scratch'])) for s in scans]
    n_in, n_out = sum(c[0] for c in counts), sum(c[1] for c in counts)

    def kernel(*refs):
        views, o_in, o_out, o_scr = [], 0, n_in, n_in + n_out
        for ni, no, ns in counts:
            views.append(refs[o_in:o_in + ni] + refs[o_out:o_out + no] + refs[o_scr:o_scr + ns])
            o_in, o_out, o_scr = o_in + ni, o_out + no, o_scr + ns
        for phase in range(3):
            for s, view in zip(scans, views):
                s['phases'][phase](*view)
    outs = pl.pallas_call(
        kernel,
        grid=(batch, nblk),
        in_specs=[x for s in scans for x in s['in_specs']],
        out_specs=[x for s in scans for x in s['out_specs']],
        out_shape=[x for s in scans for x in s['out_shape']],
        scratch_shapes=[x for s in scans for x in s['scratch']],
        compiler_params=_params("parallel", "arbitrary"),
        name=name,
    )(*[x for s in scans for x in s['operands']])
    split, o = [], 0
    for _, no, _ in counts:
        split.append(outs[o:o + no])
        o += no
    return split


def _gla_scan(zg, s0, a_w, a_b, norm_g, batch, t_pad, t_valid, c):
    nblk = t_pad // c
    aw = jnp.concatenate([a_w, jnp.zeros((LANES - GLA_LOWRANK, GLA_W), F32)], axis=0).astype(BF16)
    tri = jnp.asarray(np.tril(np.ones((c, c), np.float32)))
    grp = np.arange(GLA_W) // HEAD_DIM
    ones_bd = jnp.asarray(grp[:, None] == grp[None, :], BF16)
    state = pl.BlockSpec((1, GLA_HEADS, HEAD_DIM, HEAD_DIM), lambda b, i: (b, 0, 0, 0))
    return dict(
        phases=(_gla_init, functools.partial(_gla_body, t_valid=t_valid, t_pad=t_pad), _gla_fin),
        in_specs=[pl.BlockSpec((c, ZG_W), lambda b, i: (b * nblk + i, 0)), state,
                  _const_spec((LANES, GLA_W)), _const_spec((1, GLA_W)), _const_spec((1, GLA_W)),
                  _const_spec((c, c)), _const_spec((GLA_W, GLA_W))],
        out_specs=[pl.BlockSpec((c, GLA_W), lambda b, i: (b * nblk + i, 0)), state],
        out_shape=[jax.ShapeDtypeStruct((batch * t_pad, GLA_W), BF16),
                   jax.ShapeDtypeStruct((batch, GLA_HEADS, HEAD_DIM, HEAD_DIM), F32)],
        scratch=[pltpu.VMEM((GLA_W, GLA_W), F32)],
        operands=(zg, s0, aw, a_b.reshape(1, GLA_W), jnp.tile(norm_g, GLA_HEADS).reshape(1, GLA_W), tri, ones_bd))


def _gla(zg, s0, a_w, a_b, norm_g, batch, t_pad, t_valid, c=SCAN_CHUNK):
    (out,) = _run_scans([_gla_scan(zg, s0, a_w, a_b, norm_g, batch, t_pad, t_valid, c)], batch, t_pad // c,
                        "gla_scan")
    return out


def _mlstm_init(zm_ref, conv0_ref, c0_ref, n0_ref, m0_ref, cw_ref, cb_ref, wq_ref, wk_ref, gb_ref,
                ng_ref, skip_ref, tri_ref, ones_ref, om_ref, cout_ref, nout_ref, mout_ref,
                c_scr, n_scr, m_scr, ext_scr):
    @pl.when(pl.program_id(1) == 0)
    def _():
        _load_block_diag(c_scr, c0_ref)
        n_scr[...] = n0_ref[0]
        m_scr[...] = m0_ref[0]
        ext_scr[0:8, :] = conv0_ref[0]


def _mlstm_fin(zm_ref, conv0_ref, c0_ref, n0_ref, m0_ref, cw_ref, cb_ref, wq_ref, wk_ref, gb_ref,
               ng_ref, skip_ref, tri_ref, ones_ref, om_ref, cout_ref, nout_ref, mout_ref,
               c_scr, n_scr, m_scr, ext_scr):
    @pl.when(pl.program_id(1) == pl.num_programs(1) - 1)
    def _():
        _store_block_diag(cout_ref, c_scr)
        nout_ref[0] = n_scr[...]
        mout_ref[0] = m_scr[...]


def _mlstm_body(zm_ref, conv0_ref, c0_ref, n0_ref, m0_ref, cw_ref, cb_ref, wq_ref, wk_ref, gb_ref,
                ng_ref, skip_ref, tri_ref, ones_ref, om_ref, cout_ref, nout_ref, mout_ref,
                c_scr, n_scr, m_scr, ext_scr, *, t_valid, t_pad):
    i = pl.program_id(1)
    c = zm_ref.shape[0]
    d = HEAD_DIM
    nh = MLSTM_HEADS
    mu = zm_ref[:, 0:MLSTM_W]
    ext_scr[8:8 + c, :] = mu
    u_conv = cb_ref[...] + mu * cw_ref[CONV_WIDTH - 1:CONV_WIDTH, :]
    for j in range(1, CONV_WIDTH):
        u_conv = u_conv + ext_scr[8 - j:8 - j + c, :] * cw_ref[CONV_WIDTH - 1 - j:CONV_WIDTH - j, :]
    ext_scr[0:8, :] = mu[c - 8:c, :]
    u_act = u_conv * _sigmoid(u_conv)
    ub = u_act.astype(BF16)
    q = _dot(ub, wq_ref[...])
    k = _dot(ub, wk_ref[...]) * (d ** -0.5)
    v = zm_ref[:, MLSTM_W:2 * MLSTM_W]
    og = zm_ref[:, 2 * MLSTM_W:3 * MLSTM_W]
    gz = zm_ref[:, 3 * MLSTM_W:ZM_W] + gb_ref[...]
    lane = lax.broadcasted_iota(jnp.int32, (c, LANES), 1)
    x = jnp.where(lane < nh, gz, _log_sigmoid(gz))
    if t_valid < t_pad:
        valid = (i * c + lax.broadcasted_iota(jnp.int32, (c, 1), 0)) < t_valid
        x = jnp.where(valid, x, jnp.where(lane < nh, NEG, 0.0))
    fc = _dot_hi(tri_ref[...], x)
    x = jnp.where(lane < nh, x, fc)
    sel = (lax.broadcasted_iota(jnp.int32, (8, LANES), 0)
           == lax.broadcasted_iota(jnp.int32, (8, LANES), 1)).astype(F32)
    xt = _dot_nt_hi(sel, x)
    row_i = lax.broadcasted_iota(jnp.int32, (c, c), 0)
    col_i = lax.broadcasted_iota(jnp.int32, (c, c), 1)
    w = MLSTM_W
    lane_head = lax.broadcasted_iota(jnp.int32, (1, w), 1) // d
    c_bd = c_scr[...]
    n_all = n_scr[0:1, :]
    m_all = m_scr[0:1, :]
    qb = q.astype(BF16)
    qs = jnp.concatenate([jnp.where(lane_head == h, q, 0.0) for h in range(nh)], axis=0)
    qk_all = _dot_nt(qs.astype(BF16), k.astype(BF16))
    qn_sum = _group_sum(q * n_all, ones_ref[...])
    zc, zr = jnp.zeros((c, w), F32), jnp.zeros((1, w), F32)
    mt_l, ws_l, rs_l, wl_l, wc_l, mn_l = zc, zc, zc, zc, zr, zr
    p_parts = []
    for h in range(nh):
        i_col, f_col = x[:, h:h + 1], x[:, nh + h:nh + h + 1]
        i_row, f_row = xt[h:h + 1, :], xt[nh + h:nh + h + 1, :]
        a = f_col + m_all[:, h * d:h * d + 1]
        dmat = jnp.where(col_i <= row_i, f_col - f_row + i_row, NEG)
        m_t = jnp.maximum(a, jnp.max(dmat, axis=-1, keepdims=True))
        p_h = qk_all[h * c:(h + 1) * c] * jnp.exp(dmat - m_t)
        p_parts.append(p_h)
        m_new = m_t[c - 1:c, :]
        on_head = lane_head == h
        mt_l = jnp.where(on_head, m_t, mt_l)
        ws_l = jnp.where(on_head, jnp.exp(a - m_t), ws_l)
        rs_l = jnp.where(on_head, jnp.sum(p_h, axis=-1, keepdims=True), rs_l)
        wl_l = jnp.where(on_head, jnp.exp(f_col[c - 1:c, :] - f_col + i_col - m_new), wl_l)
        wc_l = jnp.where(on_head, jnp.exp(a[c - 1:c, :] - m_new), wc_l)
        mn_l = jnp.where(on_head, m_new, mn_l)
    nv = _dot(jnp.concatenate(p_parts, axis=0).astype(BF16), v.astype(BF16))
    num = jnp.where(lane_head == 0, nv[0:c], 0.0)
    for h in range(1, nh):
        num = jnp.where(lane_head == h, nv[h * c:(h + 1) * c], num)
    num = num + ws_l * _dot(qb, c_bd.astype(BF16))
    den = rs_l + ws_l * qn_sum
    hm = num / jnp.maximum(jnp.abs(den), jnp.exp(-mt_l))
    kw = k * wl_l
    same_head = (lax.broadcasted_iota(jnp.int32, (w, w), 0) // d) == (lax.broadcasted_iota(jnp.int32, (w, w), 1) // d)
    c_scr[...] = wc_l * c_bd + jnp.where(same_head, _dot_tn_hi(kw, v), 0.0)
    n_scr[0:1, :] = wc_l * n_all + jnp.sum(kw, axis=0, keepdims=True)
    m_scr[0:1, :] = mn_l
    hn = hm * lax.rsqrt(_group_mean_sq(hm, ones_ref[...]) + EPS) * ng_ref[...]
    om_ref[...] = (_sigmoid(og) * (hn + skip_ref[...] * u_act)).astype(BF16)


def _block_diag_heads(w):
    nh, d, _ = w.shape
    eye = jnp.eye(nh, dtype=w.dtype)
    return jnp.einsum('hde,hg->hdge', w, eye).reshape(nh * d, nh * d)


def _mlstm_scan(zm, conv0, c0, n0, m0, lp, batch, t_pad, t_valid, c):
    nblk = t_pad // c
    nh, d = MLSTM_HEADS, HEAD_DIM
    conv0p = jnp.concatenate([jnp.zeros((batch, 8 - (CONV_WIDTH - 1), MLSTM_W), F32), conv0.astype(F32)], axis=1)
    pad7 = jnp.zeros((batch, 7, MLSTM_W), F32)
    n0p = jnp.concatenate([n0.reshape(batch, 1, MLSTM_W), pad7], axis=1)
    m0p = jnp.concatenate([jnp.repeat(m0, d, axis=1).reshape(batch, 1, MLSTM_W), pad7], axis=1)
    gb = jnp.concatenate([lp['ml_gate_b'][0], lp['ml_gate_b'][1],
                          jnp.zeros((LANES - 2 * nh,), F32)]).reshape(1, LANES)
    tri = jnp.asarray(np.tril(np.ones((c, c), np.float32)))
    grp = np.arange(MLSTM_W) // d
    ones_bd = jnp.asarray(grp[:, None] == grp[None, :], BF16)
    per_b = lambda shape: pl.BlockSpec((1,) + shape, lambda b, i: (b,) + (0,) * len(shape))
    return dict(
        phases=(_mlstm_init, functools.partial(_mlstm_body, t_valid=t_valid, t_pad=t_pad), _mlstm_fin),
        in_specs=[pl.BlockSpec((c, ZM_W), lambda b, i: (b * nblk + i, 0)),
                  per_b((8, MLSTM_W)), per_b((nh, d, d)), per_b((8, MLSTM_W)), per_b((8, MLSTM_W)),
                  _const_spec((CONV_WIDTH, MLSTM_W)), _const_spec((1, MLSTM_W)),
                  _const_spec((MLSTM_W, MLSTM_W)), _const_spec((MLSTM_W, MLSTM_W)),
                  _const_spec((1, LANES)), _const_spec((1, MLSTM_W)), _const_spec((1, MLSTM_W)),
                  _const_spec((c, c)), _const_spec((MLSTM_W, MLSTM_W))],
        out_specs=[pl.BlockSpec((c, MLSTM_W), lambda b, i: (b * nblk + i, 0)),
                   per_b((nh, d, d)), per_b((8, MLSTM_W)), per_b((8, MLSTM_W))],
        out_shape=[jax.ShapeDtypeStruct((batch * t_pad, MLSTM_W), BF16),
                   jax.ShapeDtypeStruct((batch, nh, d, d), F32),
                   jax.ShapeDtypeStruct((batch, 8, MLSTM_W), F32),
                   jax.ShapeDtypeStruct((batch, 8, MLSTM_W), F32)],
        scratch=[pltpu.VMEM((MLSTM_W, MLSTM_W), F32), pltpu.VMEM((8, MLSTM_W), F32),
                 pltpu.VMEM((8, MLSTM_W), F32), pltpu.VMEM((8 + c, MLSTM_W), F32)],
        operands=(zm, conv0p, c0, n0p, m0p, lp['ml_conv_w'], lp['ml_conv_b'].reshape(1, MLSTM_W),
                  _block_diag_heads(lp['ml_wq']).astype(BF16), _block_diag_heads(lp['ml_wk']).astype(BF16),
                  gb, jnp.tile(lp['ml_norm_g'], nh).reshape(1, MLSTM_W), lp['ml_skip'].reshape(1, MLSTM_W),
                  tri, ones_bd))


def _mlstm_states(outs, batch):
    om, c_f, n_f, m_f = outs
    nh, d = MLSTM_HEADS, HEAD_DIM
    return om, c_f, n_f[:, 0].reshape(batch, nh, d), m_f[:, 0].reshape(batch, nh, d)[:, :, 0]


def _mlstm(zm, conv0, c0, n0, m0, lp, batch, t_pad, t_valid, c=SCAN_CHUNK):
    (outs,) = _run_scans([_mlstm_scan(zm, conv0, c0, n0, m0, lp, batch, t_pad, t_valid, c)], batch, t_pad // c,
                         "mlstm_scan")
    return _mlstm_states(outs, batch)


def _gla_mlstm(zg, zm, gla_s0, conv0, c0, n0, m0, lp, batch, t_pad, t_valid, c=SCAN_CHUNK):
    g_out, m_out = _run_scans(
        [_gla_scan(zg, gla_s0, lp['gla_a_w'], lp['gla_a_b'], lp['gla_norm_g'], batch, t_pad, t_valid, c),
         _mlstm_scan(zm, conv0, c0, n0, m0, lp, batch, t_pad, t_valid, c)], batch, t_pad // c, "gla_mlstm_scan")
    return tuple(g_out) + _mlstm_states(m_out, batch)


CMP_IN = CMP_STRIDE * 2 * KV_W
CMP_G = 2 * 2 * KV_W


def _cmp_weights(lp):
    w1 = lp['cmp_w1'].reshape(2, 2, CMP_STRIDE, HEAD_DIM, CMP_HIDDEN)
    eye = jnp.eye(2, dtype=F32)
    w_ab = jnp.einsum('sarde,st,hg->rshdatge', w1, eye, eye).reshape(CMP_IN, CMP_G)
    pe = lp['cmp_pe'].reshape(2, 2, CMP_STRIDE, 1, HEAD_DIM)
    pe = jnp.broadcast_to(jnp.transpose(pe, (1, 2, 0, 3, 4)),
                          (2, CMP_STRIDE, 2, NSA_KV_HEADS, HEAD_DIM)).reshape(2, CMP_IN)
    pe8 = jnp.concatenate([pe, jnp.zeros((6, CMP_IN), F32)], axis=0)
    b1 = jnp.broadcast_to(lp['cmp_b1'][:, None, :], (2, NSA_KV_HEADS, CMP_HIDDEN)).reshape(1, 2 * KV_W)
    w2 = jnp.einsum('sed,st,hg->shetgd', lp['cmp_w2'], eye, eye).reshape(2 * KV_W, 2 * KV_W)
    b2 = jnp.broadcast_to(lp['cmp_b2'][:, None, :], (2, NSA_KV_HEADS, HEAD_DIM)).reshape(1, 2 * KV_W)
    g1 = jnp.tile(lp['nsa_qk_g'][1], NSA_KV_HEADS).reshape(1, KV_W)
    grp = np.arange(KV_W) // HEAD_DIM
    ones_bd = jnp.asarray(grp[:, None] == grp[None, :], BF16)
    return (w_ab.astype(BF16), pe8.astype(BF16), b1, w2.astype(BF16), b2, g1, ones_bd)


def _cmp_finish(gsum, n_rows, wab_ref, pe_ref, b1_ref, w2_ref, b2_ref, g1_ref, ones_ref):
    half = 2 * KV_W
    g_pe = _dot(pe_ref[...], wab_ref[...])
    bias = g_pe[0:1, 0:half] + g_pe[1:2, half:CMP_G] + b1_ref[...]
    hid = gsum[:, 0:half] + pltpu.roll(gsum[:, half:CMP_G], n_rows - 1, 0) + bias
    act = hid * _sigmoid(hid)
    cmp = _dot(act.astype(BF16), w2_ref[...]) + b2_ref[...]
    kc = cmp[:, 0:KV_W]
    kc = kc * lax.rsqrt(_group_mean_sq(kc, ones_ref[...]) + EPS) * g1_ref[...]
    return kc, cmp[:, KV_W:half]


def _cmp_kernel(r_ref, wab_ref, pe_ref, b1_ref, w2_ref, b2_ref, g1_ref, ones_ref, kc_ref, vc_ref, *, n_rows):
    gsum = _dot(r_ref[0], wab_ref[...])
    kc, vc = _cmp_finish(gsum, n_rows, wab_ref, pe_ref, b1_ref, w2_ref, b2_ref, g1_ref, ones_ref)
    kc_ref[0] = kc.astype(BF16)
    vc_ref[0] = vc.astype(BF16)


def _cmp_specs():
    half = 2 * KV_W
    return [_const_spec((CMP_IN, CMP_G)), _const_spec((8, CMP_IN)), _const_spec((1, half)),
            _const_spec((half, half)), _const_spec((1, half)), _const_spec((1, KV_W)),
            _const_spec((KV_W, KV_W))]


def _compress(r01, cw, batch, t_len):
    n16 = t_len // CMP_STRIDE
    x = r01.reshape(batch, n16, CMP_IN)
    blk = lambda w: pl.BlockSpec((1, n16, w), lambda b: (b, 0, 0))
    return pl.pallas_call(
        functools.partial(_cmp_kernel, n_rows=n16),
        grid=(batch,),
        in_specs=[blk(CMP_IN)] + _cmp_specs(),
        out_specs=[blk(KV_W), blk(KV_W)],
        out_shape=[jax.ShapeDtypeStruct((batch, n16, KV_W), BF16)] * 2,
        compiler_params=_params("parallel"),
        name="nsa_compress",
    )(x, *cw)


def _nsa_kernel(qpad_ref, ksel_ref, vsel_ref, kwin_ref, vwin_ref, kc_ref, vc_ref, gate_ref, mt_ref,
                et_ref, pl_ref, ge_ref, bound_ref, out_ref, sc_scr, lhs_scr, m_scr, l_scr, acc_scr, *, nb, fixed_max):
    qi = pl.program_id(2)
    tq, grp = NSA_TQ, NSA_GROUP
    tk = NSA_TK_FIXED if fixed_max else NSA_TK
    rows = grp * tq
    nbp, ncp = mt_ref.shape
    start = qi * tq
    q4 = jnp.concatenate([qpad_ref[:, g * LANES:(g + 1) * LANES] for g in range(grp)], axis=0)
    tpos = start + (lax.broadcasted_iota(jnp.int32, (rows, 1), 0) & (tq - 1))

    sc = _dot_nt(q4, kc_ref[0])
    ccol = lax.broadcasted_iota(jnp.int32, (rows, ncp), 1)
    vis = (ccol * CMP_STRIDE + (CMP_BLOCK - 1)) <= tpos
    if fixed_max:
        bound = bound_ref[0, 0]
        e = jnp.where(vis, jnp.exp(sc - bound), 0.0)
        den = jnp.sum(e, axis=-1, keepdims=True)
        p_c = e / jnp.where(den > 0.0, den, 1.0)
    else:
        s_m = jnp.where(vis, sc, NEG)
        e = jnp.where(vis, jnp.exp(s_m - jnp.max(s_m, axis=-1, keepdims=True)), 0.0)
        p_c = e / jnp.maximum(jnp.sum(e, axis=-1, keepdims=True), 1e-30)
    o_c = _dot(p_c.astype(BF16), vc_ref[0])

    pg = p_c[0:tq]
    for g in range(1, grp):
        pg = pg + p_c[g * tq:(g + 1) * tq]
    h1 = pg.astype(BF16)
    r1 = pg - h1.astype(F32)
    h2 = r1.astype(BF16)
    h3 = (r1 - h2.astype(F32)).astype(BF16)
    mt = mt_ref[...]
    p_sel = _dot_nt(mt, h1) + _dot_nt(mt, h2) + _dot_nt(mt, h3)
    blk = lax.broadcasted_iota(jnp.int32, (nbp, tq), 0)
    cur = (start + lax.broadcasted_iota(jnp.int32, (nbp, tq), 1)) // SEL_BLOCK
    forced = (blk == 0) | (blk == cur) | (blk == cur - 1)
    score = jnp.where(blk > cur, -1.0, jnp.where(forced, FORCE_SCORE, p_sel))
    if nb < nbp:
        score = jnp.where(blk >= nb, -2.0, score)
    sc_scr[...] = score

    s_grp = [score[8 * v:8 * v + 8, :] for v in range(nbp // 8)]
    sub = lax.broadcasted_iota(jnp.int32, (8, tq), 0)
    cur_max = (start + tq - 1) // SEL_BLOCK
    sizes = [nb * (b + 1) // NSA_RANK_SIZES for b in range(NSA_RANK_SIZES)]
    for b, n_used in enumerate(sizes):
        n_prev = sizes[b - 1] if b > 0 else 0

        @pl.when((cur_max >= n_prev) & (cur_max < n_used))
        def _():
            n_grp = n_used // 8
            cnt = [jnp.zeros((8, tq), F32) for _ in range(n_grp)]
            for i in range(n_used):
                row = sc_scr[i:i + 1, :]
                vi, ri = divmod(i, 8)
                for v in range(n_grp):
                    if v > vi:
                        hit = jnp.where(row >= s_grp[v], 1.0, 0.0)
                    elif v < vi:
                        hit = jnp.where(row > s_grp[v], 1.0, 0.0)
                    else:
                        hit = jnp.where(sub > ri, jnp.where(row >= s_grp[v], 1.0, 0.0),
                                        jnp.where(row > s_grp[v], 1.0, 0.0))
                    cnt[v] = cnt[v] + hit
            parts = [jnp.where(jnp.concatenate(cnt, axis=0) < SEL_TOPK, 0.0, -1.0)]
            if n_used < nbp:
                parts.append(jnp.full((nbp - n_used, tq), -1.0, F32))
            sc_scr[...] = jnp.concatenate(parts, axis=0)
    unsel_t = sc_scr[...].T.astype(BF16)
    lhs_scr[...] = jnp.concatenate([q4, jnp.concatenate([unsel_t] * grp, axis=0)], axis=1)

    l_scr[...] = jnp.zeros((rows, LANES), F32)
    acc_scr[...] = jnp.zeros((rows, LANES), F32)
    n_full = start // tk

    def key_operands(kt):
        koff = pl.multiple_of(kt * tk, tk)
        kk = jnp.concatenate([ksel_ref[pl.ds(koff, tk), :], et_ref[pl.ds(koff, tk), :]], axis=1)
        return koff, kk, vsel_ref[pl.ds(koff, tk), :]

    if fixed_max:
        bound = bound_ref[0, 0]

        def key_tile(kt, masked):
            koff, kk, vv = key_operands(kt)
            s = _dot_nt(lhs_scr[...], kk)
            if masked:
                s = jnp.where(koff + lax.broadcasted_iota(jnp.int32, (rows, tk), 1) <= tpos, s, NEG)
            p = jnp.exp(s - bound)
            part = p[:, 0:LANES]
            for c in range(1, tk // LANES):
                part = part + p[:, c * LANES:(c + 1) * LANES]
            l_scr[...] = l_scr[...] + part
            acc_scr[...] = acc_scr[...] + _dot(p.astype(BF16), vv)
    else:
        m_scr[...] = jnp.full((rows, LANES), NEG, F32)
        qpos = start + lax.broadcasted_iota(jnp.int32, (tq, tk), 0)

        def key_tile(kt, masked):
            koff, kk, vv = key_operands(kt)
            for g in range(grp):
                rs = slice(g * tq, (g + 1) * tq)
                s = _dot_nt(lhs_scr[rs, :], kk)
                if masked:
                    s = jnp.where(koff + lax.broadcasted_iota(jnp.int32, (tq, tk), 1) <= qpos, s, NEG)
                m_old = m_scr[rs, :]
                m_new = jnp.maximum(m_old, jnp.max(s, axis=-1, keepdims=True))
                p = jnp.exp(s - jnp.concatenate([m_new] * (tk // LANES), axis=1))
                alpha = jnp.exp(m_old - m_new)
                l_scr[rs, :] = alpha * l_scr[rs, :] + jnp.sum(p, axis=-1, keepdims=True)
                acc_scr[rs, :] = alpha * acc_scr[rs, :] + _dot(p.astype(BF16), vv)
                m_scr[rs, :] = m_new

    unroll = NSA_TILE_UNROLL if fixed_max else 1

    def tile_group(kq, carry):
        for u in range(unroll):
            key_tile(unroll * kq + u, False)
        return carry
    lax.fori_loop(0, n_full // unroll, tile_group, 0)
    done = (n_full // unroll) * unroll
    left = n_full - done
    size = unroll // 2
    while size >= 1:
        @pl.when((left & size) != 0)
        def _(done=done, size=size):
            for u in range(size):
                key_tile(done + u, False)
        done = done + (left & size)
        size //= 2
    key_tile(n_full, True)
    if fixed_max:
        o_s = acc_scr[...] / jnp.sum(l_scr[...], axis=-1, keepdims=True)
    else:
        o_s = acc_scr[...] / l_scr[...]

    wk = WINDOW + tq
    wstart = pl.multiple_of(jnp.maximum(start - WINDOW, 0), tq)
    s = _dot_nt(q4, kwin_ref[pl.ds(wstart, wk), :])
    wpos = wstart + lax.broadcasted_iota(jnp.int32, (rows, wk), 1)
    in_window = (wpos <= tpos) & (wpos > tpos - WINDOW)
    if fixed_max:
        e = jnp.where(in_window, jnp.exp(s - bound), 0.0)
    else:
        s = jnp.where(in_window, s, NEG)
        e = jnp.exp(s - jnp.max(s, axis=-1, keepdims=True))
    o_w = _dot(e.astype(BF16), vwin_ref[pl.ds(wstart, wk), :]) / jnp.sum(e, axis=-1, keepdims=True)

    gt = gate_ref[...]
    g_hi = gt.astype(BF16)
    g_lo = (gt - g_hi.astype(F32)).astype(BF16)
    out = None
    for branch, o_b in enumerate((o_c, o_s, o_w)):
        ob = o_b.astype(BF16)
        placed = _dot(ob[0:tq], pl_ref[0, 0])
        for g in range(1, grp):
            placed = placed + _dot(ob[g * tq:(g + 1) * tq], pl_ref[0, g])
        weight = _dot(g_hi, ge_ref[branch]) + _dot(g_lo, ge_ref[branch])
        out = placed * weight if out is None else out + placed * weight
    out_ref[...] = out.astype(BF16)


def _nsa_constants(t_len):
    nb = t_len // SEL_BLOCK
    nbp = max(LANES, -(-nb // LANES) * LANES)
    ncp = t_len // CMP_STRIDE
    j = np.arange(nbp)[:, None]
    n = np.arange(ncp)[None, :]
    first = (n >= SEL_RATIO * j) & (n <= SEL_RATIO * j + SEL_RATIO - 1)
    second = (n >= SEL_RATIO * j - 1) & (n <= SEL_RATIO * j + SEL_RATIO - 2)
    mt = (first.astype(np.float32) + second.astype(np.float32)) * (n < ncp - 1) * (j < nb)
    et = (np.arange(t_len)[:, None] // SEL_BLOCK == np.arange(nbp)[None, :]).astype(np.float32) * SEL_BIAS
    place = np.zeros((NSA_KV_HEADS, NSA_GROUP, LANES, NSA_GROUP * HEAD_DIM), np.float32)
    for h in range(NSA_KV_HEADS):
        for g in range(NSA_GROUP):
            place[h, g, h * HEAD_DIM + np.arange(HEAD_DIM), g * HEAD_DIM + np.arange(HEAD_DIM)] = 1.0
    gate_expand = np.zeros((3, LANES, NSA_GROUP * HEAD_DIM), np.float32)
    for g in range(NSA_GROUP):
        for branch in range(3):
            gate_expand[branch, g * 3 + branch, g * HEAD_DIM:(g + 1) * HEAD_DIM] = 1.0
    return (nb, jnp.asarray(mt, BF16), jnp.asarray(et, BF16), jnp.asarray(place, BF16),
            jnp.asarray(gate_expand, BF16))


def _nsa_attend(qpad, kv, kc, vc, gates, qk_g, batch, t_len):
    tq = NSA_TQ
    nq = t_len // tq
    rows = NSA_GROUP * tq
    nb, mt, et, place, gate_expand = _nsa_constants(t_len)
    nbp, ncp = mt.shape
    seq = lambda c: pl.BlockSpec((t_len, KV_W), lambda b, h, i: (b, c))
    bound = (HEAD_DIM ** 0.5) * jnp.max(jnp.abs(qk_g[0])) * jnp.max(jnp.abs(qk_g[1:4])) * 1.02 + 0.1

    def attend(fixed_max):
        return pl.pallas_call(
            functools.partial(_nsa_kernel, nb=nb, fixed_max=fixed_max),
            grid=(batch, NSA_KV_HEADS, nq),
            in_specs=[pl.BlockSpec((tq, NSA_GROUP * LANES), lambda b, h, i: (b * nq + i, h)),
                      seq(0), seq(1), seq(2), seq(3),
                      pl.BlockSpec((1, ncp, KV_W), lambda b, h, i: (b, 0, 0)),
                      pl.BlockSpec((1, ncp, KV_W), lambda b, h, i: (b, 0, 0)),
                      pl.BlockSpec((tq, LANES), lambda b, h, i: (b * nq + i, h)),
                      _const_spec((nbp, ncp)), _const_spec((t_len, nbp)),
                      pl.BlockSpec((1, NSA_GROUP, LANES, NSA_GROUP * HEAD_DIM), lambda b, h, i: (h, 0, 0, 0)),
                      _const_spec((3, LANES, NSA_GROUP * HEAD_DIM)),
                      pl.BlockSpec(memory_space=pltpu.SMEM)],
            out_specs=pl.BlockSpec((tq, NSA_GROUP * HEAD_DIM), lambda b, h, i: (b * nq + i, h)),
            out_shape=jax.ShapeDtypeStruct((batch * t_len, NSA_W), BF16),
            scratch_shapes=[pltpu.VMEM((nbp, tq), F32), pltpu.VMEM((rows, LANES + nbp), BF16),
                            pltpu.VMEM((rows, LANES), F32), pltpu.VMEM((rows, LANES), F32),
                            pltpu.VMEM((rows, LANES), F32)],
            compiler_params=_params("parallel", "parallel", "arbitrary"),
            name="nsa_attend",
        )(qpad, kv, kv, kv, kv, kc, vc, gates, mt, et, place, gate_expand, bound.reshape(1, 1))
    return lax.cond(bound <= NSA_MAX_FIXED_SHIFT, lambda: attend(True), lambda: attend(False))


def _outffn_kernel(x_ref, og_ref, on_ref, om_ref, wo_ref, g2_ref, wu_ref, wd_ref, y_ref):
    x1 = (x_ref[...] + _dot(og_ref[...], wo_ref[0:GLA_W, :])
          + _dot(on_ref[...], wo_ref[GLA_W:GLA_W + NSA_W, :])
          + _dot(om_ref[...], wo_ref[GLA_W + NSA_W:GLA_W + NSA_W + MLSTM_W, :]))
    h = x1 * lax.rsqrt(jnp.mean(x1 * x1, axis=-1, keepdims=True) + EPS) * g2_ref[...]
    hid = jnp.maximum(_dot(h.astype(BF16), wu_ref[...]), 0.0)
    y_ref[...] = x1 + _dot((hid * hid).astype(BF16), wd_ref[...])


def _out_ffn(x, og, on, om, wo, g2, wu, wd, tm):
    n = x.shape[0]
    row = lambda w: pl.BlockSpec((tm, w), lambda i: (i, 0))
    return pl.pallas_call(
        _outffn_kernel,
        grid=(n // tm,),
        in_specs=[row(D_MODEL), row(GLA_W), row(NSA_W), row(MLSTM_W),
                  _const_spec((D_MODEL, D_MODEL)), _const_spec((1, D_MODEL)),
                  _const_spec((D_MODEL, D_FF)), _const_spec((D_FF, D_MODEL))],
        out_specs=row(D_MODEL),
        out_shape=jax.ShapeDtypeStruct((n, D_MODEL), F32),
        compiler_params=_params("parallel"),
        name="out_ffn",
    )(x, og, on, om, wo, g2, wu, wd)


def _layer_weights(lp):
    return {'w_in': _pack_w_in(lp['w_in']), 'w_out': lp['w_out'].astype(BF16),
            'w_up': lp['w_up'].astype(BF16), 'w_down': lp['w_down'].astype(BF16),
            'cmp': _cmp_weights(lp)}


def _rows_on_lanes_to_state(a, n_slots):
    batch, _, n_rows = a.shape
    return jnp.transpose(a.reshape(batch, n_slots, NSA_KV_HEADS, HEAD_DIM, n_rows), (0, 4, 1, 2, 3))


def _prompt_layer(x, lp, lw, batch, t_len):
    d = HEAD_DIM
    zg, zm, qpad, rows_t, win_t, kv, r01, gates = _nsa_prep(
        None, lp['nsa_qk_g'], lp['nsa_gate_b'], 256, seq=(batch, t_len),
        proj=(x, lp['norm1_g'].reshape(1, D_MODEL), lw['w_in']))
    zero = lambda *s: jnp.zeros(s, F32)
    og, s_gla, om, c_m, n_m, m_m = _gla_mlstm(
        zg, zm, zero(batch, GLA_HEADS, d, d), zero(batch, CONV_WIDTH - 1, MLSTM_W),
        zero(batch, MLSTM_HEADS, d, d), zero(batch, MLSTM_HEADS, d), zero(batch, MLSTM_HEADS), lp,
        batch, t_len, t_len)
    kc, vc = _compress(r01, lw['cmp'], batch, t_len)
    on = _nsa_attend(qpad, kv, kc, vc, gates, lp['nsa_qk_g'], batch, t_len)
    y = _out_ffn(x, og, on, om, lw['w_out'], lp['norm2_g'].reshape(1, D_MODEL), lw['w_up'], lw['w_down'], 256)
    wlen = min(WINDOW, t_len)
    new_rows = _rows_on_lanes_to_state(rows_t, 4)
    win_state = _rows_on_lanes_to_state(win_t[:, :, t_len - wlen:], 2)
    conv_state = zm.reshape(batch, t_len, ZM_W)[:, t_len - (CONV_WIDTH - 1):, 0:MLSTM_W]
    return y, (new_rows, win_state, s_gla, c_m, n_m, m_m, conv_state)


GATHER_PAGES = 32
GROUPS_PER_PAGE = PAGE_SIZE // CMP_STRIDE


def _paged_cache_view(cache_nsa_kv):
    depth, n_pool = cache_nsa_kv.shape[0], cache_nsa_kv.shape[1]
    return jnp.transpose(cache_nsa_kv, (0, 1, 3, 4, 5, 2)).reshape(depth, n_pool, ROWS_W, PAGE_SIZE)


def _window_cache_view(cache_nsa_win):
    depth, batch, wb = cache_nsa_win.shape[0:3]
    return jnp.transpose(cache_nsa_win, (0, 1, 3, 4, 5, 2)).reshape(depth, batch, WIN_W, wb)


def _cmp_gather_kernel(pt_ref, ct_ref, w_ref, perm_ref, g_ref, buf, xs, sems, *, layer):
    s = pl.program_id(0)
    slot = s % 2

    def page_copies(step, dst):
        return [pltpu.make_async_copy(ct_ref.at[layer, pt_ref[step * GATHER_PAGES + p], pl.ds(0, 2 * KV_W), :],
                                      buf.at[dst, p], sems.at[dst, p]) for p in range(GATHER_PAGES)]

    @pl.when(s == 0)
    def _():
        for c in page_copies(0, 0):
            c.start()

    @pl.when(s + 1 < pl.num_programs(0))
    def _():
        for c in page_copies(s + 1, 1 - slot):
            c.start()
    for c in page_copies(s, slot):
        c.wait()

    perm = perm_ref[...]
    for p in range(GATHER_PAGES):
        xs[p] = _dot_nt(perm, buf[slot, p].astype(BF16))
    parts = []
    n_rows = GATHER_PAGES * GROUPS_PER_PAGE
    for sl in range(2):
        acc = jnp.zeros((n_rows, 2 * KV_W), F32)
        for rp in range(CMP_STRIDE // 2):
            xr = jnp.concatenate(
                [xs[:, r * GROUPS_PER_PAGE:(r + 1) * GROUPS_PER_PAGE, sl * KV_W:(sl + 1) * KV_W]
                 .reshape(n_rows, KV_W) for r in (2 * rp, 2 * rp + 1)], axis=1)
            acc = acc + _dot(xr.astype(BF16), w_ref[rp, sl])
        parts.append(acc)
    g_ref[0] = jnp.concatenate([parts[0][:, 0:KV_W], parts[1][:, 0:KV_W],
                                parts[0][:, KV_W:2 * KV_W], parts[1][:, KV_W:2 * KV_W]], axis=1)


def _cmp_gather(ct, layer, pt_flat, w_ab, batch, n_pages):
    assert n_pages % GATHER_PAGES == 0
    steps = n_pages // GATHER_PAGES
    rows = GATHER_PAGES * GROUPS_PER_PAGE
    w4 = w_ab.reshape(CMP_STRIDE, 2, KV_W, 2, 2, KV_W)
    w_rs = jnp.stack([w4[:, s, :, :, s, :] for s in range(2)], axis=1).reshape(CMP_STRIDE, 2, KV_W, 2 * KV_W)
    w_rs = jnp.transpose(w_rs.reshape(CMP_STRIDE // 2, 2, 2, KV_W, 2 * KV_W), (0, 2, 1, 3, 4))
    w_rs = w_rs.reshape(CMP_STRIDE // 2, 2, 2 * KV_W, 2 * KV_W)
    i = np.arange(PAGE_SIZE)
    perm = np.zeros((PAGE_SIZE, PAGE_SIZE), np.float32)
    perm[i, (i % GROUPS_PER_PAGE) * CMP_STRIDE + i // GROUPS_PER_PAGE] = 1.0
    perm = jnp.asarray(perm, BF16)
    grid_spec = pltpu.PrefetchScalarGridSpec(
        num_scalar_prefetch=1,
        grid=(batch * steps,),
        in_specs=[pl.BlockSpec(memory_space=pl.ANY),
                  pl.BlockSpec((CMP_STRIDE // 2, 2, 2 * KV_W, 2 * KV_W), lambda s, pt: (0, 0, 0, 0)),
                  pl.BlockSpec((PAGE_SIZE, PAGE_SIZE), lambda s, pt: (0, 0))],
        out_specs=pl.BlockSpec((1, rows, CMP_G), lambda s, pt: (s // steps, s % steps, 0)),
        scratch_shapes=[pltpu.VMEM((2, GATHER_PAGES, 2 * KV_W, PAGE_SIZE), F32),
                        pltpu.VMEM((GATHER_PAGES, PAGE_SIZE, 2 * KV_W), F32),
                        pltpu.SemaphoreType.DMA((2, GATHER_PAGES))],
    )
    return pl.pallas_call(
        functools.partial(_cmp_gather_kernel, layer=layer),
        grid_spec=grid_spec,
        out_shape=jax.ShapeDtypeStruct((batch, n_pages * GROUPS_PER_PAGE, CMP_G), F32),
        compiler_params=_params("arbitrary"),
        name="nsa_cmp_gather",
    )(pt_flat, ct, w_rs, perm)


def _to_col(row, n):
    eye = (lax.broadcasted_iota(jnp.int32, (n, n), 0) == lax.broadcasted_iota(jnp.int32, (n, n), 1))
    return jnp.sum(jnp.where(eye, jnp.broadcast_to(row, (n, n)), 0.0), axis=1, keepdims=True)


def _dec_cmp_kernel(g_ref, q_ref, wc_ref, wnew_ref, wab_ref, pe_ref, b1_ref, w2_ref, b2_ref, g1_ref,
                    ones_ref, m_ref, oc_ref, ow_ref, idx_ref, wout_ref, *, n_groups, nb, cur):
    nh, grp = NSA_KV_HEADS, NSA_GROUP
    q8 = q_ref[0]
    kc, vc = _cmp_finish(g_ref[0], n_groups, wab_ref, pe_ref, b1_ref, w2_ref, b2_ref, g1_ref, ones_ref)
    sc = _dot_nt(q8, kc.astype(BF16))
    vis = lax.broadcasted_iota(jnp.int32, sc.shape, 1) < n_groups - 1
    s_m = jnp.where(vis, sc, NEG)
    e = jnp.where(vis, jnp.exp(s_m - jnp.max(s_m, axis=-1, keepdims=True)), 0.0)
    p_c = e / jnp.maximum(jnp.sum(e, axis=-1, keepdims=True), 1e-30)
    oc_ref[0] = _dot(p_c.astype(BF16), vc.astype(BF16))

    nsp = m_ref.shape[1]
    pg = jnp.concatenate([jnp.sum(p_c[h * grp:(h + 1) * grp], axis=0, keepdims=True) for h in range(nh)]
                         + [jnp.zeros((8 - nh, n_groups), F32)], axis=0)
    h1 = pg.astype(BF16)
    r1 = pg - h1.astype(F32)
    h2 = r1.astype(BF16)
    h3 = (r1 - h2.astype(F32)).astype(BF16)
    m = m_ref[...]
    p_sel = _dot(h1, m) + _dot(h2, m) + _dot(h3, m)
    blk = lax.broadcasted_iota(jnp.int32, (1, nsp), 1)
    forced = (blk == 0) | (blk == cur) | (blk == cur - 1)
    ii = lax.broadcasted_iota(jnp.int32, (nsp, nsp), 0)
    jj = lax.broadcasted_iota(jnp.int32, (nsp, nsp), 1)
    slot = lax.broadcasted_iota(jnp.int32, (SEL_TOPK, nsp), 0).astype(F32)
    blk_f = lax.broadcasted_iota(jnp.int32, (SEL_TOPK, nsp), 1).astype(F32)
    for h in range(nh):
        score = jnp.where(blk > cur, -1.0, jnp.where(forced, FORCE_SCORE, p_sel[h:h + 1, :]))
        score = jnp.where(blk >= nb, -2.0, score)
        s_col = _to_col(score, nsp)
        ge = jnp.where(s_col >= score, 1.0, 0.0)
        gt = jnp.where(s_col > score, 1.0, 0.0)
        rank = jnp.sum(jnp.where(jj > ii, ge, gt), axis=0, keepdims=True)
        sel = (rank < SEL_TOPK).astype(F32)
        before = jnp.sum(jnp.where(ii < jj, _to_col(sel, nsp), 0.0), axis=0, keepdims=True)
        onehot = jnp.where((before == slot) & (sel > 0.5), 1.0, 0.0)
        idx = jnp.sum(onehot * blk_f, axis=1, keepdims=True)
        idx_ref[0, h] = jnp.broadcast_to(idx, (SEL_TOPK, LANES)).astype(jnp.int32)

    wt = wc_ref[0, 0]
    wb = wt.shape[1]
    wnew = wnew_ref[0]
    s = _dot(q8, wt[0:KV_W].astype(BF16))
    s = jnp.where(lax.broadcasted_iota(jnp.int32, s.shape, 1) > wb - WINDOW, s, NEG)
    qf = q8.astype(F32)
    s_new = jnp.sum(qf * wnew[:, 0:KV_W].astype(BF16).astype(F32), axis=-1, keepdims=True)
    mx = jnp.maximum(jnp.max(s, axis=-1, keepdims=True), s_new)
    e = jnp.exp(s - mx)
    e_new = jnp.exp(s_new - mx)
    num = _dot_nt(e.astype(BF16), wt[KV_W:2 * KV_W].astype(BF16)) + e_new * wnew[:, KV_W:2 * KV_W]
    ow_ref[0] = num / (jnp.sum(e, axis=-1, keepdims=True) + e_new)
    lane = lax.broadcasted_iota(jnp.int32, wt.shape, 1)
    wout_ref[0] = jnp.where(lane == wb - 1, _to_col(wnew, 2 * KV_W), pltpu.roll(wt, wb - 1, 1))


def _dec_constants(past):
    n_groups = past // CMP_STRIDE
    nb = past // SEL_BLOCK + 1
    nsp = -(-nb // LANES) * LANES
    j = np.arange(nsp)[None, :]
    n = np.arange(n_groups)[:, None]
    first = (n >= SEL_RATIO * j) & (n <= SEL_RATIO * j + SEL_RATIO - 1)
    second = (n >= SEL_RATIO * j - 1) & (n <= SEL_RATIO * j + SEL_RATIO - 2)
    m = (first.astype(np.float32) + second.astype(np.float32)) * (n < n_groups - 1) * (j < nb)
    return n_groups, nb, jnp.asarray(m, BF16)


def _dec_cmp_attn(gsum, q8, wt, layer, win_new, cw, batch, past):
    n_groups, nb, m = _dec_constants(past)
    nsp = m.shape[1]
    wb = wt.shape[3]
    assert wb == WINDOW
    per_b = lambda shape: pl.BlockSpec((1,) + shape, lambda b: (b,) + (0,) * len(shape))
    kern = functools.partial(_dec_cmp_kernel, n_groups=n_groups, nb=nb, cur=past // SEL_BLOCK)
    return pl.pallas_call(
        kern,
        grid=(batch,),
        in_specs=[per_b((n_groups, CMP_G)), per_b((NSA_HEADS, LANES)),
                  pl.BlockSpec((1, 1, 2 * KV_W, wb), lambda b: (layer, b, 0, 0)),
                  per_b((1, 2 * KV_W))] + _cmp_specs() + [_const_spec((n_groups, nsp))],
        out_specs=[per_b((NSA_HEADS, LANES)), per_b((NSA_HEADS, LANES)),
                   per_b((NSA_KV_HEADS, SEL_TOPK, LANES)), per_b((2 * KV_W, wb))],
        out_shape=[jax.ShapeDtypeStruct((batch, NSA_HEADS, LANES), F32),
                   jax.ShapeDtypeStruct((batch, NSA_HEADS, LANES), F32),
                   jax.ShapeDtypeStruct((batch, NSA_KV_HEADS, SEL_TOPK, LANES), jnp.int32),
                   jax.ShapeDtypeStruct((batch, 2 * KV_W, wb), F32)],
        compiler_params=_params("parallel"),
        name="nsa_decode_cmp",
    )(gsum, q8, wt, win_new, *cw, m)


def _dec_sel_kernel(pt_ref, idx_ref, q_ref, new_ref, oc_ref, ow_ref, gate_ref, ct_ref, out_ref,
                    kv_buf, sems, *, layer, n_pages, n_cache_blocks):
    b = pl.program_id(0)
    buf_slot = b % 2
    nh, nq, topk = NSA_KV_HEADS, NSA_HEADS, SEL_TOPK
    half = PAGE_SIZE // SEL_BLOCK
    nblk = nh * topk

    def page_copies(seq, dst):
        copies = []
        for j in range(nblk):
            blk = jnp.minimum(idx_ref[seq * nblk + j], n_cache_blocks - 1)
            page = pt_ref[seq * n_pages + blk // half]
            copies.append(pltpu.make_async_copy(ct_ref.at[layer, page, pl.ds(2 * KV_W, 2 * KV_W), :],
                                                kv_buf.at[dst, j], sems.at[dst, j]))
        return copies

    @pl.when(b == 0)
    def _():
        for c in page_copies(0, 0):
            c.start()

    @pl.when(b + 1 < pl.num_programs(0))
    def _():
        for c in page_copies(b + 1, 1 - buf_slot):
            c.start()
    for c in page_copies(b, buf_slot):
        c.wait()

    q8 = q_ref[0]
    new = new_ref[0]
    s_new = jnp.sum(q8 * new[:, 2 * KV_W:3 * KV_W].astype(BF16).astype(F32), axis=-1, keepdims=True)
    v_new = new[:, 3 * KV_W:4 * KV_W]
    qb = q8.astype(BF16)
    page_half = lax.broadcasted_iota(jnp.int32, (1, PAGE_SIZE), 1) // SEL_BLOCK
    o_heads = []
    for h in range(nh):
        s_parts = []
        m = s_new
        for k in range(topk):
            j = h * topk + k
            blk = idx_ref[b * nblk + j]
            s_k = _dot(qb, kv_buf[buf_slot, j, 0:KV_W, :].astype(BF16))
            keep = jnp.where(blk < n_cache_blocks, 0.0, NEG)
            s_k = s_k + jnp.where(page_half == blk % half, keep, NEG)
            s_parts.append(s_k)
            m = jnp.maximum(m, jnp.max(s_k, axis=-1, keepdims=True))
        e_new = jnp.exp(s_new - m)
        den = e_new
        num = e_new * v_new
        for k in range(topk):
            e = jnp.exp(s_parts[k] - m)
            den = den + jnp.sum(e, axis=-1, keepdims=True)
            num = num + _dot_nt(e.astype(BF16), kv_buf[buf_slot, h * topk + k, KV_W:2 * KV_W, :].astype(BF16))
        o_heads.append(num / den)
    row = lax.broadcasted_iota(jnp.int32, (nq, KV_W), 0)
    o_s = jnp.where(row < NSA_GROUP, o_heads[0], o_heads[1])
    gt = gate_ref[0]
    out_ref[0] = gt[:, 0:1] * oc_ref[0] + gt[:, 1:2] * o_s + gt[:, 2:3] * ow_ref[0]


def _dec_sel_attn(ct, layer, pt_flat, idx_flat, q8, rows_new, o_c, o_w, gates, batch, n_pages):
    half = PAGE_SIZE // SEL_BLOCK
    nh, grp, nq = NSA_KV_HEADS, NSA_GROUP, NSA_HEADS
    g3 = gates.reshape(batch, nh, LANES)[:, :, 0:3 * grp].reshape(batch, nq, 3)
    g8 = jnp.concatenate([g3, jnp.zeros((batch, nq, LANES - 3), F32)], axis=-1)
    new8 = jnp.broadcast_to(rows_new.reshape(batch, 1, ROWS_W), (batch, nq, ROWS_W))

    per_b = lambda w: pl.BlockSpec((1, nq, w), lambda b, pt, idx: (b, 0, 0))
    grid_spec = pltpu.PrefetchScalarGridSpec(
        num_scalar_prefetch=2,
        grid=(batch,),
        in_specs=[per_b(LANES), per_b(ROWS_W), per_b(LANES), per_b(LANES), per_b(LANES),
                  pl.BlockSpec(memory_space=pl.ANY)],
        out_specs=per_b(LANES),
        scratch_shapes=[pltpu.VMEM((2, nh * SEL_TOPK, 2 * KV_W, PAGE_SIZE), F32),
                        pltpu.SemaphoreType.DMA((2, nh * SEL_TOPK))],
    )
    kern = functools.partial(_dec_sel_kernel, layer=layer, n_pages=n_pages, n_cache_blocks=n_pages * half)
    return pl.pallas_call(
        kern,
        grid_spec=grid_spec,
        out_shape=jax.ShapeDtypeStruct((batch, nq, LANES), F32),
        compiler_params=_params("arbitrary"),
        name="nsa_decode_sel",
    )(pt_flat, idx_flat, q8.astype(F32), new8, o_c, o_w, g8, ct)


def _sample_layer(x, lp, lw, layer, ct, wt, gla_s0, c0, n0, m0, conv0, page_table):
    batch, n_pages = page_table.shape
    past = n_pages * PAGE_SIZE
    d = HEAD_DIM
    c = SCAN_CHUNK_DECODE
    zg, zm, qpad, rows, win, _, _, gates = _nsa_prep(
        None, lp['nsa_qk_g'], lp['nsa_gate_b'], batch,
        proj=(x, lp['norm1_g'].reshape(1, D_MODEL), lw['w_in']))
    pad = lambda z: jnp.pad(z[:, None, :], ((0, 0), (0, c - 1), (0, 0))).reshape(batch * c, z.shape[-1])
    first = lambda o: o.reshape(batch, c, o.shape[-1])[:, 0]
    og, s_gla, om, c_m, n_m, m_m = _gla_mlstm(pad(zg), pad(zm), gla_s0, conv0, c0, n0, m0, lp, batch, c, 1, c)
    pt_flat = page_table.reshape(-1)
    gsum = _cmp_gather(ct, layer, pt_flat, lw['cmp'][0], batch, n_pages)
    q8 = qpad.reshape(batch, NSA_HEADS, LANES)
    o_c, o_w, idx, win_t = _dec_cmp_attn(gsum, q8, wt, layer, win.reshape(batch, 1, 2 * KV_W), lw['cmp'],
                                         batch, past)
    o_n = _dec_sel_attn(ct, layer, pt_flat, idx[:, :, :, 0].reshape(-1), q8, rows, o_c, o_w, gates,
                        batch, n_pages)
    on = jnp.stack([o_n[:, h * NSA_GROUP:(h + 1) * NSA_GROUP, h * d:(h + 1) * d]
                    for h in range(NSA_KV_HEADS)], axis=1)
    on = on.reshape(batch, NSA_W).astype(BF16)
    y = _out_ffn(x, first(og), on, first(om), lw['w_out'], lp['norm2_g'].reshape(1, D_MODEL),
                 lw['w_up'], lw['w_down'], batch)
    new_rows = rows.reshape(batch, 1, 4, NSA_KV_HEADS, d)
    win_state = _rows_on_lanes_to_state(win_t, 2)
    conv_state = jnp.concatenate([conv0.astype(F32), zm[:, None, 0:MLSTM_W]], axis=1)[:, 1:]
    return y, (new_rows, win_state, s_gla, c_m, n_m, m_m, conv_state)


def kernel(x_prompt, x_sample, cache_nsa_kv, cache_nsa_win, state_gla, state_mlstm_c, state_mlstm_n,
           state_mlstm_m, state_mlstm_conv, page_table, norm1_g, w_in, gla_a_w, gla_a_b, gla_norm_g,
           nsa_qk_g, nsa_gate_b, cmp_pe, cmp_w1, cmp_b1, cmp_w2, cmp_b2, ml_conv_w, ml_conv_b, ml_wq, ml_wk,
           ml_gate_b, ml_norm_g, ml_skip, w_out, norm2_g, w_up, w_down):
    bp, t_len, _ = x_prompt.shape
    n_dec = x_sample.shape[0]
    depth = w_in.shape[0]
    x_p = x_prompt.reshape(bp * t_len, D_MODEL)
    x_s = x_sample.reshape(n_dec, D_MODEL)
    states_p, states_s = [], []
    ct = _paged_cache_view(cache_nsa_kv)
    wt = _window_cache_view(cache_nsa_win)
    for l in range(depth):
        lp = {'norm1_g': norm1_g[l], 'w_in': w_in[l], 'gla_a_w': gla_a_w[l], 'gla_a_b': gla_a_b[l],
              'gla_norm_g': gla_norm_g[l], 'nsa_qk_g': nsa_qk_g[l], 'nsa_gate_b': nsa_gate_b[l],
              'cmp_pe': cmp_pe[l], 'cmp_w1': cmp_w1[l], 'cmp_b1': cmp_b1[l], 'cmp_w2': cmp_w2[l],
              'cmp_b2': cmp_b2[l], 'ml_conv_w': ml_conv_w[l], 'ml_conv_b': ml_conv_b[l], 'ml_wq': ml_wq[l],
              'ml_wk': ml_wk[l], 'ml_gate_b': ml_gate_b[l], 'ml_norm_g': ml_norm_g[l], 'ml_skip': ml_skip[l],
              'w_out': w_out[l], 'norm2_g': norm2_g[l], 'w_up': w_up[l], 'w_down': w_down[l]}
        lw = _layer_weights(lp)
        x_p, st_p = _prompt_layer(x_p, lp, lw, bp, t_len)
        x_s, st_s = _sample_layer(x_s, lp, lw, l, ct, wt, state_gla[l],
                                  state_mlstm_c[l], state_mlstm_n[l], state_mlstm_m[l], state_mlstm_conv[l],
                                  page_table)
        states_p.append(st_p)
        states_s.append(st_s)
    outs = [x_p.reshape(bp, t_len, D_MODEL), x_s.reshape(n_dec, 1, D_MODEL)]
    for i in range(7):
        outs.append(jnp.stack([s[i] for s in states_p]))
        outs.append(jnp.stack([s[i] for s in states_s]))
    return tuple(outs)
```

```python
import functools

import numpy as np
import jax
import jax.numpy as jnp
from jax import lax
from jax.experimental import pallas as pl
from jax.experimental.pallas import tpu as pltpu

F32 = jnp.float32
BF16 = jnp.bfloat16

D_MODEL = 1024
HEAD_DIM = 64
PAGE_SIZE = 128
GLA_HEADS = 4
GLA_LOWRANK = 16
GLA_TAU = 16.0
NSA_HEADS = 8
NSA_KV_HEADS = 2
NSA_GROUP = NSA_HEADS // NSA_KV_HEADS
CMP_BLOCK = 32
CMP_STRIDE = 16
CMP_HIDDEN = 64
SEL_BLOCK = 64
SEL_RATIO = SEL_BLOCK // CMP_STRIDE
SEL_TOPK = 16
WINDOW = 512
FORCE_SCORE = 1.0e4
MLSTM_HEADS = 4
CONV_WIDTH = 4
D_FF = 4 * D_MODEL
EPS = 1e-6

GLA_W = GLA_HEADS * HEAD_DIM
NSA_W = NSA_HEADS * HEAD_DIM
KV_W = NSA_KV_HEADS * HEAD_DIM
MLSTM_W = MLSTM_HEADS * HEAD_DIM

LANES = 128
ZG_W = 4 * GLA_W + LANES
GATE_W = NSA_KV_HEADS * LANES
ZN_W = NSA_W + 6 * KV_W + GATE_W
ZM_W = 3 * MLSTM_W + LANES
ROWS_W = 4 * KV_W
WIN_W = 2 * KV_W
QPAD_W = NSA_HEADS * LANES

SCAN_CHUNK = 128
SCAN_CHUNK_DECODE = 32
GLA_SUB = 32
NSA_TQ = 128
NSA_TK = 256
NSA_TK_FIXED = 512
NSA_MAX_FIXED_SHIFT = 40.0
NSA_RANK_SIZES = 4
NSA_TILE_UNROLL = 8
NEG = -1.0e30
SEL_BIAS = 29952.0

VMEM_LIMIT = 56 * 1024 * 1024


def _dot(a, b):
    return jnp.dot(a, b, preferred_element_type=F32)


def _dot_hi(a, b):
    return jnp.dot(a, b, preferred_element_type=F32, precision=lax.Precision.HIGHEST)


def _dot_nt(a, b):
    return lax.dot_general(a, b, (((1,), (1,)), ((), ())), preferred_element_type=F32)


def _dot_nt_hi(a, b):
    return lax.dot_general(a, b, (((1,), (1,)), ((), ())), preferred_element_type=F32,
                           precision=lax.Precision.HIGHEST)


def _dot_tn_hi(a, b):
    return lax.dot_general(a, b, (((0,), (0,)), ((), ())), preferred_element_type=F32,
                           precision=lax.Precision.HIGHEST)


def _sigmoid(x):
    return 1.0 / (1.0 + jnp.exp(-x))


def _log_sigmoid(x):
    return jnp.minimum(x, 0.0) - jnp.log(1.0 + jnp.exp(-jnp.abs(x)))


def _group_sum(x, ones_bd):
    hi = x.astype(BF16)
    lo = (x - hi.astype(F32)).astype(BF16)
    return _dot(hi, ones_bd) + _dot(lo, ones_bd)


def _group_mean_sq(x, ones_bd):
    return _group_sum(x * x, ones_bd) * (1.0 / HEAD_DIM)


def _params(*sem):
    return pltpu.CompilerParams(dimension_semantics=sem, vmem_limit_bytes=VMEM_LIMIT)


def _const_spec(shape):
    nd = len(shape)
    return pl.BlockSpec(shape, lambda *_: (0,) * nd)


def _pack_w_in(w_in):
    def cols(a, b):
        return w_in[:, a:b]

    def zeros(n):
        return jnp.zeros((D_MODEL, n), w_in.dtype)
    o_nsa = 4 * GLA_W + GLA_LOWRANK
    o_ng = o_nsa + NSA_W + 6 * KV_W
    o_ml = o_ng + 3 * NSA_HEADS
    o_mi = o_ml + 2 * MLSTM_W
    o_mo = o_mi + 2 * MLSTM_HEADS
    n_gate = 3 * NSA_GROUP
    parts = [cols(0, o_nsa), zeros(LANES - GLA_LOWRANK),
             cols(o_nsa, o_ng),
             cols(o_ng, o_ng + n_gate), zeros(LANES - n_gate),
             cols(o_ng + n_gate, o_ml), zeros(LANES - n_gate),
             cols(o_ml, o_mi), cols(o_mo, o_mo + MLSTM_W), cols(o_mi, o_mo),
             zeros(LANES - 2 * MLSTM_HEADS)]
    return jnp.concatenate(parts, axis=1).astype(BF16)


def _nsa_prep_kernel(zn_ref, gq_ref, gr_ref, gw_ref, gb_ref, ones_ref, place_ref,
                     qpad_ref, rows_ref, win_ref, kv_ref, r01_ref, gate_ref, *, transposed):
    ones_bd = ones_ref[...]
    q = zn_ref[:, 0:NSA_W]
    qn = q * lax.rsqrt(_group_mean_sq(q, ones_bd) + EPS) * gq_ref[...]
    qpad_ref[...] = _dot(qn.astype(BF16), place_ref[...]).astype(BF16)

    r = zn_ref[:, NSA_W:NSA_W + ROWS_W]
    col = lax.broadcasted_iota(jnp.int32, r.shape, 1)
    rn = r * lax.rsqrt(_group_mean_sq(r, ones_bd) + EPS) * gr_ref[...]
    rows = jnp.where((col >= 2 * KV_W) & (col < 3 * KV_W), rn, r)
    if transposed:
        rows_ref[0] = rows.T
    else:
        rows_ref[...] = rows
    r01_ref[...] = rows[:, 0:2 * KV_W].astype(BF16)

    w = zn_ref[:, NSA_W + ROWS_W:NSA_W + ROWS_W + WIN_W]
    colw = lax.broadcasted_iota(jnp.int32, w.shape, 1)
    wn = w * lax.rsqrt(_group_mean_sq(w, ones_bd[0:WIN_W, 0:WIN_W]) + EPS) * gw_ref[...]
    win = jnp.where(colw < KV_W, wn, w)
    if transposed:
        win_ref[0] = win.T
    else:
        win_ref[...] = win
    kv_ref[:, 0:2 * KV_W] = rows[:, 2 * KV_W:4 * KV_W].astype(BF16)
    kv_ref[:, 2 * KV_W:4 * KV_W] = win.astype(BF16)

    gate_ref[...] = _sigmoid(zn_ref[:, NSA_W + ROWS_W + WIN_W:ZN_W] + gb_ref[...])


def _nsa_prep(zn, qk_g, gate_b, tm, seq=None, proj=None):
    n = zn.shape[0] if proj is None else proj[0].shape[0]
    scale = HEAD_DIM ** -0.5
    gq = (jnp.tile(qk_g[0], NSA_HEADS) * scale).reshape(1, NSA_W)
    gr = jnp.tile(qk_g[2], ROWS_W // HEAD_DIM).reshape(1, ROWS_W)
    gw = jnp.tile(qk_g[3], WIN_W // HEAD_DIM).reshape(1, WIN_W)
    n_gate = 3 * NSA_GROUP
    gpad = jnp.zeros((LANES - n_gate,), F32)
    gb = jnp.concatenate([gate_b[0:n_gate], gpad, gate_b[n_gate:], gpad]).reshape(1, GATE_W)
    grp = np.arange(NSA_W) // HEAD_DIM
    ones_bd = jnp.asarray(grp[:, None] == grp[None, :], BF16)
    src = np.arange(NSA_W)
    head, d = src // HEAD_DIM, src % HEAD_DIM
    dst = head * LANES + (head // NSA_GROUP) * HEAD_DIM + d
    place = np.zeros((NSA_W, QPAD_W), np.float32)
    place[src, dst] = 1.0
    place = jnp.asarray(place, BF16)
    row = lambda w: pl.BlockSpec((tm, w), lambda i: (i, 0))
    if seq is None:
        state_spec = row
        state_shape = lambda w: jax.ShapeDtypeStruct((n, w), F32)
    else:
        batch, t_len = seq
        nblk = t_len // tm
        state_spec = lambda w: pl.BlockSpec((1, w, tm), lambda i: (i // nblk, 0, i % nblk))
        state_shape = lambda w: jax.ShapeDtypeStruct((batch, w, t_len), F32)
    prep_in_specs = [_const_spec((1, NSA_W)), _const_spec((1, ROWS_W)), _const_spec((1, WIN_W)),
                     _const_spec((1, GATE_W)), _const_spec((NSA_W, NSA_W)), _const_spec((NSA_W, QPAD_W))]
    prep_out_specs = [row(QPAD_W), state_spec(ROWS_W), state_spec(WIN_W), row(4 * KV_W), row(2 * KV_W),
                      row(GATE_W)]
    prep_out_shape = [jax.ShapeDtypeStruct((n, QPAD_W), BF16), state_shape(ROWS_W), state_shape(WIN_W),
                      jax.ShapeDtypeStruct((n, 4 * KV_W), BF16), jax.ShapeDtypeStruct((n, 2 * KV_W), BF16),
                      jax.ShapeDtypeStruct((n, GATE_W), F32)]
    if proj is None:
        return pl.pallas_call(
            functools.partial(_nsa_prep_kernel, transposed=seq is not None),
            grid=(n // tm,),
            in_specs=[row(ZN_W)] + prep_in_specs,
            out_specs=prep_out_specs,
            out_shape=prep_out_shape,
            compiler_params=_params("parallel"),
            name="nsa_prep",
        )(zn, gq, gr, gw, gb, ones_bd, place)
    x, g, w = proj
    return pl.pallas_call(
        functools.partial(_inproj_prep_kernel, transposed=seq is not None),
        grid=(n // tm,),
        in_specs=[row(D_MODEL), _const_spec((1, D_MODEL)), _const_spec(w.shape)] + prep_in_specs,
        out_specs=[row(ZG_W), row(ZM_W)] + prep_out_specs,
        out_shape=[jax.ShapeDtypeStruct((n, ZG_W), F32), jax.ShapeDtypeStruct((n, ZM_W), F32)] + prep_out_shape,
        compiler_params=_params("parallel"),
        name="in_proj_nsa_prep",
    )(x, g, w, gq, gr, gw, gb, ones_bd, place)


def _inproj_prep_kernel(x_ref, g_ref, w_ref, gq_ref, gr_ref, gw_ref, gb_ref, ones_ref, place_ref,
                        zg_ref, zm_ref, qpad_ref, rows_ref, win_ref, kv_ref, r01_ref, gate_ref, *, transposed):
    x = x_ref[...]
    h = x * lax.rsqrt(jnp.mean(x * x, axis=-1, keepdims=True) + EPS) * g_ref[...]
    hb = h.astype(BF16)
    zg_ref[...] = _dot(hb, w_ref[:, 0:ZG_W])
    zm_ref[...] = _dot(hb, w_ref[:, ZG_W + ZN_W:ZG_W + ZN_W + ZM_W])
    zn = _dot(hb, w_ref[:, ZG_W:ZG_W + ZN_W])
    _nsa_prep_kernel(zn, gq_ref, gr_ref, gw_ref, gb_ref, ones_ref, place_ref,
                     qpad_ref, rows_ref, win_ref, kv_ref, r01_ref, gate_ref, transposed=transposed)


def _load_block_diag(bd_scr, heads_ref):
    nh, d = heads_ref.shape[1], heads_ref.shape[2]
    bd_scr[...] = jnp.zeros(bd_scr.shape, F32)
    for h in range(nh):
        bd_scr[h * d:(h + 1) * d, h * d:(h + 1) * d] = heads_ref[0, h]


def _store_block_diag(heads_ref, bd_scr):
    nh, d = heads_ref.shape[1], heads_ref.shape[2]
    for h in range(nh):
        heads_ref[0, h] = bd_scr[h * d:(h + 1) * d, h * d:(h + 1) * d]


def _gla_init(zg_ref, s0_ref, aw_ref, ab_ref, ng_ref, tri_ref, ones_ref, og_ref, sout_ref, s_scr):
    @pl.when(pl.program_id(1) == 0)
    def _():
        _load_block_diag(s_scr, s0_ref)


def _gla_fin(zg_ref, s0_ref, aw_ref, ab_ref, ng_ref, tri_ref, ones_ref, og_ref, sout_ref, s_scr):
    @pl.when(pl.program_id(1) == pl.num_programs(1) - 1)
    def _():
        _store_block_diag(sout_ref, s_scr)


def _gla_body(zg_ref, s0_ref, aw_ref, ab_ref, ng_ref, tri_ref, ones_ref, og_ref, sout_ref, s_scr,
              *, t_valid, t_pad):
    i = pl.program_id(1)
    c = zg_ref.shape[0]
    d = HEAD_DIM
    q = zg_ref[:, 0:GLA_W] * (d ** -0.5)
    k = zg_ref[:, GLA_W:2 * GLA_W]
    v = zg_ref[:, 2 * GLA_W:3 * GLA_W]
    r = zg_ref[:, 3 * GLA_W:4 * GLA_W]
    ga = zg_ref[:, 4 * GLA_W:ZG_W]
    g = _log_sigmoid(_dot(ga.astype(BF16), aw_ref[...]) + ab_ref[...]) * (1.0 / GLA_TAU)
    if t_valid < t_pad:
        valid = (i * c + lax.broadcasted_iota(jnp.int32, (c, 1), 0)) < t_valid
        g = jnp.where(valid, g, 0.0)
        k = jnp.where(valid, k, 0.0)
    bcum = _dot_hi(tri_ref[...], g)
    b_end = bcum[c - 1:c, :]
    nh, w, sub = GLA_HEADS, GLA_W, min(GLA_SUB, c)
    lane_head = lax.broadcasted_iota(jnp.int32, (1, w), 1) // d
    s_bd = s_scr[...]
    o = _dot((q * jnp.exp(bcum)).astype(BF16), s_bd.astype(BF16))
    vb = v.astype(BF16)
    o_sub = []
    for j in range(c // sub):
        lo, hi = j * sub, (j + 1) * sub
        base = bcum[lo - 1:lo, :] if j > 0 else jnp.zeros((1, w), F32)
        qt = q[lo:hi] * jnp.exp(bcum[lo:hi] - base)
        kt = (k[0:hi] * jnp.exp(base - bcum[0:hi])).astype(BF16)
        qs = jnp.concatenate([jnp.where(lane_head == h, qt, 0.0) for h in range(nh)], axis=0)
        sc = _dot_nt(qs.astype(BF16), kt)
        t_row = lo + (lax.broadcasted_iota(jnp.int32, (nh * sub, hi), 0) % sub)
        sc = jnp.where(lax.broadcasted_iota(jnp.int32, (nh * sub, hi), 1) <= t_row, sc, 0.0)
        ov = _dot(sc.astype(BF16), vb[0:hi])
        o_j = jnp.where(lane_head == 0, ov[0:sub], 0.0)
        for h in range(1, nh):
            o_j = jnp.where(lane_head == h, ov[h * sub:(h + 1) * sub], o_j)
        o_sub.append(o_j)
    o = o + jnp.concatenate(o_sub, axis=0)
    last = (lax.broadcasted_iota(jnp.int32, (c, LANES), 0) == c - 1).astype(F32)
    decay_col = jnp.exp(_dot_tn_hi(bcum, last))
    k_hat = k * jnp.exp(b_end - bcum)
    same_head = (lax.broadcasted_iota(jnp.int32, (w, w), 0) // d) == (lax.broadcasted_iota(jnp.int32, (w, w), 1) // d)
    s_scr[...] = (jnp.concatenate([decay_col] * (w // LANES), axis=1) * s_bd
                  + jnp.where(same_head, _dot_tn_hi(k_hat, v), 0.0))
    on = o * lax.rsqrt(_group_mean_sq(o, ones_ref[...]) + EPS) * ng_ref[...]
    og_ref[...] = (on * (r * _sigmoid(r))).astype(BF16)


def _run_scans(scans, batch, nblk, name):
    counts = [(len(s['in_specs']), len(s['out_specs']), len(s['scratch'])) for s in scans]
    n_in, n_out = sum(c[0] for c in counts), sum(c[1] for c in counts)

    def kernel(*refs):
        views, o_in, o_out, o_scr = [], 0, n_in, n_in + n_out
        for ni, no, ns in counts:
            views.append(refs[o_in:o_in + ni] + refs[o_out:o_out + no] + refs[o_scr:o_scr + ns])
            o_in, o_out, o_scr = o_in + ni, o_out + no, o_scr + ns
        for phase in range(3):
            for s, view in zip(scans, views):
                s['phases'][phase](*view)
    outs = pl.pallas_call(
        kernel,
        grid=(batch, nblk),
        in_specs=[x for s in scans for x in s['in_specs']],
        out_specs=[x for s in scans for x in s['out_specs']],
        out_shape=[x for s in scans for x in s['out_shape']],
        scratch_shapes=[x for s in scans for x in s['scratch']],
        compiler_params=_params("parallel", "arbitrary"),
        name=name,
    )(*[x for s in scans for x in s['operands']])
    split, o = [], 0
    for _, no, _ in counts:
        split.append(outs[o:o + no])
        o += no
    return split


def _gla_scan(zg, s0, a_w, a_b, norm_g, batch, t_pad, t_valid, c):
    nblk = t_pad // c
    aw = jnp.concatenate([a_w, jnp.zeros((LANES - GLA_LOWRANK, GLA_W), F32)], axis=0).astype(BF16)
    tri = jnp.asarray(np.tril(np.ones((c, c), np.float32)))
    grp = np.arange(GLA_W) // HEAD_DIM
    ones_bd = jnp.asarray(grp[:, None] == grp[None, :], BF16)
    state = pl.BlockSpec((1, GLA_HEADS, HEAD_DIM, HEAD_DIM), lambda b, i: (b, 0, 0, 0))
    return dict(
        phases=(_gla_init, functools.partial(_gla_body, t_valid=t_valid, t_pad=t_pad), _gla_fin),
        in_specs=[pl.BlockSpec((c, ZG_W), lambda b, i: (b * nblk + i, 0)), state,
                  _const_spec((LANES, GLA_W)), _const_spec((1, GLA_W)), _const_spec((1, GLA_W)),
                  _const_spec((c, c)), _const_spec((GLA_W, GLA_W))],
        out_specs=[pl.BlockSpec((c, GLA_W), lambda b, i: (b * nblk + i, 0)), state],
        out_shape=[jax.ShapeDtypeStruct((batch * t_pad, GLA_W), BF16),
                   jax.ShapeDtypeStruct((batch, GLA_HEADS, HEAD_DIM, HEAD_DIM), F32)],
        scratch=[pltpu.VMEM((GLA_W, GLA_W), F32)],
        operands=(zg, s0, aw, a_b.reshape(1, GLA_W), jnp.tile(norm_g, GLA_HEADS).reshape(1, GLA_W), tri, ones_bd))


def _gla(zg, s0, a_w, a_b, norm_g, batch, t_pad, t_valid, c=SCAN_CHUNK):
    (out,) = _run_scans([_gla_scan(zg, s0, a_w, a_b, norm_g, batch, t_pad, t_valid, c)], batch, t_pad // c,
                        "gla_scan")
    return out


def _mlstm_init(zm_ref, conv0_ref, c0_ref, n0_ref, m0_ref, cw_ref, cb_ref, wq_ref, wk_ref, gb_ref,
                ng_ref, skip_ref, tri_ref, ones_ref, om_ref, cout_ref, nout_ref, mout_ref,
                c_scr, n_scr, m_scr, ext_scr):
    @pl.when(pl.program_id(1) == 0)
    def _():
        _load_block_diag(c_scr, c0_ref)
        n_scr[...] = n0_ref[0]
        m_scr[...] = m0_ref[0]
        ext_scr[0:8, :] = conv0_ref[0]


def _mlstm_fin(zm_ref, conv0_ref, c0_ref, n0_ref, m0_ref, cw_ref, cb_ref, wq_ref, wk_ref, gb_ref,
               ng_ref, skip_ref, tri_ref, ones_ref, om_ref, cout_ref, nout_ref, mout_ref,
               c_scr, n_scr, m_scr, ext_scr):
    @pl.when(pl.program_id(1) == pl.num_programs(1) - 1)
    def _():
        _store_block_diag(cout_ref, c_scr)
        nout_ref[0] = n_scr[...]
        mout_ref[0] = m_scr[...]


def _mlstm_body(zm_ref, conv0_ref, c0_ref, n0_ref, m0_ref, cw_ref, cb_ref, wq_ref, wk_ref, gb_ref,
                ng_ref, skip_ref, tri_ref, ones_ref, om_ref, cout_ref, nout_ref, mout_ref,
                c_scr, n_scr, m_scr, ext_scr, *, t_valid, t_pad):
    i = pl.program_id(1)
    c = zm_ref.shape[0]
    d = HEAD_DIM
    nh = MLSTM_HEADS
    mu = zm_ref[:, 0:MLSTM_W]
    ext_scr[8:8 + c, :] = mu
    u_conv = cb_ref[...] + mu * cw_ref[CONV_WIDTH - 1:CONV_WIDTH, :]
    for j in range(1, CONV_WIDTH):
        u_conv = u_conv + ext_scr[8 - j:8 - j + c, :] * cw_ref[CONV_WIDTH - 1 - j:CONV_WIDTH - j, :]
    ext_scr[0:8, :] = mu[c - 8:c, :]
    u_act = u_conv * _sigmoid(u_conv)
    ub = u_act.astype(BF16)
    q = _dot(ub, wq_ref[...])
    k = _dot(ub, wk_ref[...]) * (d ** -0.5)
    v = zm_ref[:, MLSTM_W:2 * MLSTM_W]
    og = zm_ref[:, 2 * MLSTM_W:3 * MLSTM_W]
    gz = zm_ref[:, 3 * MLSTM_W:ZM_W] + gb_ref[...]
    lane = lax.broadcasted_iota(jnp.int32, (c, LANES), 1)
    x = jnp.where(lane < nh, gz, _log_sigmoid(gz))
    if t_valid < t_pad:
        valid = (i * c + lax.broadcasted_iota(jnp.int32, (c, 1), 0)) < t_valid
        x = jnp.where(valid, x, jnp.where(lane < nh, NEG, 0.0))
    fc = _dot_hi(tri_ref[...], x)
    x = jnp.where(lane < nh, x, fc)
    sel = (lax.broadcasted_iota(jnp.int32, (8, LANES), 0)
           == lax.broadcasted_iota(jnp.int32, (8, LANES), 1)).astype(F32)
    xt = _dot_nt_hi(sel, x)
    row_i = lax.broadcasted_iota(jnp.int32, (c, c), 0)
    col_i = lax.broadcasted_iota(jnp.int32, (c, c), 1)
    w = MLSTM_W
    lane_head = lax.broadcasted_iota(jnp.int32, (1, w), 1) // d
    c_bd = c_scr[...]
    n_all = n_scr[0:1, :]
    m_all = m_scr[0:1, :]
    qb = q.astype(BF16)
    qs = jnp.concatenate([jnp.where(lane_head == h, q, 0.0) for h in range(nh)], axis=0)
    qk_all = _dot_nt(qs.astype(BF16), k.astype(BF16))
    qn_sum = _group_sum(q * n_all, ones_ref[...])
    zc, zr = jnp.zeros((c, w), F32), jnp.zeros((1, w), F32)
    mt_l, ws_l, rs_l, wl_l, wc_l, mn_l = zc, zc, zc, zc, zr, zr
    p_parts = []
    for h in range(nh):
        i_col, f_col = x[:, h:h + 1], x[:, nh + h:nh + h + 1]
        i_row, f_row = xt[h:h + 1, :], xt[nh + h:nh + h + 1, :]
        a = f_col + m_all[:, h * d:h * d + 1]
        dmat = jnp.where(col_i <= row_i, f_col - f_row + i_row, NEG)
        m_t = jnp.maximum(a, jnp.max(dmat, axis=-1, keepdims=True))
        p_h = qk_all[h * c:(h + 1) * c] * jnp.exp(dmat - m_t)
        p_parts.append(p_h)
        m_new = m_t[c - 1:c, :]
        on_head = lane_head == h
        mt_l = jnp.where(on_head, m_t, mt_l)
        ws_l = jnp.where(on_head, jnp.exp(a - m_t), ws_l)
        rs_l = jnp.where(on_head, jnp.sum(p_h, axis=-1, keepdims=True), rs_l)
        wl_l = jnp.where(on_head, jnp.exp(f_col[c - 1:c, :] - f_col + i_col - m_new), wl_l)
        wc_l = jnp.where(on_head, jnp.exp(a[c - 1:c, :] - m_new), wc_l)
        mn_l = jnp.where(on_head, m_new, mn_l)
    nv = _dot(jnp.concatenate(p_parts, axis=0).astype(BF16), v.astype(BF16))
    num = jnp.where(lane_head == 0, nv[0:c], 0.0)
    for h in range(1, nh):
        num = jnp.where(lane_head == h, nv[h * c:(h + 1) * c], num)
    num = num + ws_l * _dot(qb, c_bd.astype(BF16))
    den = rs_l + ws_l * qn_sum
    hm = num / jnp.maximum(jnp.abs(den), jnp.exp(-mt_l))
    kw = k * wl_l
    same_head = (lax.broadcasted_iota(jnp.int32, (w, w), 0) // d) == (lax.broadcasted_iota(jnp.int32, (w, w), 1) // d)
    c_scr[...] = wc_l * c_bd + jnp.where(same_head, _dot_tn_hi(kw, v), 0.0)
    n_scr[0:1, :] = wc_l * n_all + jnp.sum(kw, axis=0, keepdims=True)
    m_scr[0:1, :] = mn_l
    hn = hm * lax.rsqrt(_group_mean_sq(hm, ones_ref[...]) + EPS) * ng_ref[...]
    om_ref[...] = (_sigmoid(og) * (hn + skip_ref[...] * u_act)).astype(BF16)


def _block_diag_heads(w):
    nh, d, _ = w.shape
    eye = jnp.eye(nh, dtype=w.dtype)
    return jnp.einsum('hde,hg->hdge', w, eye).reshape(nh * d, nh * d)


def _mlstm_scan(zm, conv0, c0, n0, m0, lp, batch, t_pad, t_valid, c):
    nblk = t_pad // c
    nh, d = MLSTM_HEADS, HEAD_DIM
    conv0p = jnp.concatenate([jnp.zeros((batch, 8 - (CONV_WIDTH - 1), MLSTM_W), F32), conv0.astype(F32)], axis=1)
    pad7 = jnp.zeros((batch, 7, MLSTM_W), F32)
    n0p = jnp.concatenate([n0.reshape(batch, 1, MLSTM_W), pad7], axis=1)
    m0p = jnp.concatenate([jnp.repeat(m0, d, axis=1).reshape(batch, 1, MLSTM_W), pad7], axis=1)
    gb = jnp.concatenate([lp['ml_gate_b'][0], lp['ml_gate_b'][1],
                          jnp.zeros((LANES - 2 * nh,), F32)]).reshape(1, LANES)
    tri = jnp.asarray(np.tril(np.ones((c, c), np.float32)))
    grp = np.arange(MLSTM_W) // d
    ones_bd = jnp.asarray(grp[:, None] == grp[None, :], BF16)
    per_b = lambda shape: pl.BlockSpec((1,) + shape, lambda b, i: (b,) + (0,) * len(shape))
    return dict(
        phases=(_mlstm_init, functools.partial(_mlstm_body, t_valid=t_valid, t_pad=t_pad), _mlstm_fin),
        in_specs=[pl.BlockSpec((c, ZM_W), lambda b, i: (b * nblk + i, 0)),
                  per_b((8, MLSTM_W)), per_b((nh, d, d)), per_b((8, MLSTM_W)), per_b((8, MLSTM_W)),
                  _const_spec((CONV_WIDTH, MLSTM_W)), _const_spec((1, MLSTM_W)),
                  _const_spec((MLSTM_W, MLSTM_W)), _const_spec((MLSTM_W, MLSTM_W)),
                  _const_spec((1, LANES)), _const_spec((1, MLSTM_W)), _const_spec((1, MLSTM_W)),
                  _const_spec((c, c)), _const_spec((MLSTM_W, MLSTM_W))],
        out_specs=[pl.BlockSpec((c, MLSTM_W), lambda b, i: (b * nblk + i, 0)),
                   per_b((nh, d, d)), per_b((8, MLSTM_W)), per_b((8, MLSTM_W))],
        out_shape=[jax.ShapeDtypeStruct((batch * t_pad, MLSTM_W), BF16),
                   jax.ShapeDtypeStruct((batch, nh, d, d), F32),
                   jax.ShapeDtypeStruct((batch, 8, MLSTM_W), F32),
                   jax.ShapeDtypeStruct((batch, 8, MLSTM_W), F32)],
        scratch=[pltpu.VMEM((MLSTM_W, MLSTM_W), F32), pltpu.VMEM((8, MLSTM_W), F32),
                 pltpu.VMEM((8, MLSTM_W), F32), pltpu.VMEM((8 + c, MLSTM_W), F32)],
        operands=(zm, conv0p, c0, n0p, m0p, lp['ml_conv_w'], lp['ml_conv_b'].reshape(1, MLSTM_W),
                  _block_diag_heads(lp['ml_wq']).astype(BF16), _block_diag_heads(lp['ml_wk']).astype(BF16),
                  gb, jnp.tile(lp['ml_norm_g'], nh).reshape(1, MLSTM_W), lp['ml_skip'].reshape(1, MLSTM_W),
                  tri, ones_bd))


def _mlstm_states(outs, batch):
    om, c_f, n_f, m_f = outs
    nh, d = MLSTM_HEADS, HEAD_DIM
    return om, c_f, n_f[:, 0].reshape(batch, nh, d), m_f[:, 0].reshape(batch, nh, d)[:, :, 0]


def _mlstm(zm, conv0, c0, n0, m0, lp, batch, t_pad, t_valid, c=SCAN_CHUNK):
    (outs,) = _run_scans([_mlstm_scan(zm, conv0, c0, n0, m0, lp, batch, t_pad, t_valid, c)], batch, t_pad // c,
                         "mlstm_scan")
    return _mlstm_states(outs, batch)


def _gla_mlstm(zg, zm, gla_s0, conv0, c0, n0, m0, lp, batch, t_pad, t_valid, c=SCAN_CHUNK):
    g_out, m_out = _run_scans(
        [_gla_scan(zg, gla_s0, lp['gla_a_w'], lp['gla_a_b'], lp['gla_norm_g'], batch, t_pad, t_valid, c),
         _mlstm_scan(zm, conv0, c0, n0, m0, lp, batch, t_pad, t_valid, c)], batch, t_pad // c, "gla_mlstm_scan")
    return tuple(g_out) + _mlstm_states(m_out, batch)


CMP_IN = CMP_STRIDE * 2 * KV_W
CMP_G = 2 * 2 * KV_W


def _cmp_weights(lp):
    w1 = lp['cmp_w1'].reshape(2, 2, CMP_STRIDE, HEAD_DIM, CMP_HIDDEN)
    eye = jnp.eye(2, dtype=F32)
    w_ab = jnp.einsum('sarde,st,hg->rshdatge', w1, eye, eye).reshape(CMP_IN, CMP_G)
    pe = lp['cmp_pe'].reshape(2, 2, CMP_STRIDE, 1, HEAD_DIM)
    pe = jnp.broadcast_to(jnp.transpose(pe, (1, 2, 0, 3, 4)),
                          (2, CMP_STRIDE, 2, NSA_KV_HEADS, HEAD_DIM)).reshape(2, CMP_IN)
    pe8 = jnp.concatenate([pe, jnp.zeros((6, CMP_IN), F32)], axis=0)
    b1 = jnp.broadcast_to(lp['cmp_b1'][:, None, :], (2, NSA_KV_HEADS, CMP_HIDDEN)).reshape(1, 2 * KV_W)
    w2 = jnp.einsum('sed,st,hg->shetgd', lp['cmp_w2'], eye, eye).reshape(2 * KV_W, 2 * KV_W)
    b2 = jnp.broadcast_to(lp['cmp_b2'][:, None, :], (2, NSA_KV_HEADS, HEAD_DIM)).reshape(1, 2 * KV_W)
    g1 = jnp.tile(lp['nsa_qk_g'][1], NSA_KV_HEADS).reshape(1, KV_W)
    grp = np.arange(KV_W) // HEAD_DIM
    ones_bd = jnp.asarray(grp[:, None] == grp[None, :], BF16)
    return (w_ab.astype(BF16), pe8.astype(BF16), b1, w2.astype(BF16), b2, g1, ones_bd)


def _cmp_finish(gsum, n_rows, wab_ref, pe_ref, b1_ref, w2_ref, b2_ref, g1_ref, ones_ref):
    half = 2 * KV_W
    g_pe = _dot(pe_ref[...], wab_ref[...])
    bias = g_pe[0:1, 0:half] + g_pe[1:2, half:CMP_G] + b1_ref[...]
    hid = gsum[:, 0:half] + pltpu.roll(gsum[:, half:CMP_G], n_rows - 1, 0) + bias
    act = hid * _sigmoid(hid)
    cmp = _dot(act.astype(BF16), w2_ref[...]) + b2_ref[...]
    kc = cmp[:, 0:KV_W]
    kc = kc * lax.rsqrt(_group_mean_sq(kc, ones_ref[...]) + EPS) * g1_ref[...]
    return kc, cmp[:, KV_W:half]


def _cmp_kernel(r_ref, wab_ref, pe_ref, b1_ref, w2_ref, b2_ref, g1_ref, ones_ref, kc_ref, vc_ref, *, n_rows):
    gsum = _dot(r_ref[0], wab_ref[...])
    kc, vc = _cmp_finish(gsum, n_rows, wab_ref, pe_ref, b1_ref, w2_ref, b2_ref, g1_ref, ones_ref)
    kc_ref[0] = kc.astype(BF16)
    vc_ref[0] = vc.astype(BF16)


def _cmp_specs():
    half = 2 * KV_W
    return [_const_spec((CMP_IN, CMP_G)), _const_spec((8, CMP_IN)), _const_spec((1, half)),
            _const_spec((half, half)), _const_spec((1, half)), _const_spec((1, KV_W)),
            _const_spec((KV_W, KV_W))]


def _compress(r01, cw, batch, t_len):
    n16 = t_len // CMP_STRIDE
    x = r01.reshape(batch, n16, CMP_IN)
    blk = lambda w: pl.BlockSpec((1, n16, w), lambda b: (b, 0, 0))
    return pl.pallas_call(
        functools.partial(_cmp_kernel, n_rows=n16),
        grid=(batch,),
        in_specs=[blk(CMP_IN)] + _cmp_specs(),
        out_specs=[blk(KV_W), blk(KV_W)],
        out_shape=[jax.ShapeDtypeStruct((batch, n16, KV_W), BF16)] * 2,
        compiler_params=_params("parallel"),
        name="nsa_compress",
    )(x, *cw)


def _nsa_kernel(qpad_ref, ksel_ref, vsel_ref, kwin_ref, vwin_ref, kc_ref, vc_ref, gate_ref, mt_ref,
                et_ref, pl_ref, ge_ref, bound_ref, out_ref, sc_scr, lhs_scr, m_scr, l_scr, acc_scr, *, nb, fixed_max):
    qi = pl.program_id(2)
    tq, grp = NSA_TQ, NSA_GROUP
    tk = NSA_TK_FIXED if fixed_max else NSA_TK
    rows = grp * tq
    nbp, ncp = mt_ref.shape
    start = qi * tq
    q4 = jnp.concatenate([qpad_ref[:, g * LANES:(g + 1) * LANES] for g in range(grp)], axis=0)
    tpos = start + (lax.broadcasted_iota(jnp.int32, (rows, 1), 0) & (tq - 1))

    sc = _dot_nt(q4, kc_ref[0])
    ccol = lax.broadcasted_iota(jnp.int32, (rows, ncp), 1)
    vis = (ccol * CMP_STRIDE + (CMP_BLOCK - 1)) <= tpos
    if fixed_max:
        bound = bound_ref[0, 0]
        e = jnp.where(vis, jnp.exp(sc - bound), 0.0)
        den = jnp.sum(e, axis=-1, keepdims=True)
        p_c = e / jnp.where(den > 0.0, den, 1.0)
    else:
        s_m = jnp.where(vis, sc, NEG)
        e = jnp.where(vis, jnp.exp(s_m - jnp.max(s_m, axis=-1, keepdims=True)), 0.0)
        p_c = e / jnp.maximum(jnp.sum(e, axis=-1, keepdims=True), 1e-30)
    o_c = _dot(p_c.astype(BF16), vc_ref[0])

    pg = p_c[0:tq]
    for g in range(1, grp):
        pg = pg + p_c[g * tq:(g + 1) * tq]
    h1 = pg.astype(BF16)
    r1 = pg - h1.astype(F32)
    h2 = r1.astype(BF16)
    h3 = (r1 - h2.astype(F32)).astype(BF16)
    mt = mt_ref[...]
    p_sel = _dot_nt(mt, h1) + _dot_nt(mt, h2) + _dot_nt(mt, h3)
    blk = lax.broadcasted_iota(jnp.int32, (nbp, tq), 0)
    cur = (start + lax.broadcasted_iota(jnp.int32, (nbp, tq), 1)) // SEL_BLOCK
    forced = (blk == 0) | (blk == cur) | (blk == cur - 1)
    score = jnp.where(blk > cur, -1.0, jnp.where(forced, FORCE_SCORE, p_sel))
    if nb < nbp:
        score = jnp.where(blk >= nb, -2.0, score)
    sc_scr[...] = score

    s_grp = [score[8 * v:8 * v + 8, :] for v in range(nbp // 8)]
    sub = lax.broadcasted_iota(jnp.int32, (8, tq), 0)
    cur_max = (start + tq - 1) // SEL_BLOCK
    sizes = [nb * (b + 1) // NSA_RANK_SIZES for b in range(NSA_RANK_SIZES)]
    for b, n_used in enumerate(sizes):
        n_prev = sizes[b - 1] if b > 0 else 0

        @pl.when((cur_max >= n_prev) & (cur_max < n_used))
        def _():
            n_grp = n_used // 8
            cnt = [jnp.zeros((8, tq), F32) for _ in range(n_grp)]
            for i in range(n_used):
                row = sc_scr[i:i + 1, :]
                vi, ri = divmod(i, 8)
                for v in range(n_grp):
                    if v > vi:
                        hit = jnp.where(row >= s_grp[v], 1.0, 0.0)
                    elif v < vi:
                        hit = jnp.where(row > s_grp[v], 1.0, 0.0)
                    else:
                        hit = jnp.where(sub > ri, jnp.where(row >= s_grp[v], 1.0, 0.0),
                                        jnp.where(row > s_grp[v], 1.0, 0.0))
                    cnt[v] = cnt[v] + hit
            parts = [jnp.where(jnp.concatenate(cnt, axis=0) < SEL_TOPK, 0.0, -1.0)]
            if n_used < nbp:
                parts.append(jnp.full((nbp - n_used, tq), -1.0, F32))
            sc_scr[...] = jnp.concatenate(parts, axis=0)
    unsel_t = sc_scr[...].T.astype(BF16)
    lhs_scr[...] = jnp.concatenate([q4, jnp.concatenate([unsel_t] * grp, axis=0)], axis=1)

    l_scr[...] = jnp.zeros((rows, LANES), F32)
    acc_scr[...] = jnp.zeros((rows, LANES), F32)
    n_full = start // tk

    def key_operands(kt):
        koff = pl.multiple_of(kt * tk, tk)
        kk = jnp.concatenate([ksel_ref[pl.ds(koff, tk), :], et_ref[pl.ds(koff, tk), :]], axis=1)
        return koff, kk, vsel_ref[pl.ds(koff, tk), :]

    if fixed_max:
        bound = bound_ref[0, 0]

        def key_tile(kt, masked):
            koff, kk, vv = key_operands(kt)
            s = _dot_nt(lhs_scr[...], kk)
            if masked:
                s = jnp.where(koff + lax.broadcasted_iota(jnp.int32, (rows, tk), 1) <= tpos, s, NEG)
            p = jnp.exp(s - bound)
            part = p[:, 0:LANES]
            for c in range(1, tk // LANES):
                part = part + p[:, c * LANES:(c + 1) * LANES]
            l_scr[...] = l_scr[...] + part
            acc_scr[...] = acc_scr[...] + _dot(p.astype(BF16), vv)
    else:
        m_scr[...] = jnp.full((rows, LANES), NEG, F32)
        qpos = start + lax.broadcasted_iota(jnp.int32, (tq, tk), 0)

        def key_tile(kt, masked):
            koff, kk, vv = key_operands(kt)
            for g in range(grp):
                rs = slice(g * tq, (g + 1) * tq)
                s = _dot_nt(lhs_scr[rs, :], kk)
                if masked:
                    s = jnp.where(koff + lax.broadcasted_iota(jnp.int32, (tq, tk), 1) <= qpos, s, NEG)
                m_old = m_scr[rs, :]
                m_new = jnp.maximum(m_old, jnp.max(s, axis=-1, keepdims=True))
                p = jnp.exp(s - jnp.concatenate([m_new] * (tk // LANES), axis=1))
                alpha = jnp.exp(m_old - m_new)
                l_scr[rs, :] = alpha * l_scr[rs, :] + jnp.sum(p, axis=-1, keepdims=True)
                acc_scr[rs, :] = alpha * acc_scr[rs, :] + _dot(p.astype(BF16), vv)
                m_scr[rs, :] = m_new

    unroll = NSA_TILE_UNROLL if fixed_max else 1

    def tile_group(kq, carry):
        for u in range(unroll):
            key_tile(unroll * kq + u, False)
        return carry
    lax.fori_loop(0, n_full // unroll, tile_group, 0)
    done = (n_full // unroll) * unroll
    left = n_full - done
    size = unroll // 2
    while size >= 1:
        @pl.when((left & size) != 0)
        def _(done=done, size=size):
            for u in range(size):
                key_tile(done + u, False)
        done = done + (left & size)
        size //= 2
    key_tile(n_full, True)
    if fixed_max:
        o_s = acc_scr[...] / jnp.sum(l_scr[...], axis=-1, keepdims=True)
    else:
        o_s = acc_scr[...] / l_scr[...]

    wk = WINDOW + tq
    wstart = pl.multiple_of(jnp.maximum(start - WINDOW, 0), tq)
    s = _dot_nt(q4, kwin_ref[pl.ds(wstart, wk), :])
    wpos = wstart + lax.broadcasted_iota(jnp.int32, (rows, wk), 1)
    in_window = (wpos <= tpos) & (wpos > tpos - WINDOW)
    if fixed_max:
        e = jnp.where(in_window, jnp.exp(s - bound), 0.0)
    else:
        s = jnp.where(in_window, s, NEG)
        e = jnp.exp(s - jnp.max(s, axis=-1, keepdims=True))
    o_w = _dot(e.astype(BF16), vwin_ref[pl.ds(wstart, wk), :]) / jnp.sum(e, axis=-1, keepdims=True)

    gt = gate_ref[...]
    g_hi = gt.astype(BF16)
    g_lo = (gt - g_hi.astype(F32)).astype(BF16)
    out = None
    for branch, o_b in enumerate((o_c, o_s, o_w)):
        ob = o_b.astype(BF16)
        placed = _dot(ob[0:tq], pl_ref[0, 0])
        for g in range(1, grp):
            placed = placed + _dot(ob[g * tq:(g + 1) * tq], pl_ref[0, g])
        weight = _dot(g_hi, ge_ref[branch]) + _dot(g_lo, ge_ref[branch])
        out = placed * weight if out is None else out + placed * weight
    out_ref[...] = out.astype(BF16)


def _nsa_constants(t_len):
    nb = t_len // SEL_BLOCK
    nbp = max(LANES, -(-nb // LANES) * LANES)
    ncp = t_len // CMP_STRIDE
    j = np.arange(nbp)[:, None]
    n = np.arange(ncp)[None, :]
    first = (n >= SEL_RATIO * j) & (n <= SEL_RATIO * j + SEL_RATIO - 1)
    second = (n >= SEL_RATIO * j - 1) & (n <= SEL_RATIO * j + SEL_RATIO - 2)
    mt = (first.astype(np.float32) + second.astype(np.float32)) * (n < ncp - 1) * (j < nb)
    et = (np.arange(t_len)[:, None] // SEL_BLOCK == np.arange(nbp)[None, :]).astype(np.float32) * SEL_BIAS
    place = np.zeros((NSA_KV_HEADS, NSA_GROUP, LANES, NSA_GROUP * HEAD_DIM), np.float32)
    for h in range(NSA_KV_HEADS):
        for g in range(NSA_GROUP):
            place[h, g, h * HEAD_DIM + np.arange(HEAD_DIM), g * HEAD_DIM + np.arange(HEAD_DIM)] = 1.0
    gate_expand = np.zeros((3, LANES, NSA_GROUP * HEAD_DIM), np.float32)
    for g in range(NSA_GROUP):
        for branch in range(3):
            gate_expand[branch, g * 3 + branch, g * HEAD_DIM:(g + 1) * HEAD_DIM] = 1.0
    return (nb, jnp.asarray(mt, BF16), jnp.asarray(et, BF16), jnp.asarray(place, BF16),
            jnp.asarray(gate_expand, BF16))


def _nsa_attend(qpad, kv, kc, vc, gates, qk_g, batch, t_len):
    tq = NSA_TQ
    nq = t_len // tq
    rows = NSA_GROUP * tq
    nb, mt, et, place, gate_expand = _nsa_constants(t_len)
    nbp, ncp = mt.shape
    seq = lambda c: pl.BlockSpec((t_len, KV_W), lambda b, h, i: (b, c))
    bound = (HEAD_DIM ** 0.5) * jnp.max(jnp.abs(qk_g[0])) * jnp.max(jnp.abs(qk_g[1:4])) * 1.02 + 0.1

    def attend(fixed_max):
        return pl.pallas_call(
            functools.partial(_nsa_kernel, nb=nb, fixed_max=fixed_max),
            grid=(batch, NSA_KV_HEADS, nq),
            in_specs=[pl.BlockSpec((tq, NSA_GROUP * LANES), lambda b, h, i: (b * nq + i, h)),
                      seq(0), seq(1), seq(2), seq(3),
                      pl.BlockSpec((1, ncp, KV_W), lambda b, h, i: (b, 0, 0)),
                      pl.BlockSpec((1, ncp, KV_W), lambda b, h, i: (b, 0, 0)),
                      pl.BlockSpec((tq, LANES), lambda b, h, i: (b * nq + i, h)),
                      _const_spec((nbp, ncp)), _const_spec((t_len, nbp)),
                      pl.BlockSpec((1, NSA_GROUP, LANES, NSA_GROUP * HEAD_DIM), lambda b, h, i: (h, 0, 0, 0)),
                      _const_spec((3, LANES, NSA_GROUP * HEAD_DIM)),
                      pl.BlockSpec(memory_space=pltpu.SMEM)],
            out_specs=pl.BlockSpec((tq, NSA_GROUP * HEAD_DIM), lambda b, h, i: (b * nq + i, h)),
            out_shape=jax.ShapeDtypeStruct((batch * t_len, NSA_W), BF16),
            scratch_shapes=[pltpu.VMEM((nbp, tq), F32), pltpu.VMEM((rows, LANES + nbp), BF16),
                            pltpu.VMEM((rows, LANES), F32), pltpu.VMEM((rows, LANES), F32),
                            pltpu.VMEM((rows, LANES), F32)],
            compiler_params=_params("parallel", "parallel", "arbitrary"),
            name="nsa_attend",
        )(qpad, kv, kv, kv, kv, kc, vc, gates, mt, et, place, gate_expand, bound.reshape(1, 1))
    return lax.cond(bound <= NSA_MAX_FIXED_SHIFT, lambda: attend(True), lambda: attend(False))


def _outffn_kernel(x_ref, og_ref, on_ref, om_ref, wo_ref, g2_ref, wu_ref, wd_ref, y_ref):
    x1 = (x_ref[...] + _dot(og_ref[...], wo_ref[0:GLA_W, :])
          + _dot(on_ref[...], wo_ref[GLA_W:GLA_W + NSA_W, :])
          + _dot(om_ref[...], wo_ref[GLA_W + NSA_W:GLA_W + NSA_W + MLSTM_W, :]))
    h = x1 * lax.rsqrt(jnp.mean(x1 * x1, axis=-1, keepdims=True) + EPS) * g2_ref[...]
    hid = jnp.maximum(_dot(h.astype(BF16), wu_ref[...]), 0.0)
    y_ref[...] = x1 + _dot((hid * hid).astype(BF16), wd_ref[...])


def _out_ffn(x, og, on, om, wo, g2, wu, wd, tm):
    n = x.shape[0]
    row = lambda w: pl.BlockSpec((tm, w), lambda i: (i, 0))
    resident = lambda shape: pl.BlockSpec(shape, lambda i: (0, 0), pipeline_mode=pl.Buffered(1))
    return pl.pallas_call(
        _outffn_kernel,
        grid=(n // tm,),
        in_specs=[row(D_MODEL), row(GLA_W), row(NSA_W), row(MLSTM_W),
                  resident((D_MODEL, D_MODEL)), _const_spec((1, D_MODEL)),
                  resident((D_MODEL, D_FF)), resident((D_FF, D_MODEL))],
        out_specs=row(D_MODEL),
        out_shape=jax.ShapeDtypeStruct((n, D_MODEL), F32),
        compiler_params=_params("parallel"),
        name="out_ffn",
    )(x, og, on, om, wo, g2, wu, wd)


def _layer_weights(lp):
    return {'w_in': _pack_w_in(lp['w_in']), 'w_out': lp['w_out'].astype(BF16),
            'w_up': lp['w_up'].astype(BF16), 'w_down': lp['w_down'].astype(BF16),
            'cmp': _cmp_weights(lp)}


def _rows_on_lanes_to_state(a, n_slots):
    batch, _, n_rows = a.shape
    return jnp.transpose(a.reshape(batch, n_slots, NSA_KV_HEADS, HEAD_DIM, n_rows), (0, 4, 1, 2, 3))


def _prompt_layer(x, lp, lw, batch, t_len):
    d = HEAD_DIM
    zg, zm, qpad, rows_t, win_t, kv, r01, gates = _nsa_prep(
        None, lp['nsa_qk_g'], lp['nsa_gate_b'], 256, seq=(batch, t_len),
        proj=(x, lp['norm1_g'].reshape(1, D_MODEL), lw['w_in']))
    zero = lambda *s: jnp.zeros(s, F32)
    og, s_gla, om, c_m, n_m, m_m = _gla_mlstm(
        zg, zm, zero(batch, GLA_HEADS, d, d), zero(batch, CONV_WIDTH - 1, MLSTM_W),
        zero(batch, MLSTM_HEADS, d, d), zero(batch, MLSTM_HEADS, d), zero(batch, MLSTM_HEADS), lp,
        batch, t_len, t_len)
    kc, vc = _compress(r01, lw['cmp'], batch, t_len)
    on = _nsa_attend(qpad, kv, kc, vc, gates, lp['nsa_qk_g'], batch, t_len)
    y = _out_ffn(x, og, on, om, lw['w_out'], lp['norm2_g'].reshape(1, D_MODEL), lw['w_up'], lw['w_down'], 512)
    wlen = min(WINDOW, t_len)
    new_rows = _rows_on_lanes_to_state(rows_t, 4)
    win_state = _rows_on_lanes_to_state(win_t[:, :, t_len - wlen:], 2)
    conv_state = zm.reshape(batch, t_len, ZM_W)[:, t_len - (CONV_WIDTH - 1):, 0:MLSTM_W]
    return y, (new_rows, win_state, s_gla, c_m, n_m, m_m, conv_state)


GATHER_PAGES = 32
GROUPS_PER_PAGE = PAGE_SIZE // CMP_STRIDE


def _paged_cache_view(cache_nsa_kv):
    depth, n_pool = cache_nsa_kv.shape[0], cache_nsa_kv.shape[1]
    return jnp.transpose(cache_nsa_kv, (0, 1, 3, 4, 5, 2)).reshape(depth, n_pool, ROWS_W, PAGE_SIZE)


def _window_cache_view(cache_nsa_win):
    depth, batch, wb = cache_nsa_win.shape[0:3]
    return jnp.transpose(cache_nsa_win, (0, 1, 3, 4, 5, 2)).reshape(depth, batch, WIN_W, wb)


def _cmp_gather_kernel(pt_ref, ct_ref, w_ref, perm_ref, g_ref, buf, xs, sems, *, layer):
    s = pl.program_id(0)
    slot = s % 2

    def page_copies(step, dst):
        return [pltpu.make_async_copy(ct_ref.at[layer, pt_ref[step * GATHER_PAGES + p], pl.ds(0, 2 * KV_W), :],
                                      buf.at[dst, p], sems.at[dst, p]) for p in range(GATHER_PAGES)]

    @pl.when(s == 0)
    def _():
        for c in page_copies(0, 0):
            c.start()

    @pl.when(s + 1 < pl.num_programs(0))
    def _():
        for c in page_copies(s + 1, 1 - slot):
            c.start()
    for c in page_copies(s, slot):
        c.wait()

    perm = perm_ref[...]
    for p in range(GATHER_PAGES):
        xs[p] = _dot_nt(perm, buf[slot, p].astype(BF16))
    parts = []
    n_rows = GATHER_PAGES * GROUPS_PER_PAGE
    for sl in range(2):
        acc = jnp.zeros((n_rows, 2 * KV_W), F32)
        for rp in range(CMP_STRIDE // 2):
            xr = jnp.concatenate(
                [xs[:, r * GROUPS_PER_PAGE:(r + 1) * GROUPS_PER_PAGE, sl * KV_W:(sl + 1) * KV_W]
                 .reshape(n_rows, KV_W) for r in (2 * rp, 2 * rp + 1)], axis=1)
            acc = acc + _dot(xr.astype(BF16), w_ref[rp, sl])
        parts.append(acc)
    g_ref[0] = jnp.concatenate([parts[0][:, 0:KV_W], parts[1][:, 0:KV_W],
                                parts[0][:, KV_W:2 * KV_W], parts[1][:, KV_W:2 * KV_W]], axis=1)


def _cmp_gather(ct, layer, pt_flat, w_ab, batch, n_pages):
    assert n_pages % GATHER_PAGES == 0
    steps = n_pages // GATHER_PAGES
    rows = GATHER_PAGES * GROUPS_PER_PAGE
    w4 = w_ab.reshape(CMP_STRIDE, 2, KV_W, 2, 2, KV_W)
    w_rs = jnp.stack([w4[:, s, :, :, s, :] for s in range(2)], axis=1).reshape(CMP_STRIDE, 2, KV_W, 2 * KV_W)
    w_rs = jnp.transpose(w_rs.reshape(CMP_STRIDE // 2, 2, 2, KV_W, 2 * KV_W), (0, 2, 1, 3, 4))
    w_rs = w_rs.reshape(CMP_STRIDE // 2, 2, 2 * KV_W, 2 * KV_W)
    i = np.arange(PAGE_SIZE)
    perm = np.zeros((PAGE_SIZE, PAGE_SIZE), np.float32)
    perm[i, (i % GROUPS_PER_PAGE) * CMP_STRIDE + i // GROUPS_PER_PAGE] = 1.0
    perm = jnp.asarray(perm, BF16)
    grid_spec = pltpu.PrefetchScalarGridSpec(
        num_scalar_prefetch=1,
        grid=(batch * steps,),
        in_specs=[pl.BlockSpec(memory_space=pl.ANY),
                  pl.BlockSpec((CMP_STRIDE // 2, 2, 2 * KV_W, 2 * KV_W), lambda s, pt: (0, 0, 0, 0)),
                  pl.BlockSpec((PAGE_SIZE, PAGE_SIZE), lambda s, pt: (0, 0))],
        out_specs=pl.BlockSpec((1, rows, CMP_G), lambda s, pt: (s // steps, s % steps, 0)),
        scratch_shapes=[pltpu.VMEM((2, GATHER_PAGES, 2 * KV_W, PAGE_SIZE), F32),
                        pltpu.VMEM((GATHER_PAGES, PAGE_SIZE, 2 * KV_W), F32),
                        pltpu.SemaphoreType.DMA((2, GATHER_PAGES))],
    )
    return pl.pallas_call(
        functools.partial(_cmp_gather_kernel, layer=layer),
        grid_spec=grid_spec,
        out_shape=jax.ShapeDtypeStruct((batch, n_pages * GROUPS_PER_PAGE, CMP_G), F32),
        compiler_params=_params("arbitrary"),
        name="nsa_cmp_gather",
    )(pt_flat, ct, w_rs, perm)


def _to_col(row, n):
    eye = (lax.broadcasted_iota(jnp.int32, (n, n), 0) == lax.broadcasted_iota(jnp.int32, (n, n), 1))
    return jnp.sum(jnp.where(eye, jnp.broadcast_to(row, (n, n)), 0.0), axis=1, keepdims=True)


def _dec_cmp_kernel(g_ref, q_ref, wc_ref, wnew_ref, wab_ref, pe_ref, b1_ref, w2_ref, b2_ref, g1_ref,
                    ones_ref, m_ref, oc_ref, ow_ref, idx_ref, wout_ref, *, n_groups, nb, cur):
    nh, grp = NSA_KV_HEADS, NSA_GROUP
    q8 = q_ref[0]
    kc, vc = _cmp_finish(g_ref[0], n_groups, wab_ref, pe_ref, b1_ref, w2_ref, b2_ref, g1_ref, ones_ref)
    sc = _dot_nt(q8, kc.astype(BF16))
    vis = lax.broadcasted_iota(jnp.int32, sc.shape, 1) < n_groups - 1
    s_m = jnp.where(vis, sc, NEG)
    e = jnp.where(vis, jnp.exp(s_m - jnp.max(s_m, axis=-1, keepdims=True)), 0.0)
    p_c = e / jnp.maximum(jnp.sum(e, axis=-1, keepdims=True), 1e-30)
    oc_ref[0] = _dot(p_c.astype(BF16), vc.astype(BF16))

    nsp = m_ref.shape[1]
    pg = jnp.concatenate([jnp.sum(p_c[h * grp:(h + 1) * grp], axis=0, keepdims=True) for h in range(nh)]
                         + [jnp.zeros((8 - nh, n_groups), F32)], axis=0)
    h1 = pg.astype(BF16)
    r1 = pg - h1.astype(F32)
    h2 = r1.astype(BF16)
    h3 = (r1 - h2.astype(F32)).astype(BF16)
    m = m_ref[...]
    p_sel = _dot(h1, m) + _dot(h2, m) + _dot(h3, m)
    blk = lax.broadcasted_iota(jnp.int32, (1, nsp), 1)
    forced = (blk == 0) | (blk == cur) | (blk == cur - 1)
    ii = lax.broadcasted_iota(jnp.int32, (nsp, nsp), 0)
    jj = lax.broadcasted_iota(jnp.int32, (nsp, nsp), 1)
    slot = lax.broadcasted_iota(jnp.int32, (SEL_TOPK, nsp), 0).astype(F32)
    blk_f = lax.broadcasted_iota(jnp.int32, (SEL_TOPK, nsp), 1).astype(F32)
    for h in range(nh):
        score = jnp.where(blk > cur, -1.0, jnp.where(forced, FORCE_SCORE, p_sel[h:h + 1, :]))
        score = jnp.where(blk >= nb, -2.0, score)
        s_col = _to_col(score, nsp)
        ge = jnp.where(s_col >= score, 1.0, 0.0)
        gt = jnp.where(s_col > score, 1.0, 0.0)
        rank = jnp.sum(jnp.where(jj > ii, ge, gt), axis=0, keepdims=True)
        sel = (rank < SEL_TOPK).astype(F32)
        before = jnp.sum(jnp.where(ii < jj, _to_col(sel, nsp), 0.0), axis=0, keepdims=True)
        onehot = jnp.where((before == slot) & (sel > 0.5), 1.0, 0.0)
        idx = jnp.sum(onehot * blk_f, axis=1, keepdims=True)
        idx_ref[0, h] = jnp.broadcast_to(idx, (SEL_TOPK, LANES)).astype(jnp.int32)

    wt = wc_ref[0, 0]
    wb = wt.shape[1]
    wnew = wnew_ref[0]
    s = _dot(q8, wt[0:KV_W].astype(BF16))
    s = jnp.where(lax.broadcasted_iota(jnp.int32, s.shape, 1) > wb - WINDOW, s, NEG)
    qf = q8.astype(F32)
    s_new = jnp.sum(qf * wnew[:, 0:KV_W].astype(BF16).astype(F32), axis=-1, keepdims=True)
    mx = jnp.maximum(jnp.max(s, axis=-1, keepdims=True), s_new)
    e = jnp.exp(s - mx)
    e_new = jnp.exp(s_new - mx)
    num = _dot_nt(e.astype(BF16), wt[KV_W:2 * KV_W].astype(BF16)) + e_new * wnew[:, KV_W:2 * KV_W]
    ow_ref[0] = num / (jnp.sum(e, axis=-1, keepdims=True) + e_new)
    lane = lax.broadcasted_iota(jnp.int32, wt.shape, 1)
    wout_ref[0] = jnp.where(lane == wb - 1, _to_col(wnew, 2 * KV_W), pltpu.roll(wt, wb - 1, 1))


def _dec_constants(past):
    n_groups = past // CMP_STRIDE
    nb = past // SEL_BLOCK + 1
    nsp = -(-nb // LANES) * LANES
    j = np.arange(nsp)[None, :]
    n = np.arange(n_groups)[:, None]
    first = (n >= SEL_RATIO * j) & (n <= SEL_RATIO * j + SEL_RATIO - 1)
    second = (n >= SEL_RATIO * j - 1) & (n <= SEL_RATIO * j + SEL_RATIO - 2)
    m = (first.astype(np.float32) + second.astype(np.float32)) * (n < n_groups - 1) * (j < nb)
    return n_groups, nb, jnp.asarray(m, BF16)


def _dec_cmp_attn(gsum, q8, wt, layer, win_new, cw, batch, past):
    n_groups, nb, m = _dec_constants(past)
    nsp = m.shape[1]
    wb = wt.shape[3]
    assert wb == WINDOW
    per_b = lambda shape: pl.BlockSpec((1,) + shape, lambda b: (b,) + (0,) * len(shape))
    kern = functools.partial(_dec_cmp_kernel, n_groups=n_groups, nb=nb, cur=past // SEL_BLOCK)
    return pl.pallas_call(
        kern,
        grid=(batch,),
        in_specs=[per_b((n_groups, CMP_G)), per_b((NSA_HEADS, LANES)),
                  pl.BlockSpec((1, 1, 2 * KV_W, wb), lambda b: (layer, b, 0, 0)),
                  per_b((1, 2 * KV_W))] + _cmp_specs() + [_const_spec((n_groups, nsp))],
        out_specs=[per_b((NSA_HEADS, LANES)), per_b((NSA_HEADS, LANES)),
                   per_b((NSA_KV_HEADS, SEL_TOPK, LANES)), per_b((2 * KV_W, wb))],
        out_shape=[jax.ShapeDtypeStruct((batch, NSA_HEADS, LANES), F32),
                   jax.ShapeDtypeStruct((batch, NSA_HEADS, LANES), F32),
                   jax.ShapeDtypeStruct((batch, NSA_KV_HEADS, SEL_TOPK, LANES), jnp.int32),
                   jax.ShapeDtypeStruct((batch, 2 * KV_W, wb), F32)],
        compiler_params=_params("parallel"),
        name="nsa_decode_cmp",
    )(gsum, q8, wt, win_new, *cw, m)


def _dec_sel_kernel(pt_ref, idx_ref, q_ref, new_ref, oc_ref, ow_ref, gate_ref, ct_ref, out_ref,
                    kv_buf, sems, *, layer, n_pages, n_cache_blocks):
    b = pl.program_id(0)
    buf_slot = b % 2
    nh, nq, topk = NSA_KV_HEADS, NSA_HEADS, SEL_TOPK
    half = PAGE_SIZE // SEL_BLOCK
    nblk = nh * topk

    def page_copies(seq, dst):
        copies = []
        for j in range(nblk):
            blk = jnp.minimum(idx_ref[seq * nblk + j], n_cache_blocks - 1)
            page = pt_ref[seq * n_pages + blk // half]
            copies.append(pltpu.make_async_copy(ct_ref.at[layer, page, pl.ds(2 * KV_W, 2 * KV_W), :],
                                                kv_buf.at[dst, j], sems.at[dst, j]))
        return copies

    @pl.when(b == 0)
    def _():
        for c in page_copies(0, 0):
            c.start()

    @pl.when(b + 1 < pl.num_programs(0))
    def _():
        for c in page_copies(b + 1, 1 - buf_slot):
            c.start()
    for c in page_copies(b, buf_slot):
        c.wait()

    q8 = q_ref[0]
    new = new_ref[0]
    s_new = jnp.sum(q8 * new[:, 2 * KV_W:3 * KV_W].astype(BF16).astype(F32), axis=-1, keepdims=True)
    v_new = new[:, 3 * KV_W:4 * KV_W]
    qb = q8.astype(BF16)
    page_half = lax.broadcasted_iota(jnp.int32, (1, PAGE_SIZE), 1) // SEL_BLOCK
    o_heads = []
    for h in range(nh):
        s_parts = []
        m = s_new
        for k in range(topk):
            j = h * topk + k
            blk = idx_ref[b * nblk + j]
            s_k = _dot(qb, kv_buf[buf_slot, j, 0:KV_W, :].astype(BF16))
            keep = jnp.where(blk < n_cache_blocks, 0.0, NEG)
            s_k = s_k + jnp.where(page_half == blk % half, keep, NEG)
            s_parts.append(s_k)
            m = jnp.maximum(m, jnp.max(s_k, axis=-1, keepdims=True))
        e_new = jnp.exp(s_new - m)
        den = e_new
        num = e_new * v_new
        for k in range(topk):
            e = jnp.exp(s_parts[k] - m)
            den = den + jnp.sum(e, axis=-1, keepdims=True)
            num = num + _dot_nt(e.astype(BF16), kv_buf[buf_slot, h * topk + k, KV_W:2 * KV_W, :].astype(BF16))
        o_heads.append(num / den)
    row = lax.broadcasted_iota(jnp.int32, (nq, KV_W), 0)
    o_s = jnp.where(row < NSA_GROUP, o_heads[0], o_heads[1])
    gt = gate_ref[0]
    out_ref[0] = gt[:, 0:1] * oc_ref[0] + gt[:, 1:2] * o_s + gt[:, 2:3] * ow_ref[0]


def _dec_sel_attn(ct, layer, pt_flat, idx_flat, q8, rows_new, o_c, o_w, gates, batch, n_pages):
    half = PAGE_SIZE // SEL_BLOCK
    nh, grp, nq = NSA_KV_HEADS, NSA_GROUP, NSA_HEADS
    g3 = gates.reshape(batch, nh, LANES)[:, :, 0:3 * grp].reshape(batch, nq, 3)
    g8 = jnp.concatenate([g3, jnp.zeros((batch, nq, LANES - 3), F32)], axis=-1)
    new8 = jnp.broadcast_to(rows_new.reshape(batch, 1, ROWS_W), (batch, nq, ROWS_W))

    per_b = lambda w: pl.BlockSpec((1, nq, w), lambda b, pt, idx: (b, 0, 0))
    grid_spec = pltpu.PrefetchScalarGridSpec(
        num_scalar_prefetch=2,
        grid=(batch,),
        in_specs=[per_b(LANES), per_b(ROWS_W), per_b(LANES), per_b(LANES), per_b(LANES),
                  pl.BlockSpec(memory_space=pl.ANY)],
        out_specs=per_b(LANES),
        scratch_shapes=[pltpu.VMEM((2, nh * SEL_TOPK, 2 * KV_W, PAGE_SIZE), F32),
                        pltpu.SemaphoreType.DMA((2, nh * SEL_TOPK))],
    )
    kern = functools.partial(_dec_sel_kernel, layer=layer, n_pages=n_pages, n_cache_blocks=n_pages * half)
    return pl.pallas_call(
        kern,
        grid_spec=grid_spec,
        out_shape=jax.ShapeDtypeStruct((batch, nq, LANES), F32),
        compiler_params=_params("arbitrary"),
        name="nsa_decode_sel",
    )(pt_flat, idx_flat, q8.astype(F32), new8, o_c, o_w, g8, ct)


def _sample_layer(x, lp, lw, layer, ct, wt, gla_s0, c0, n0, m0, conv0, page_table):
    batch, n_pages = page_table.shape
    past = n_pages * PAGE_SIZE
    d = HEAD_DIM
    c = SCAN_CHUNK_DECODE
    zg, zm, qpad, rows, win, _, _, gates = _nsa_prep(
        None, lp['nsa_qk_g'], lp['nsa_gate_b'], batch,
        proj=(x, lp['norm1_g'].reshape(1, D_MODEL), lw['w_in']))
    pad = lambda z: jnp.pad(z[:, None, :], ((0, 0), (0, c - 1), (0, 0))).reshape(batch * c, z.shape[-1])
    first = lambda o: o.reshape(batch, c, o.shape[-1])[:, 0]
    og, s_gla, om, c_m, n_m, m_m = _gla_mlstm(pad(zg), pad(zm), gla_s0, conv0, c0, n0, m0, lp, batch, c, 1, c)
    pt_flat = page_table.reshape(-1)
    gsum = _cmp_gather(ct, layer, pt_flat, lw['cmp'][0], batch, n_pages)
    q8 = qpad.reshape(batch, NSA_HEADS, LANES)
    o_c, o_w, idx, win_t = _dec_cmp_attn(gsum, q8, wt, layer, win.reshape(batch, 1, 2 * KV_W), lw['cmp'],
                                         batch, past)
    o_n = _dec_sel_attn(ct, layer, pt_flat, idx[:, :, :, 0].reshape(-1), q8, rows, o_c, o_w, gates,
                        batch, n_pages)
    on = jnp.stack([o_n[:, h * NSA_GROUP:(h + 1) * NSA_GROUP, h * d:(h + 1) * d]
                    for h in range(NSA_KV_HEADS)], axis=1)
    on = on.reshape(batch, NSA_W).astype(BF16)
    y = _out_ffn(x, first(og), on, first(om), lw['w_out'], lp['norm2_g'].reshape(1, D_MODEL),
                 lw['w_up'], lw['w_down'], batch)
    new_rows = rows.reshape(batch, 1, 4, NSA_KV_HEADS, d)
    win_state = _rows_on_lanes_to_state(win_t, 2)
    conv_state = jnp.concatenate([conv0.astype(F32), zm[:, None, 0:MLSTM_W]], axis=1)[:, 1:]
    return y, (new_rows, win_state, s_gla, c_m, n_m, m_m, conv_state)


def kernel(x_prompt, x_sample, cache_nsa_kv, cache_nsa_win, state_gla, state_mlstm_c, state_mlstm_n,
           state_mlstm_m, state_mlstm_conv, page_table, norm1_g, w_in, gla_a_w, gla_a_b, gla_norm_g,
           nsa_qk_g, nsa_gate_b, cmp_pe, cmp_w1, cmp_b1, cmp_w2, cmp_b2, ml_conv_w, ml_conv_b, ml_wq, ml_wk,
           ml_gate_b, ml_norm_g, ml_skip, w_out, norm2_g, w_up, w_down):
    bp, t_len, _ = x_prompt.shape
    n_dec = x_sample.shape[0]
    depth = w_in.shape[0]
    x_p = x_prompt.reshape(bp * t_len, D_MODEL)
    x_s = x_sample.reshape(n_dec, D_MODEL)
    states_p, states_s = [], []
    ct = _paged_cache_view(cache_nsa_kv)
    wt = _window_cache_view(cache_nsa_win)
    for l in range(depth):
        lp = {'norm1_g': norm1_g[l], 'w_in': w_in[l], 'gla_a_w': gla_a_w[l], 'gla_a_b': gla_a_b[l],
              'gla_norm_g': gla_norm_g[l], 'nsa_qk_g': nsa_qk_g[l], 'nsa_gate_b': nsa_gate_b[l],
              'cmp_pe': cmp_pe[l], 'cmp_w1': cmp_w1[l], 'cmp_b1': cmp_b1[l], 'cmp_w2': cmp_w2[l],
              'cmp_b2': cmp_b2[l], 'ml_conv_w': ml_conv_w[l], 'ml_conv_b': ml_conv_b[l], 'ml_wq': ml_wq[l],
              'ml_wk': ml_wk[l], 'ml_gate_b': ml_gate_b[l], 'ml_norm_g': ml_norm_g[l], 'ml_skip': ml_skip[l],
              'w_out': w_out[l], 'norm2_g': norm2_g[l], 'w_up': w_up[l], 'w_down': w_down[l]}
        lw = _layer_weights(lp)
        x_p, st_p = _prompt_layer(x_p, lp, lw, bp, t_len)
        x_s, st_s = _sample_layer(x_s, lp, lw, l, ct, wt, state_gla[l],
                                  state_mlstm_c[l], state_mlstm_n[l], state_mlstm_m[l], state_mlstm_conv[l],
                                  page_table)
        states_p.append(st_p)
        states_s.append(st_s)
    outs = [x_p.reshape(bp, t_len, D_MODEL), x_s.reshape(n_dec, 1, D_MODEL)]
    for i in range(7):
        outs.append(jnp.stack([s[i] for s in states_p]))
        outs.append(jnp.stack([s[i] for s in states_s]))
    return tuple(outs)
```
